```python
import jax, jax.numpy as jnp
from jax import lax
import numpy as np

D_MODEL = 1024
BATCH = 4
SEQ = 4096
DEPTH = 1
DEC_BATCH = 8
DEC_SEQ = 8192
PAST_LEN = 128

N_FOURIER_GROUPS = 4
FOURIER_GROUP_DIM = 128
FOURIER_WIDTH = N_FOURIER_GROUPS * FOURIER_GROUP_DIM
N_RET_HEADS = 8
RET_QK_DIM = 64
RET_V_DIM = 128
RET_QK_WIDTH = N_RET_HEADS * RET_QK_DIM
RET_V_WIDTH = N_RET_HEADS * RET_V_DIM
CHUNK = 128
ROPE_THETA = 10000.0
D_FF = 2816
CONV_WIDTH = 3
NORM_EPS = 1e-6
GN_EPS = 1e-5
IN_SPLITS = [FOURIER_WIDTH, RET_QK_WIDTH, RET_QK_WIDTH, RET_V_WIDTH, RET_V_WIDTH, D_MODEL, D_MODEL]
IN_WIDTH = sum(IN_SPLITS)

kernel_name = "hybrid_fnet_retention_convglu_encoder"


def rms_norm(x, g):
    xf = x.astype(jnp.float32)
    y = xf * lax.rsqrt(jnp.mean(xf * xf, axis=-1, keepdims=True) + NORM_EPS)
    return (y * g.astype(jnp.float32)).astype(x.dtype)


def rotary(x):
    s, d = x.shape[1], x.shape[-1]
    inv = ROPE_THETA ** (-jnp.arange(0, d, 2, dtype=jnp.float32) / d)
    ang = jnp.arange(s, dtype=jnp.float32)[:, None] * inv[None, :]
    cos = jnp.cos(ang)[None, :, None, :]
    sin = jnp.sin(ang)[None, :, None, :]
    xf = x.astype(jnp.float32)
    x1, x2 = xf[..., : d // 2], xf[..., d // 2:]
    return jnp.concatenate([x1 * cos - x2 * sin, x1 * sin + x2 * cos], axis=-1)


def fourier_mix(f):
    b, s, _ = f.shape
    fg = f.reshape(b, s, N_FOURIER_GROUPS, FOURIER_GROUP_DIM).astype(jnp.float32)
    y = jnp.fft.fft2(fg, axes=(1, 3), norm="ortho").real
    return y.reshape(b, s, FOURIER_WIDTH).astype(f.dtype)


def retention_direction(q, k, v, log_gamma, include_diag):
    b, s, h, dk = q.shape
    dv = v.shape[-1]
    n_chunks = s // CHUNK

    def chunks(t):
        return t.reshape(b, n_chunks, CHUNK, h, t.shape[-1]).transpose(1, 0, 3, 2, 4)

    qc, kc, vc = chunks(q), chunks(k), chunks(v)
    pos = jnp.arange(CHUNK, dtype=jnp.float32)
    diff = pos[:, None] - pos[None, :]
    mask = (diff >= 0) if include_diag else (diff > 0)
    inner_decay = jnp.where(mask[None], jnp.exp(log_gamma[:, None, None] * jnp.maximum(diff, 0.0)[None]), 0.0)
    q_decay = jnp.exp(log_gamma[:, None] * (pos + 1.0)[None])[..., None]
    k_decay = jnp.exp(log_gamma[:, None] * (CHUNK - 1.0 - pos)[None])[..., None]
    chunk_decay = jnp.exp(log_gamma * CHUNK)[:, None, None]

    def step(state, inp):
        qi, ki, vi = inp
        scores = jnp.einsum('bhqd,bhkd->bhqk', qi, ki) * inner_decay
        o = jnp.einsum('bhqk,bhke->bhqe', scores, vi) + jnp.einsum('bhqd,bhde->bhqe', qi * q_decay, state)
        state = chunk_decay * state + jnp.einsum('bhkd,bhke->bhde', ki * k_decay, vi)
        return state, o

    s0 = jnp.zeros((b, h, dk, dv), jnp.float32)
    _, o = lax.scan(step, s0, (qc, kc, vc))
    return o.transpose(1, 0, 3, 2, 4).reshape(b, s, h, dv)


def bidirectional_retention(q, k, v, g, decay_logit):
    b, s = q.shape[0], q.shape[1]
    log_gamma = jax.nn.log_sigmoid(decay_logit.astype(jnp.float32))
    qr = rotary(q) * (RET_QK_DIM ** -0.5)
    kr = rotary(k)
    vf = v.reshape(b, s, N_RET_HEADS, RET_V_DIM).astype(jnp.float32)
    o_fwd = retention_direction(qr, kr, vf, log_gamma[0], True)
    o_bwd = retention_direction(qr[:, ::-1], kr[:, ::-1], vf[:, ::-1], log_gamma[1], False)[:, ::-1]
    o = o_fwd + o_bwd
    mu = jnp.mean(o, axis=-1, keepdims=True)
    var = jnp.mean(jnp.square(o - mu), axis=-1, keepdims=True)
    o = (o - mu) * lax.rsqrt(var + GN_EPS)
    return o.reshape(b, s, RET_V_WIDTH).astype(v.dtype) * jax.nn.silu(g)


def encoder_layer(x, norm1_g, w_in, w_four_proj, w_ret_proj, w_out, ret_decay_logit,
                  norm2_g, w_up, conv_w, conv_b, w_down):
    b, s, _ = x.shape
    u = rms_norm(x, norm1_g)
    proj = u @ w_in
    f, q, k, v, g_ret, g_a, g_b = jnp.split(proj, list(np.cumsum(IN_SPLITS[:-1])), axis=-1)
    branch_a = fourier_mix(f) @ w_four_proj
    q = q.reshape(b, s, N_RET_HEADS, RET_QK_DIM)
    k = k.reshape(b, s, N_RET_HEADS, RET_QK_DIM)
    branch_b = bidirectional_retention(q, k, v, g_ret, ret_decay_logit) @ w_ret_proj
    merged = jax.nn.sigmoid(g_a) * branch_a + jax.nn.sigmoid(g_b) * branch_b
    x = x + merged @ w_out
    u2 = rms_norm(x, norm2_g)
    h_gate, h_val = jnp.split(u2 @ w_up, 2, axis=-1)
    hp = jnp.pad(h_gate, ((0, 0), (1, 1), (0, 0)))
    h_conv = hp[:, :-2] * conv_w[0] + hp[:, 1:-1] * conv_w[1] + hp[:, 2:] * conv_w[2] + conv_b
    x = x + (jax.nn.gelu(h_conv, approximate=False) * h_val) @ w_down
    return x


def setup_inputs(seed: int = 0) -> dict:
    key = jax.random.key(seed)
    ks = jax.random.split(key, 16)
    f32 = jnp.float32
    gamma0 = 1.0 - 2.0 ** (-5.0 - np.arange(N_RET_HEADS, dtype=np.float32))
    logit0 = jnp.asarray(np.log(gamma0 / (1.0 - gamma0)).astype(np.float32))
    decay_logit = logit0[None, None, :] + 0.05 * jax.random.normal(ks[7], (DEPTH, 2, N_RET_HEADS), f32)
    return {
        "x_prompt": jax.random.normal(ks[0], (BATCH, SEQ, D_MODEL), f32),
        "x_sample": jax.random.normal(ks[1], (DEC_BATCH, DEC_SEQ, D_MODEL), f32),
        "norm1_g": 1.0 + 0.02 * jax.random.normal(ks[2], (DEPTH, D_MODEL), f32),
        "w_in": jax.random.normal(ks[3], (DEPTH, D_MODEL, IN_WIDTH), f32) * D_MODEL ** -0.5,
        "w_four_proj": jax.random.normal(ks[4], (DEPTH, FOURIER_WIDTH, D_MODEL), f32) * FOURIER_WIDTH ** -0.5,
        "w_ret_proj": jax.random.normal(ks[5], (DEPTH, RET_V_WIDTH, D_MODEL), f32) * RET_V_WIDTH ** -0.5,
        "w_out": jax.random.normal(ks[6], (DEPTH, D_MODEL, D_MODEL), f32) * D_MODEL ** -0.5,
        "ret_decay_logit": decay_logit,
        "norm2_g": 1.0 + 0.02 * jax.random.normal(ks[8], (DEPTH, D_MODEL), f32),
        "w_up": jax.random.normal(ks[9], (DEPTH, D_MODEL, 2 * D_FF), f32) * D_MODEL ** -0.5,
        "conv_w": jax.random.normal(ks[10], (DEPTH, CONV_WIDTH, D_FF), f32) * CONV_WIDTH ** -0.5,
        "conv_b": 0.01 * jax.random.normal(ks[11], (DEPTH, D_FF), f32),
        "w_down": jax.random.normal(ks[12], (DEPTH, D_FF, D_MODEL), f32) * D_FF ** -0.5,
        "final_norm_g": 1.0 + 0.02 * jax.random.normal(ks[13], (D_MODEL,), f32),
    }


def reference(x_prompt, x_sample, norm1_g, w_in, w_four_proj, w_ret_proj, w_out, ret_decay_logit,
              norm2_g, w_up, conv_w, conv_b, w_down, final_norm_g):
    def trunk(x):
        for l in range(DEPTH):
            x = encoder_layer(x, norm1_g[l], w_in[l], w_four_proj[l], w_ret_proj[l], w_out[l],
                              ret_decay_logit[l], norm2_g[l], w_up[l], conv_w[l], conv_b[l], w_down[l])
        return rms_norm(x, final_norm_g)

    y_prompt = trunk(x_prompt)
    y_sample = trunk(x_sample)
    return (y_prompt, y_sample)
```

```python
import functools

import numpy as np
import jax
import jax.numpy as jnp
from jax import lax
from jax.experimental import pallas as pl
from jax.experimental.pallas import tpu as pltpu

D_MODEL = 1024
N_FOURIER_GROUPS = 4
FOURIER_GROUP_DIM = 128
FOURIER_WIDTH = N_FOURIER_GROUPS * FOURIER_GROUP_DIM
N_RET_HEADS = 8
RET_QK_DIM = 64
RET_V_DIM = 128
RET_QK_WIDTH = N_RET_HEADS * RET_QK_DIM
RET_V_WIDTH = N_RET_HEADS * RET_V_DIM
CHUNK = 128
ROPE_THETA = 10000.0
D_FF = 2816
NORM_EPS = 1e-6
GN_EPS = 1e-5
IN_SPLITS = (FOURIER_WIDTH, RET_QK_WIDTH, RET_QK_WIDTH, RET_V_WIDTH, RET_V_WIDTH, D_MODEL, D_MODEL)
IN_OFFSETS = tuple(int(o) for o in np.cumsum((0,) + IN_SPLITS))
IN_WIDTH = IN_OFFSETS[-1]

LANES = 128
SUBLANES = 8
VMEM_LIMIT_BYTES = 56 * 1024 * 1024

F32 = jnp.float32
BF16 = jnp.bfloat16

TOKEN_TILE = 512
FF_CHUNK = 256
FFT_N1 = 128
PITCH_PAD = SUBLANES


def _resident(shape):
    return pl.BlockSpec(shape, lambda *_: (0,) * len(shape), pipeline_mode=pl.Buffered(1))


def _rms(x, g):
    ms = jnp.mean(x * x, axis=-1, keepdims=True)
    return x * lax.rsqrt(ms + NORM_EPS) * g


def _inproj_kernel(x_ref, g_ref, w_ref, cos_ref, sin_ref,
                   f_ref, q_ref, k_ref, v_ref, gr_ref, ga_ref, gb_ref):
    u = _rms(x_ref[0], g_ref[...]).astype(BF16)

    def proj(i):
        return jnp.dot(u, w_ref[:, IN_OFFSETS[i]:IN_OFFSETS[i + 1]], preferred_element_type=F32)

    reps = RET_QK_WIDTH // LANES
    cos = jnp.concatenate([cos_ref[...]] * reps, axis=1)
    sin = jnp.concatenate([sin_ref[...]] * reps, axis=1)
    lane = lax.broadcasted_iota(jnp.int32, cos.shape, 1)
    first_half = (lane % RET_QK_DIM) < (RET_QK_DIM // 2)

    def rotary(t):
        partner = jnp.where(first_half,
                            pltpu.roll(t, RET_QK_WIDTH - RET_QK_DIM // 2, 1),
                            pltpu.roll(t, RET_QK_DIM // 2, 1))
        return t * cos + partner * sin

    f_ref[0] = proj(0).astype(BF16)
    q_ref[0] = (rotary(proj(1)) * (RET_QK_DIM ** -0.5)).astype(BF16)
    k_ref[0] = rotary(proj(2)).astype(BF16)
    v_ref[0] = proj(3).astype(BF16)
    gr_ref[0] = proj(4).astype(BF16)
    ga_ref[0] = proj(5).astype(BF16)
    gb_ref[0] = proj(6).astype(BF16)


def _inproj(x, norm_g, w_in_bf16, cos_tab, sin_tab):
    b, s, d = x.shape
    tm = TOKEN_TILE
    widths = IN_SPLITS
    out_shape = [jax.ShapeDtypeStruct((b, s, w), BF16) for w in widths]
    out_specs = [pl.BlockSpec((1, tm, w), lambda i, j: (i, j, 0)) for w in widths]
    return pl.pallas_call(
        _inproj_kernel,
        grid=(b, s // tm),
        in_specs=[
            pl.BlockSpec((1, tm, d), lambda i, j: (i, j, 0)),
            _resident((1, d)),
            _resident((d, IN_WIDTH)),
            pl.BlockSpec((tm, LANES), lambda i, j: (j, 0)),
            pl.BlockSpec((tm, LANES), lambda i, j: (j, 0)),
        ],
        out_specs=out_specs,
        out_shape=out_shape,
        compiler_params=pltpu.CompilerParams(
            dimension_semantics=("parallel", "parallel"), vmem_limit_bytes=VMEM_LIMIT_BYTES),
        name="inproj",
    )(x, norm_g, w_in_bf16, cos_tab, sin_tab)


def _fft_kernel(x_ref, fc_ref, f1_ref, g2_ref, o_ref, z_ref, y_ref, *, n1, n2):
    p1 = n2 + PITCH_PAD
    p2 = 2 * n1 + PITCH_PAD
    gd = FOURIER_GROUP_DIM

    def stage0(i, carry):
        xs = x_ref[0, pl.ds(pl.multiple_of(i * n2, n2), n2), :]
        z = jnp.dot(xs, fc_ref[...], preferred_element_type=F32)
        row = pl.multiple_of(i * p1, SUBLANES)
        z_ref[0, pl.ds(row, n2), :] = z[:, :gd]
        z_ref[1, pl.ds(row, n2), :] = z[:, gd:]
        return carry

    lax.fori_loop(0, n1, stage0, 0)

    def stage1(j, carry):
        zr = z_ref[0, pl.ds(j, n1, stride=p1), :]
        zi = z_ref[1, pl.ds(j, n1, stride=p1), :]
        rhs = jnp.concatenate([zr, zi], axis=0).astype(BF16)
        y = jnp.dot(f1_ref[...], rhs, preferred_element_type=F32)
        y_ref[pl.ds(pl.multiple_of(j * p2, SUBLANES), 2 * n1), :] = y
        return carry

    lax.fori_loop(0, n2, stage1, 0)

    def stage2(k1, carry):
        yr = y_ref[pl.ds(k1, n2, stride=p2), :]
        yi = y_ref[pl.ds(n1 + k1, n2, stride=p2), :]
        rhs = jnp.concatenate([yr, yi], axis=0).astype(BF16)
        out = jnp.dot(g2_ref[k1], rhs, preferred_element_type=F32)
        o_ref[0, pl.ds(k1, n2, stride=n1), :] = out
        return carry

    lax.fori_loop(0, n1, stage2, 0)


def _fft_tables(s, n1, n2):
    c = FOURIER_GROUP_DIM
    ic = jnp.arange(c, dtype=jnp.int32)
    ang_c = (2.0 * np.pi / c) * ((ic[:, None] * ic[None, :]) % c).astype(F32)
    scale_c = c ** -0.5
    fc = jnp.concatenate([jnp.cos(ang_c), -jnp.sin(ang_c)], axis=1) * scale_c
    i1 = jnp.arange(n1, dtype=jnp.int32)
    ang_1 = (2.0 * np.pi / n1) * ((i1[:, None] * i1[None, :]) % n1).astype(F32)
    c1, s1 = jnp.cos(ang_1), jnp.sin(ang_1)
    f1 = jnp.concatenate([jnp.concatenate([c1, s1], axis=1),
                          jnp.concatenate([-s1, c1], axis=1)], axis=0) * (n1 ** -0.5)
    k1 = jnp.arange(n1, dtype=jnp.int32)[:, None, None]
    k2 = jnp.arange(n2, dtype=jnp.int32)[None, :, None]
    m2 = jnp.arange(n2, dtype=jnp.int32)[None, None, :]
    ang_2 = (2.0 * np.pi / s) * ((m2 * (k1 + n1 * k2)) % s).astype(F32)
    g2 = jnp.concatenate([jnp.cos(ang_2), jnp.sin(ang_2)], axis=2) * (n2 ** -0.5)
    return fc.astype(BF16), f1.astype(BF16), g2.astype(BF16)


def _fourier_mix(f):
    b, s, _ = f.shape
    n1 = FFT_N1
    n2 = s // n1
    fc, f1, g2 = _fft_tables(s, n1, n2)
    gd = FOURIER_GROUP_DIM
    p1 = n2 + PITCH_PAD
    p2 = 2 * n1 + PITCH_PAD
    return pl.pallas_call(
        functools.partial(_fft_kernel, n1=n1, n2=n2),
        grid=(b, N_FOURIER_GROUPS),
        in_specs=[
            pl.BlockSpec((1, s, gd), lambda i, g: (i, 0, g)),
            _resident((gd, 2 * gd)),
            _resident((2 * n1, 2 * n1)),
            _resident((n1, n2, 2 * n2)),
        ],
        out_specs=pl.BlockSpec((1, s, gd), lambda i, g: (i, 0, g)),
        out_shape=jax.ShapeDtypeStruct((b, s, FOURIER_WIDTH), F32),
        scratch_shapes=[
            pltpu.VMEM((2, n1 * p1, gd), F32),
            pltpu.VMEM((n2 * p2, gd), F32),
        ],
        compiler_params=pltpu.CompilerParams(
            dimension_semantics=("parallel", "parallel"), vmem_limit_bytes=VMEM_LIMIT_BYTES),
        name="fourier_mix",
    )(f, fc, f1, g2)


def _ret_kernel(lg_ref, q_ref, k_ref, v_ref, g_ref, o_ref, sb_ref, sf_ref, sr_ref, *, n_chunks):
    c = CHUNK
    dk2 = 2 * RET_QK_DIM
    dv = RET_V_DIM
    pair = pl.program_id(1)
    lgf = [lg_ref[0, 2 * pair + h] for h in range(2)]
    lgb = [lg_ref[1, 2 * pair + h] for h in range(2)]

    row = lax.broadcasted_iota(jnp.int32, (c, c), 0)
    col = lax.broadcasted_iota(jnp.int32, (c, c), 1)
    diff = (row - col).astype(F32)
    decay = [jnp.where(diff >= 0.0,
                       jnp.exp(lgf[h] * jnp.maximum(diff, 0.0)),
                       jnp.exp(lgb[h] * jnp.maximum(-diff, 0.0))) for h in range(2)]

    pos = lax.broadcasted_iota(jnp.int32, (c, dk2), 0).astype(F32)
    head_a = lax.broadcasted_iota(jnp.int32, (c, dk2), 1) < RET_QK_DIM

    def per_head(fa, fb):
        return jnp.where(head_a, fa, fb)

    q_dec_f = per_head(jnp.exp(lgf[0] * (pos + 1.0)), jnp.exp(lgf[1] * (pos + 1.0)))
    q_dec_b = per_head(jnp.exp(lgb[0] * (c - pos)), jnp.exp(lgb[1] * (c - pos)))
    k_dec_f = per_head(jnp.exp(lgf[0] * (c - 1.0 - pos)), jnp.exp(lgf[1] * (c - 1.0 - pos)))
    k_dec_b = per_head(jnp.exp(lgb[0] * pos), jnp.exp(lgb[1] * pos))

    srow = lax.broadcasted_iota(jnp.int32, (dk2, 2 * dv), 0) < RET_QK_DIM
    scol = lax.broadcasted_iota(jnp.int32, (dk2, 2 * dv), 1) < dv
    block_diag = srow == scol
    chunk_f = jnp.where(scol, jnp.exp(lgf[0] * c), jnp.exp(lgf[1] * c))
    chunk_b = jnp.where(scol, jnp.exp(lgb[0] * c), jnp.exp(lgb[1] * c))

    def kv_update(state, k_dec, chunk_dec, kc, vc):
        kd = (kc.astype(F32) * k_dec).astype(BF16)
        kv = lax.dot_general(kd, vc, (((0,), (0,)), ((), ())), preferred_element_type=F32)
        return chunk_dec * state + jnp.where(block_diag, kv, 0.0)

    sr_ref[...] = jnp.zeros_like(sr_ref)

    def reverse_step(t, carry):
        i = n_chunks - 1 - t
        r0 = pl.multiple_of(i * c, c)
        sb_ref[i] = sr_ref[...].astype(BF16)
        sr_ref[...] = kv_update(sr_ref[...], k_dec_b, chunk_b,
                                k_ref[0, pl.ds(r0, c), :], v_ref[0, pl.ds(r0, c), :])
        return carry

    lax.fori_loop(0, n_chunks, reverse_step, 0)

    sf_ref[...] = jnp.zeros_like(sf_ref)

    def forward_step(i, carry):
        r0 = pl.multiple_of(i * c, c)
        qc = q_ref[0, pl.ds(r0, c), :]
        kc = k_ref[0, pl.ds(r0, c), :]
        vc = v_ref[0, pl.ds(r0, c), :]
        zero = jnp.zeros_like(kc)
        k_heads = jnp.concatenate([jnp.where(head_a, kc, zero), jnp.where(head_a, zero, kc)], axis=0)
        scores = lax.dot_general(qc, k_heads, (((1,), (1,)), ((), ())),
                                 preferred_element_type=F32)
        qf = qc.astype(F32)
        q_inter = jnp.concatenate([(qf * q_dec_f).astype(BF16), (qf * q_dec_b).astype(BF16)], axis=1)
        states = jnp.concatenate([sf_ref[...].astype(BF16), sb_ref[i]], axis=0)
        inter = jnp.dot(q_inter, states, preferred_element_type=F32)
        outs = []
        for h in range(2):
            p = (scores[:, h * c:(h + 1) * c] * decay[h]).astype(BF16)
            o = jnp.dot(p, vc[:, h * dv:(h + 1) * dv], preferred_element_type=F32)
            o = o + inter[:, h * dv:(h + 1) * dv]
            mu = jnp.mean(o, axis=-1, keepdims=True)
            d = o - mu
            var = jnp.mean(d * d, axis=-1, keepdims=True)
            outs.append(d * lax.rsqrt(var + GN_EPS))
        gate = g_ref[0, pl.ds(r0, c), :].astype(F32)
        o_ref[0, pl.ds(r0, c), :] = (jnp.concatenate(outs, axis=1) * (gate * jax.nn.sigmoid(gate))).astype(BF16)
        sf_ref[...] = kv_update(sf_ref[...], k_dec_f, chunk_f, kc, vc)
        return carry

    lax.fori_loop(0, n_chunks, forward_step, 0)


def _retention(q, k, v, g, log_gamma):
    b, s, _ = q.shape
    n_chunks = s // CHUNK
    n_pairs = N_RET_HEADS // 2
    dk2 = 2 * RET_QK_DIM
    dv2 = 2 * RET_V_DIM
    return pl.pallas_call(
        functools.partial(_ret_kernel, n_chunks=n_chunks),
        grid=(b, n_pairs),
        in_specs=[
            pl.BlockSpec(memory_space=pltpu.SMEM),
            pl.BlockSpec((1, s, dk2), lambda i, p: (i, 0, p)),
            pl.BlockSpec((1, s, dk2), lambda i, p: (i, 0, p)),
            pl.BlockSpec((1, s, dv2), lambda i, p: (i, 0, p)),
            pl.BlockSpec((1, s, dv2), lambda i, p: (i, 0, p)),
        ],
        out_specs=pl.BlockSpec((1, s, dv2), lambda i, p: (i, 0, p)),
        out_shape=jax.ShapeDtypeStruct((b, s, RET_V_WIDTH), BF16),
        scratch_shapes=[
            pltpu.VMEM((n_chunks, dk2, dv2), BF16),
            pltpu.VMEM((dk2, dv2), F32),
            pltpu.VMEM((dk2, dv2), F32),
        ],
        compiler_params=pltpu.CompilerParams(
            dimension_semantics=("parallel", "parallel"), vmem_limit_bytes=VMEM_LIMIT_BYTES),
        name="retention",
    )(log_gamma, q, k, v, g)


def _merge_kernel(x_ref, yf_ref, ret_ref, ga_ref, gb_ref, wf_ref, wr_ref, wo_ref, o_ref):
    a = jnp.dot(yf_ref[...].astype(BF16), wf_ref[...], preferred_element_type=F32)
    bb = jnp.dot(ret_ref[...], wr_ref[...], preferred_element_type=F32)
    merged = (jax.nn.sigmoid(ga_ref[...].astype(F32)) * a
              + jax.nn.sigmoid(gb_ref[...].astype(F32)) * bb)
    o_ref[...] = x_ref[...] + jnp.dot(merged.astype(BF16), wo_ref[...], preferred_element_type=F32)


def _merge(x2d, yf2d, ret2d, ga2d, gb2d, wf, wr, wo):
    t, d = x2d.shape
    tm = TOKEN_TILE

    def tile(w):
        return pl.BlockSpec((tm, w), lambda i: (i, 0))

    return pl.pallas_call(
        _merge_kernel,
        grid=(t // tm,),
        in_specs=[tile(d), tile(FOURIER_WIDTH), tile(RET_V_WIDTH), tile(d), tile(d),
                  _resident(wf.shape), _resident(wr.shape), _resident(wo.shape)],
        out_specs=tile(d),
        out_shape=jax.ShapeDtypeStruct((t, d), F32),
        compiler_params=pltpu.CompilerParams(
            dimension_semantics=("parallel",), vmem_limit_bytes=VMEM_LIMIT_BYTES),
        name="merge",
    )(x2d, yf2d, ret2d, ga2d, gb2d, wf, wr, wo)


def _ffn_kernel(xp_ref, x_ref, xn_ref, g2_ref, wu_ref, cw_ref, cb_ref, wd_ref, gf_ref, o_ref, acc_ref):
    tm = x_ref.shape[1]
    halo = SUBLANES
    rows = tm + 2 * halo
    j = pl.program_id(1)
    last = pl.num_programs(1) - 1
    x = x_ref[0]
    xx = jnp.concatenate([xp_ref[0], x, xn_ref[0]], axis=0)
    u = _rms(xx, g2_ref[...]).astype(BF16)
    u_mid = u[halo:halo + tm]
    ridx = lax.broadcasted_iota(jnp.int32, (rows, 1), 0)
    keep = jnp.logical_and(jnp.logical_or(ridx >= halo, j > 0),
                           jnp.logical_or(ridx < halo + tm, j < last)).astype(F32)

    acc_ref[...] = jnp.zeros_like(acc_ref)
    for ci in range(D_FF // FF_CHUNK):
        lo = ci * FF_CHUNK
        hg = jnp.dot(u, wu_ref[:, lo:lo + FF_CHUNK], preferred_element_type=F32) * keep
        hv = jnp.dot(u_mid, wu_ref[:, D_FF + lo:D_FF + lo + FF_CHUNK], preferred_element_type=F32)
        cw = cw_ref[:, lo:lo + FF_CHUNK]
        prev = pltpu.roll(hg, 1, 0)[halo:halo + tm]
        nxt = pltpu.roll(hg, rows - 1, 0)[halo:halo + tm]
        hc = prev * cw[0:1] + hg[halo:halo + tm] * cw[1:2] + nxt * cw[2:3] + cb_ref[:, lo:lo + FF_CHUNK]
        act = 0.5 * hc * (1.0 + lax.erf(hc * (2.0 ** -0.5)))
        acc_ref[...] += jnp.dot((act * hv).astype(BF16), wd_ref[lo:lo + FF_CHUNK, :],
                                preferred_element_type=F32)
    o_ref[0] = _rms(x + acc_ref[...], gf_ref[...])


def _ffn(x1, norm2_g, w_up, conv_w, conv_b, w_down, final_g):
    b, s, d = x1.shape
    tm = TOKEN_TILE
    hb = tm // SUBLANES
    n_hblocks = s // SUBLANES
    return pl.pallas_call(
        _ffn_kernel,
        grid=(b, s // tm),
        in_specs=[
            pl.BlockSpec((1, SUBLANES, d), lambda i, j: (i, jnp.maximum(j * hb - 1, 0), 0)),
            pl.BlockSpec((1, tm, d), lambda i, j: (i, j, 0)),
            pl.BlockSpec((1, SUBLANES, d), lambda i, j: (i, jnp.minimum((j + 1) * hb, n_hblocks - 1), 0)),
            _resident((1, d)),
            _resident(w_up.shape),
            _resident(conv_w.shape),
            _resident(conv_b.shape),
            _resident(w_down.shape),
            _resident((1, d)),
        ],
        out_specs=pl.BlockSpec((1, tm, d), lambda i, j: (i, j, 0)),
        out_shape=jax.ShapeDtypeStruct((b, s, d), F32),
        scratch_shapes=[pltpu.VMEM((tm, d), F32)],
        compiler_params=pltpu.CompilerParams(
            dimension_semantics=("parallel", "parallel"), vmem_limit_bytes=VMEM_LIMIT_BYTES),
        name="convglu_ffn",
    )(x1, x1, x1, norm2_g, w_up, conv_w, conv_b, w_down, final_g)


def _rotary_tables(s):
    half = RET_QK_DIM // 2
    inv = ROPE_THETA ** (-jnp.arange(0, RET_QK_DIM, 2, dtype=F32) / RET_QK_DIM)
    ang = jnp.arange(s, dtype=F32)[:, None] * inv[None, :]
    cos, sin = jnp.cos(ang), jnp.sin(ang)
    reps = LANES // RET_QK_DIM
    cos_tab = jnp.concatenate([cos, cos] * reps, axis=1)
    sin_tab = jnp.concatenate([-sin, sin] * reps, axis=1)
    assert cos_tab.shape == (s, LANES) and half * 2 * reps == LANES
    return cos_tab, sin_tab


def _encoder_layer(x, p):
    b, s, d = x.shape
    cos_tab, sin_tab = _rotary_tables(s)
    f, q, k, v, g_ret, g_a, g_b = _inproj(x, p["norm1_g"], p["w_in"], cos_tab, sin_tab)
    y_four = _fourier_mix(f)
    y_ret = _retention(q, k, v, g_ret, p["log_gamma"])
    t = b * s
    x1 = _merge(x.reshape(t, d), y_four.reshape(t, FOURIER_WIDTH), y_ret.reshape(t, RET_V_WIDTH),
                g_a.reshape(t, d), g_b.reshape(t, d), p["w_four_proj"], p["w_ret_proj"], p["w_out"])
    return x1.reshape(b, s, d)


def kernel(x_prompt, x_sample, norm1_g, w_in, w_four_proj, w_ret_proj, w_out, ret_decay_logit,
           norm2_g, w_up, conv_w, conv_b, w_down, final_norm_g):
    depth = w_in.shape[0]
    assert depth == 1, "the final RMSNorm is fused into the (single) layer's FFN kernel"
    layers = []
    for l in range(depth):
        layers.append(dict(
            norm1_g=norm1_g[l][None, :],
            w_in=w_in[l].astype(BF16),
            w_four_proj=w_four_proj[l].astype(BF16),
            w_ret_proj=w_ret_proj[l].astype(BF16),
            w_out=w_out[l].astype(BF16),
            log_gamma=jax.nn.log_sigmoid(ret_decay_logit[l].astype(F32)),
            norm2_g=norm2_g[l][None, :],
            w_up=w_up[l].astype(BF16),
            conv_w=conv_w[l],
            conv_b=conv_b[l][None, :],
            w_down=w_down[l].astype(BF16),
        ))
    final_g = final_norm_g[None, :]

    def trunk(x):
        p = layers[0]
        x1 = _encoder_layer(x, p)
        return _ffn(x1, p["norm2_g"], p["w_up"], p["conv_w"], p["conv_b"], p["w_down"], final_g)

    return (trunk(x_prompt), trunk(x_sample))
```

```python
import functools

import numpy as np
import jax
import jax.numpy as jnp
from jax import lax
from jax.experimental import pallas as pl
from jax.experimental.pallas import tpu as pltpu

D_MODEL = 1024
N_FOURIER_GROUPS = 4
FOURIER_GROUP_DIM = 128
FOURIER_WIDTH = N_FOURIER_GROUPS * FOURIER_GROUP_DIM
N_RET_HEADS = 8
RET_QK_DIM = 64
RET_V_DIM = 128
RET_QK_WIDTH = N_RET_HEADS * RET_QK_DIM
RET_V_WIDTH = N_RET_HEADS * RET_V_DIM
CHUNK = 128
ROPE_THETA = 10000.0
D_FF = 2816
NORM_EPS = 1e-6
GN_EPS = 1e-5
IN_SPLITS = (FOURIER_WIDTH, RET_QK_WIDTH, RET_QK_WIDTH, RET_V_WIDTH, RET_V_WIDTH, D_MODEL, D_MODEL)
IN_OFFSETS = tuple(int(o) for o in np.cumsum((0,) + IN_SPLITS))
IN_WIDTH = IN_OFFSETS[-1]

LANES = 128
SUBLANES = 8
VMEM_LIMIT_BYTES = 56 * 1024 * 1024

F32 = jnp.float32
BF16 = jnp.bfloat16

TOKEN_TILE = 512
FF_CHUNK = 256
FFT_N1 = 128
FFT_STAGE0_ROWS = 512
PITCH_PAD = SUBLANES


def _resident(shape):
    return pl.BlockSpec(shape, lambda *_: (0,) * len(shape), pipeline_mode=pl.Buffered(1))


def _rms(x, g):
    ms = jnp.mean(x * x, axis=-1, keepdims=True)
    return x * lax.rsqrt(ms + NORM_EPS) * g


def _inproj_kernel(x_ref, g_ref, w_ref, cos_ref, sin_ref,
                   f_ref, q_ref, k_ref, v_ref, gr_ref, ga_ref, gb_ref):
    u = _rms(x_ref[0], g_ref[...]).astype(BF16)

    def proj(i):
        return jnp.dot(u, w_ref[:, IN_OFFSETS[i]:IN_OFFSETS[i + 1]], preferred_element_type=F32)

    reps = RET_QK_WIDTH // LANES
    cos = jnp.concatenate([cos_ref[...]] * reps, axis=1)
    sin = jnp.concatenate([sin_ref[...]] * reps, axis=1)
    lane = lax.broadcasted_iota(jnp.int32, cos.shape, 1)
    first_half = (lane % RET_QK_DIM) < (RET_QK_DIM // 2)

    def rotary(t):
        partner = jnp.where(first_half,
                            pltpu.roll(t, RET_QK_WIDTH - RET_QK_DIM // 2, 1),
                            pltpu.roll(t, RET_QK_DIM // 2, 1))
        return t * cos + partner * sin

    f_ref[0] = proj(0).astype(BF16)
    q_ref[0] = (rotary(proj(1)) * (RET_QK_DIM ** -0.5)).astype(BF16)
    k_ref[0] = rotary(proj(2)).astype(BF16)
    v_ref[0] = proj(3).astype(BF16)
    gr_ref[0] = proj(4).astype(BF16)
    ga_ref[0] = proj(5).astype(BF16)
    gb_ref[0] = proj(6).astype(BF16)


def _inproj(x, norm_g, w_in_bf16, cos_tab, sin_tab):
    b, s, d = x.shape
    tm = TOKEN_TILE
    widths = IN_SPLITS
    out_shape = [jax.ShapeDtypeStruct((b, s, w), BF16) for w in widths]
    out_specs = [pl.BlockSpec((1, tm, w), lambda i, j: (i, j, 0)) for w in widths]
    return pl.pallas_call(
        _inproj_kernel,
        grid=(b, s // tm),
        in_specs=[
            pl.BlockSpec((1, tm, d), lambda i, j: (i, j, 0)),
            _resident((1, d)),
            _resident((d, IN_WIDTH)),
            pl.BlockSpec((tm, LANES), lambda i, j: (j, 0)),
            pl.BlockSpec((tm, LANES), lambda i, j: (j, 0)),
        ],
        out_specs=out_specs,
        out_shape=out_shape,
        compiler_params=pltpu.CompilerParams(
            dimension_semantics=("parallel", "parallel"), vmem_limit_bytes=VMEM_LIMIT_BYTES),
        name="inproj",
    )(x, norm_g, w_in_bf16, cos_tab, sin_tab)


def _fft_kernel(x_ref, fc_ref, f1_ref, g2_ref, o_ref, z_ref, y_ref, *, n1, n2):
    p1 = n2 + PITCH_PAD
    p2 = 2 * n1 + PITCH_PAD
    gd = FOURIER_GROUP_DIM

    slabs = FFT_STAGE0_ROWS // n2

    def stage0(i, carry):
        r0 = pl.multiple_of(i * FFT_STAGE0_ROWS, FFT_STAGE0_ROWS)
        z = jnp.dot(x_ref[0, pl.ds(r0, FFT_STAGE0_ROWS), :], fc_ref[...],
                    preferred_element_type=F32)
        for s in range(slabs):
            row = pl.multiple_of((i * slabs + s) * p1, SUBLANES)
            z_ref[0, pl.ds(row, n2), :] = z[s * n2:(s + 1) * n2, :gd]
            z_ref[1, pl.ds(row, n2), :] = z[s * n2:(s + 1) * n2, gd:]
        return carry

    lax.fori_loop(0, (n1 * n2) // FFT_STAGE0_ROWS, stage0, 0, unroll=2)

    def stage1(jj, carry):
        cols = []
        for t in range(2):
            j = 2 * jj + t
            zr = z_ref[0, pl.ds(j, n1, stride=p1), :]
            zi = z_ref[1, pl.ds(j, n1, stride=p1), :]
            cols.append(jnp.concatenate([zr, zi], axis=0).astype(BF16))
        rhs = jnp.concatenate(cols, axis=1)
        y = jnp.dot(f1_ref[...], rhs, preferred_element_type=F32)
        for t in range(2):
            row = pl.multiple_of((2 * jj + t) * p2, SUBLANES)
            y_ref[pl.ds(row, 2 * n1), :] = y[:, t * gd:(t + 1) * gd]
        return carry

    lax.fori_loop(0, n2 // 2, stage1, 0, unroll=4)

    def stage2(k1, carry):
        yr = y_ref[pl.ds(k1, n2, stride=p2), :]
        yi = y_ref[pl.ds(n1 + k1, n2, stride=p2), :]
        rhs = jnp.concatenate([yr, yi], axis=0).astype(BF16)
        o_ref[0, 0, k1] = jnp.dot(g2_ref[k1], rhs, preferred_element_type=F32)
        return carry

    lax.fori_loop(0, n1, stage2, 0, unroll=8)


def _fft_tables(s, n1, n2):
    c = FOURIER_GROUP_DIM
    ic = jnp.arange(c, dtype=jnp.int32)
    ang_c = (2.0 * np.pi / c) * ((ic[:, None] * ic[None, :]) % c).astype(F32)
    scale_c = c ** -0.5
    fc = jnp.concatenate([jnp.cos(ang_c), -jnp.sin(ang_c)], axis=1) * scale_c
    i1 = jnp.arange(n1, dtype=jnp.int32)
    ang_1 = (2.0 * np.pi / n1) * ((i1[:, None] * i1[None, :]) % n1).astype(F32)
    c1, s1 = jnp.cos(ang_1), jnp.sin(ang_1)
    f1 = jnp.concatenate([jnp.concatenate([c1, s1], axis=1),
                          jnp.concatenate([-s1, c1], axis=1)], axis=0) * (n1 ** -0.5)
    k1 = jnp.arange(n1, dtype=jnp.int32)[:, None, None]
    k2 = jnp.arange(n2, dtype=jnp.int32)[None, :, None]
    m2 = jnp.arange(n2, dtype=jnp.int32)[None, None, :]
    ang_2 = (2.0 * np.pi / s) * ((m2 * (k1 + n1 * k2)) % s).astype(F32)
    g2 = jnp.concatenate([jnp.cos(ang_2), jnp.sin(ang_2)], axis=2) * (n2 ** -0.5)
    return fc.astype(BF16), f1.astype(BF16), g2.astype(BF16)


def _fourier_mix(f):
    b, s, _ = f.shape
    n1 = FFT_N1
    n2 = s // n1
    fc, f1, g2 = _fft_tables(s, n1, n2)
    gd = FOURIER_GROUP_DIM
    p1 = n2 + PITCH_PAD
    p2 = 2 * n1 + PITCH_PAD
    return pl.pallas_call(
        functools.partial(_fft_kernel, n1=n1, n2=n2),
        grid=(b, N_FOURIER_GROUPS),
        in_specs=[
            pl.BlockSpec((1, s, gd), lambda i, g: (i, 0, g)),
            _resident((gd, 2 * gd)),
            _resident((2 * n1, 2 * n1)),
            _resident((n1, n2, 2 * n2)),
        ],
        out_specs=pl.BlockSpec((1, 1, n1, n2, gd), lambda i, g: (i, g, 0, 0, 0)),
        out_shape=jax.ShapeDtypeStruct((b, N_FOURIER_GROUPS, n1, n2, gd), F32),
        scratch_shapes=[
            pltpu.VMEM((2, n1 * p1, gd), F32),
            pltpu.VMEM((n2 * p2, gd), F32),
        ],
        compiler_params=pltpu.CompilerParams(
            dimension_semantics=("parallel", "parallel"), vmem_limit_bytes=VMEM_LIMIT_BYTES),
        name="fourier_mix",
    )(f, fc, f1, g2)


def _ret_kernel(lg_ref, q_ref, k_ref, v_ref, g_ref, o_ref,
                sf_ref, sb_ref, p_ref, fr_ref, br_ref, *, n_chunks):
    c = CHUNK
    dk2 = 2 * RET_QK_DIM
    dv = RET_V_DIM
    pair = pl.program_id(1)
    lgf = [lg_ref[0, 2 * pair + h] for h in range(2)]
    lgb = [lg_ref[1, 2 * pair + h] for h in range(2)]

    row = lax.broadcasted_iota(jnp.int32, (c, c), 0)
    col = lax.broadcasted_iota(jnp.int32, (c, c), 1)
    diff = (row - col).astype(F32)
    decay2 = jnp.concatenate(
        [jnp.where(diff >= 0.0,
                   jnp.exp(lgf[h] * jnp.maximum(diff, 0.0)),
                   jnp.exp(lgb[h] * jnp.maximum(-diff, 0.0))) for h in range(2)], axis=1)

    pos = lax.broadcasted_iota(jnp.int32, (c, dk2), 0).astype(F32)
    head_a = lax.broadcasted_iota(jnp.int32, (c, dk2), 1) < RET_QK_DIM

    def per_head(fa, fb):
        return jnp.where(head_a, fa, fb)

    q_dec_f = per_head(jnp.exp(lgf[0] * (pos + 1.0)), jnp.exp(lgf[1] * (pos + 1.0)))
    q_dec_b = per_head(jnp.exp(lgb[0] * (c - pos)), jnp.exp(lgb[1] * (c - pos)))
    k_dec_f = per_head(jnp.exp(lgf[0] * (c - 1.0 - pos)), jnp.exp(lgf[1] * (c - 1.0 - pos)))
    k_dec_b = per_head(jnp.exp(lgb[0] * pos), jnp.exp(lgb[1] * pos))

    srow = lax.broadcasted_iota(jnp.int32, (dk2, 2 * dv), 0) < RET_QK_DIM
    scol = lax.broadcasted_iota(jnp.int32, (dk2, 2 * dv), 1) < dv
    block_diag = srow == scol
    chunk_f = jnp.where(scol, jnp.exp(lgf[0] * c), jnp.exp(lgf[1] * c))
    chunk_b = jnp.where(scol, jnp.exp(lgb[0] * c), jnp.exp(lgb[1] * c))

    def kv_update(state, k_dec, chunk_dec, kc, vc):
        kd = (kc.astype(F32) * k_dec).astype(BF16)
        kv = lax.dot_general(kd, vc, (((0,), (0,)), ((), ())), preferred_element_type=F32)
        return chunk_dec * state + jnp.where(block_diag, kv, 0.0)

    fr_ref[...] = jnp.zeros_like(fr_ref)
    br_ref[...] = jnp.zeros_like(br_ref)

    def scan_step(t, carry):
        for run_ref, out_ref, i, k_dec, chunk_dec in (
                (fr_ref, sf_ref, t, k_dec_f, chunk_f),
                (br_ref, sb_ref, n_chunks - 1 - t, k_dec_b, chunk_b)):
            r0 = pl.multiple_of(i * c, c)
            state = run_ref[...]
            out_ref[i] = state.astype(BF16)
            run_ref[...] = kv_update(state, k_dec, chunk_dec,
                                     k_ref[0, pl.ds(r0, c), :], v_ref[0, pl.ds(r0, c), :])
        r0 = pl.multiple_of(t * c, c)
        kc = k_ref[0, pl.ds(r0, c), :]
        zero = jnp.zeros_like(kc)
        k_heads = jnp.concatenate([jnp.where(head_a, kc, zero), jnp.where(head_a, zero, kc)], axis=0)
        scores = lax.dot_general(q_ref[0, pl.ds(r0, c), :], k_heads, (((1,), (1,)), ((), ())),
                                 preferred_element_type=F32)
        p_ref[t] = (scores * decay2).astype(BF16)
        return carry

    lax.fori_loop(0, n_chunks, scan_step, 0, unroll=4)

    def out_step(i, carry):
        r0 = pl.multiple_of(i * c, c)
        qc = q_ref[0, pl.ds(r0, c), :]
        vc = v_ref[0, pl.ds(r0, c), :]
        qf = qc.astype(F32)
        q_inter = jnp.concatenate([(qf * q_dec_f).astype(BF16), (qf * q_dec_b).astype(BF16)], axis=1)
        states = jnp.concatenate([sf_ref[i], sb_ref[i]], axis=0)
        inter = jnp.dot(q_inter, states, preferred_element_type=F32)
        outs = []
        for h in range(2):
            o = jnp.dot(p_ref[i, :, h * c:(h + 1) * c], vc[:, h * dv:(h + 1) * dv],
                        preferred_element_type=F32)
            o = o + inter[:, h * dv:(h + 1) * dv]
            mu = jnp.mean(o, axis=-1, keepdims=True)
            d = o - mu
            var = jnp.mean(d * d, axis=-1, keepdims=True)
            outs.append(d * lax.rsqrt(var + GN_EPS))
        gate = g_ref[0, pl.ds(r0, c), :].astype(F32)
        o_ref[0, pl.ds(r0, c), :] = (jnp.concatenate(outs, axis=1) * (gate * jax.nn.sigmoid(gate))).astype(BF16)
        return carry

    lax.fori_loop(0, n_chunks, out_step, 0, unroll=4)


def _retention(q, k, v, g, log_gamma):
    b, s, _ = q.shape
    n_chunks = s // CHUNK
    n_pairs = N_RET_HEADS // 2
    dk2 = 2 * RET_QK_DIM
    dv2 = 2 * RET_V_DIM
    return pl.pallas_call(
        functools.partial(_ret_kernel, n_chunks=n_chunks),
        grid=(b, n_pairs),
        in_specs=[
            pl.BlockSpec(memory_space=pltpu.SMEM),
            pl.BlockSpec((1, s, dk2), lambda i, p: (i, 0, p)),
            pl.BlockSpec((1, s, dk2), lambda i, p: (i, 0, p)),
            pl.BlockSpec((1, s, dv2), lambda i, p: (i, 0, p)),
            pl.BlockSpec((1, s, dv2), lambda i, p: (i, 0, p)),
        ],
        out_specs=pl.BlockSpec((1, s, dv2), lambda i, p: (i, 0, p)),
        out_shape=jax.ShapeDtypeStruct((b, s, RET_V_WIDTH), BF16),
        scratch_shapes=[
            pltpu.VMEM((n_chunks, dk2, dv2), BF16),
            pltpu.VMEM((n_chunks, dk2, dv2), BF16),
            pltpu.VMEM((n_chunks, CHUNK, 2 * CHUNK), BF16),
            pltpu.VMEM((dk2, dv2), F32),
            pltpu.VMEM((dk2, dv2), F32),
        ],
        compiler_params=pltpu.CompilerParams(
            dimension_semantics=("parallel", "parallel"), vmem_limit_bytes=VMEM_LIMIT_BYTES),
        name="retention",
    )(log_gamma, q, k, v, g)


def _merge_kernel(x_ref, yf_ref, ret_ref, ga_ref, gb_ref, wf_ref, wr_ref, wo_ref, o_ref):
    n_groups, k2_tile = yf_ref.shape[1], yf_ref.shape[3]
    yf = jnp.concatenate(
        [jnp.concatenate([yf_ref[0, g, :, k2l, :] for g in range(n_groups)], axis=1)
         for k2l in range(k2_tile)], axis=0)
    a = jnp.dot(yf.astype(BF16), wf_ref[...], preferred_element_type=F32)
    bb = jnp.dot(ret_ref[0], wr_ref[...], preferred_element_type=F32)
    merged = (jax.nn.sigmoid(ga_ref[0].astype(F32)) * a
              + jax.nn.sigmoid(gb_ref[0].astype(F32)) * bb)
    o_ref[0] = x_ref[0] + jnp.dot(merged.astype(BF16), wo_ref[...], preferred_element_type=F32)


def _merge(x, y_four, y_ret, g_a, g_b, wf, wr, wo):
    b, s, d = x.shape
    _, n_groups, n1, n2, gd = y_four.shape
    k2_tile = SUBLANES
    tm = n1 * k2_tile

    def tile(w):
        return pl.BlockSpec((1, tm, w), lambda i, j: (i, j, 0))

    return pl.pallas_call(
        _merge_kernel,
        grid=(b, s // tm),
        in_specs=[tile(d),
                  pl.BlockSpec((1, n_groups, n1, k2_tile, gd), lambda i, j: (i, 0, 0, j, 0)),
                  tile(RET_V_WIDTH), tile(d), tile(d),
                  _resident(wf.shape), _resident(wr.shape), _resident(wo.shape)],
        out_specs=tile(d),
        out_shape=jax.ShapeDtypeStruct((b, s, d), F32),
        compiler_params=pltpu.CompilerParams(
            dimension_semantics=("parallel", "parallel"), vmem_limit_bytes=VMEM_LIMIT_BYTES),
        name="merge",
    )(x, y_four, y_ret, g_a, g_b, wf, wr, wo)


def _ffn_kernel(xp_ref, x_ref, xn_ref, g2_ref, wu_ref, cw_ref, cb_ref, wd_ref, gf_ref, o_ref, acc_ref):
    tm = x_ref.shape[1]
    halo = SUBLANES
    rows = tm + 2 * halo
    j = pl.program_id(1)
    last = pl.num_programs(1) - 1
    x = x_ref[0]
    xx = jnp.concatenate([xp_ref[0], x, xn_ref[0]], axis=0)
    u = _rms(xx, g2_ref[...]).astype(BF16)
    u_mid = u[halo:halo + tm]
    ridx = lax.broadcasted_iota(jnp.int32, (rows, 1), 0)
    keep = jnp.logical_and(jnp.logical_or(ridx >= halo, j > 0),
                           jnp.logical_or(ridx < halo + tm, j < last)).astype(F32)

    acc_ref[...] = jnp.zeros_like(acc_ref)
    for ci in range(D_FF // FF_CHUNK):
        lo = ci * FF_CHUNK
        hg = jnp.dot(u, wu_ref[:, lo:lo + FF_CHUNK], preferred_element_type=F32) * keep
        hv = jnp.dot(u_mid, wu_ref[:, D_FF + lo:D_FF + lo + FF_CHUNK], preferred_element_type=F32)
        cw = cw_ref[:, lo:lo + FF_CHUNK]
        prev = pltpu.roll(hg, 1, 0)[halo:halo + tm]
        nxt = pltpu.roll(hg, rows - 1, 0)[halo:halo + tm]
        hc = prev * cw[0:1] + hg[halo:halo + tm] * cw[1:2] + nxt * cw[2:3] + cb_ref[:, lo:lo + FF_CHUNK]
        act = 0.5 * hc * (1.0 + lax.erf(hc * (2.0 ** -0.5)))
        acc_ref[...] += jnp.dot((act * hv).astype(BF16), wd_ref[lo:lo + FF_CHUNK, :],
                                preferred_element_type=F32)
    o_ref[0] = _rms(x + acc_ref[...], gf_ref[...])


def _ffn(x1, norm2_g, w_up, conv_w, conv_b, w_down, final_g):
    b, s, d = x1.shape
    tm = TOKEN_TILE
    hb = tm // SUBLANES
    n_hblocks = s // SUBLANES
    return pl.pallas_call(
        _ffn_kernel,
        grid=(b, s // tm),
        in_specs=[
            pl.BlockSpec((1, SUBLANES, d), lambda i, j: (i, jnp.maximum(j * hb - 1, 0), 0)),
            pl.BlockSpec((1, tm, d), lambda i, j: (i, j, 0)),
            pl.BlockSpec((1, SUBLANES, d), lambda i, j: (i, jnp.minimum((j + 1) * hb, n_hblocks - 1), 0)),
            _resident((1, d)),
            _resident(w_up.shape),
            _resident(conv_w.shape),
            _resident(conv_b.shape),
            _resident(w_down.shape),
            _resident((1, d)),
        ],
        out_specs=pl.BlockSpec((1, tm, d), lambda i, j: (i, j, 0)),
        out_shape=jax.ShapeDtypeStruct((b, s, d), F32),
        scratch_shapes=[pltpu.VMEM((tm, d), F32)],
        compiler_params=pltpu.CompilerParams(
            dimension_semantics=("parallel", "parallel"), vmem_limit_bytes=VMEM_LIMIT_BYTES),
        name="convglu_ffn",
    )(x1, x1, x1, norm2_g, w_up, conv_w, conv_b, w_down, final_g)


def _rotary_tables(s):
    half = RET_QK_DIM // 2
    inv = ROPE_THETA ** (-jnp.arange(0, RET_QK_DIM, 2, dtype=F32) / RET_QK_DIM)
    ang = jnp.arange(s, dtype=F32)[:, None] * inv[None, :]
    cos, sin = jnp.cos(ang), jnp.sin(ang)
    reps = LANES // RET_QK_DIM
    cos_tab = jnp.concatenate([cos, cos] * reps, axis=1)
    sin_tab = jnp.concatenate([-sin, sin] * reps, axis=1)
    assert cos_tab.shape == (s, LANES) and half * 2 * reps == LANES
    return cos_tab, sin_tab


def _encoder_layer(x, p):
    b, s, d = x.shape
    cos_tab, sin_tab = _rotary_tables(s)
    f, q, k, v, g_ret, g_a, g_b = _inproj(x, p["norm1_g"], p["w_in"], cos_tab, sin_tab)
    y_four = _fourier_mix(f)
    y_ret = _retention(q, k, v, g_ret, p["log_gamma"])
    return _merge(x, y_four, y_ret, g_a, g_b, p["w_four_proj"], p["w_ret_proj"], p["w_out"])


def kernel(x_prompt, x_sample, norm1_g, w_in, w_four_proj, w_ret_proj, w_out, ret_decay_logit,
           norm2_g, w_up, conv_w, conv_b, w_down, final_norm_g):
    depth = w_in.shape[0]
    assert depth == 1, "the final RMSNorm is fused into the (single) layer's FFN kernel"
    layers = []
    for l in range(depth):
        layers.append(dict(
            norm1_g=norm1_g[l][None, :],
            w_in=w_in[l].astype(BF16),
            w_four_proj=w_four_proj[l].astype(BF16),
            w_ret_proj=w_ret_proj[l].astype(BF16),
            w_out=w_out[l].astype(BF16),
            log_gamma=jax.nn.log_sigmoid(ret_decay_logit[l].astype(F32)),
            norm2_g=norm2_g[l][None, :],
            w_up=w_up[l].astype(BF16),
            conv_w=conv_w[l],
            conv_b=conv_b[l][None, :],
            w_down=w_down[l].astype(BF16),
        ))
    final_g = final_norm_g[None, :]

    def trunk(x):
        p = layers[0]
        x1 = _encoder_layer(x, p)
        return _ffn(x1, p["norm2_g"], p["w_up"], p["conv_w"], p["conv_b"], p["w_down"], final_g)

    return (trunk(x_prompt), trunk(x_sample))
```

```python
import functools

import numpy as np
import jax
import jax.numpy as jnp
from jax import lax
from jax.experimental import pallas as pl
from jax.experimental.pallas import tpu as pltpu

D_MODEL = 1024
N_FOURIER_GROUPS = 4
FOURIER_GROUP_DIM = 128
FOURIER_WIDTH = N_FOURIER_GROUPS * FOURIER_GROUP_DIM
N_RET_HEADS = 8
RET_QK_DIM = 64
RET_V_DIM = 128
RET_QK_WIDTH = N_RET_HEADS * RET_QK_DIM
RET_V_WIDTH = N_RET_HEADS * RET_V_DIM
CHUNK = 128
ROPE_THETA = 10000.0
D_FF = 2816
NORM_EPS = 1e-6
GN_EPS = 1e-5
IN_SPLITS = (FOURIER_WIDTH, RET_QK_WIDTH, RET_QK_WIDTH, RET_V_WIDTH, RET_V_WIDTH, D_MODEL, D_MODEL)
IN_OFFSETS = tuple(int(o) for o in np.cumsum((0,) + IN_SPLITS))
IN_WIDTH = IN_OFFSETS[-1]

LANES = 128
SUBLANES = 8
VMEM_LIMIT_BYTES = 56 * 1024 * 1024

F32 = jnp.float32
BF16 = jnp.bfloat16

TOKEN_TILE = 512
FF_CHUNK = 256
FFT_N1 = 128
FFT_STAGE0_ROWS = 512
PITCH_PAD = SUBLANES


def _resident(shape):
    return pl.BlockSpec(shape, lambda *_: (0,) * len(shape), pipeline_mode=pl.Buffered(1))


def _rms(x, g):
    ms = jnp.mean(x * x, axis=-1, keepdims=True)
    return x * lax.rsqrt(ms + NORM_EPS) * g


def _inproj_kernel(x_ref, g_ref, w_ref, cos_ref, sin_ref,
                   f_ref, q_ref, k_ref, v_ref, gr_ref, ga_ref, gb_ref):
    u = _rms(x_ref[0], g_ref[...]).astype(BF16)

    def proj(i):
        return jnp.dot(u, w_ref[:, IN_OFFSETS[i]:IN_OFFSETS[i + 1]], preferred_element_type=F32)

    reps = RET_QK_WIDTH // LANES
    cos = jnp.concatenate([cos_ref[...]] * reps, axis=1)
    sin = jnp.concatenate([sin_ref[...]] * reps, axis=1)
    lane = lax.broadcasted_iota(jnp.int32, cos.shape, 1)
    first_half = (lane % RET_QK_DIM) < (RET_QK_DIM // 2)

    def rotary(t):
        partner = jnp.where(first_half,
                            pltpu.roll(t, RET_QK_WIDTH - RET_QK_DIM // 2, 1),
                            pltpu.roll(t, RET_QK_DIM // 2, 1))
        return t * cos + partner * sin

    def store_split(ref, val):
        parts, w = ref.shape[1], ref.shape[3]
        for p in range(parts):
            ref[0, p] = val[:, p * w:(p + 1) * w].astype(BF16)

    store_split(f_ref, proj(0))
    store_split(q_ref, rotary(proj(1)) * (RET_QK_DIM ** -0.5))
    store_split(k_ref, rotary(proj(2)))
    store_split(v_ref, proj(3))
    store_split(gr_ref, proj(4))
    ga_ref[0] = proj(5).astype(BF16)
    gb_ref[0] = proj(6).astype(BF16)


def _inproj(x, norm_g, w_in_bf16, cos_tab, sin_tab):
    b, s, d = x.shape
    tm = TOKEN_TILE
    parts = (N_FOURIER_GROUPS,) + (N_RET_HEADS // 2,) * 4
    out_shape = [jax.ShapeDtypeStruct((b, p, s, w // p), BF16) for p, w in zip(parts, IN_SPLITS[:5])]
    out_specs = [pl.BlockSpec((1, p, tm, w // p), lambda i, j: (i, 0, j, 0))
                 for p, w in zip(parts, IN_SPLITS[:5])]
    out_shape += [jax.ShapeDtypeStruct((b, s, w), BF16) for w in IN_SPLITS[5:]]
    out_specs += [pl.BlockSpec((1, tm, w), lambda i, j: (i, j, 0)) for w in IN_SPLITS[5:]]
    return pl.pallas_call(
        _inproj_kernel,
        grid=(b, s // tm),
        in_specs=[
            pl.BlockSpec((1, tm, d), lambda i, j: (i, j, 0)),
            _resident((1, d)),
            _resident((d, IN_WIDTH)),
            pl.BlockSpec((tm, LANES), lambda i, j: (j, 0)),
            pl.BlockSpec((tm, LANES), lambda i, j: (j, 0)),
        ],
        out_specs=out_specs,
        out_shape=out_shape,
        compiler_params=pltpu.CompilerParams(
            dimension_semantics=("parallel", "parallel"), vmem_limit_bytes=VMEM_LIMIT_BYTES),
        name="inproj",
    )(x, norm_g, w_in_bf16, cos_tab, sin_tab)


def _fft_kernel(x_ref, fc_ref, f1_ref, g2_ref, o_ref, z_ref, y_ref, *, n1, n2):
    p1 = n2 + PITCH_PAD
    p2 = 2 * n1 + PITCH_PAD
    gd = FOURIER_GROUP_DIM

    slabs = FFT_STAGE0_ROWS // n2

    def stage0(i, carry):
        r0 = pl.multiple_of(i * FFT_STAGE0_ROWS, FFT_STAGE0_ROWS)
        z = jnp.dot(x_ref[0, pl.ds(r0, FFT_STAGE0_ROWS), :], fc_ref[...],
                    preferred_element_type=F32)
        for s in range(slabs):
            row = pl.multiple_of((i * slabs + s) * p1, SUBLANES)
            z_ref[0, pl.ds(row, n2), :] = z[s * n2:(s + 1) * n2, :gd]
            z_ref[1, pl.ds(row, n2), :] = z[s * n2:(s + 1) * n2, gd:]
        return carry

    lax.fori_loop(0, (n1 * n2) // FFT_STAGE0_ROWS, stage0, 0, unroll=2)

    def stage1(jj, carry):
        cols = []
        for t in range(2):
            j = 2 * jj + t
            zr = z_ref[0, pl.ds(j, n1, stride=p1), :]
            zi = z_ref[1, pl.ds(j, n1, stride=p1), :]
            cols.append(jnp.concatenate([zr, zi], axis=0).astype(BF16))
        rhs = jnp.concatenate(cols, axis=1)
        y = jnp.dot(f1_ref[...], rhs, preferred_element_type=F32)
        for t in range(2):
            row = pl.multiple_of((2 * jj + t) * p2, SUBLANES)
            y_ref[pl.ds(row, 2 * n1), :] = y[:, t * gd:(t + 1) * gd]
        return carry

    lax.fori_loop(0, n2 // 2, stage1, 0, unroll=4)

    def stage2(k1, carry):
        yr = y_ref[pl.ds(k1, n2, stride=p2), :]
        yi = y_ref[pl.ds(n1 + k1, n2, stride=p2), :]
        rhs = jnp.concatenate([yr, yi], axis=0).astype(BF16)
        o_ref[0, 0, k1] = jnp.dot(g2_ref[k1], rhs, preferred_element_type=F32)
        return carry

    lax.fori_loop(0, n1, stage2, 0, unroll=8)


def _fft_tables(s, n1, n2):
    c = FOURIER_GROUP_DIM
    ic = jnp.arange(c, dtype=jnp.int32)
    ang_c = (2.0 * np.pi / c) * ((ic[:, None] * ic[None, :]) % c).astype(F32)
    scale_c = c ** -0.5
    fc = jnp.concatenate([jnp.cos(ang_c), -jnp.sin(ang_c)], axis=1) * scale_c
    i1 = jnp.arange(n1, dtype=jnp.int32)
    ang_1 = (2.0 * np.pi / n1) * ((i1[:, None] * i1[None, :]) % n1).astype(F32)
    c1, s1 = jnp.cos(ang_1), jnp.sin(ang_1)
    f1 = jnp.concatenate([jnp.concatenate([c1, s1], axis=1),
                          jnp.concatenate([-s1, c1], axis=1)], axis=0) * (n1 ** -0.5)
    k1 = jnp.arange(n1, dtype=jnp.int32)[:, None, None]
    k2 = jnp.arange(n2, dtype=jnp.int32)[None, :, None]
    m2 = jnp.arange(n2, dtype=jnp.int32)[None, None, :]
    ang_2 = (2.0 * np.pi / s) * ((m2 * (k1 + n1 * k2)) % s).astype(F32)
    g2 = jnp.concatenate([jnp.cos(ang_2), jnp.sin(ang_2)], axis=2) * (n2 ** -0.5)
    return fc.astype(BF16), f1.astype(BF16), g2.astype(BF16)


def _fourier_mix(f):
    b, _, s, _ = f.shape
    n1 = FFT_N1
    n2 = s // n1
    fc, f1, g2 = _fft_tables(s, n1, n2)
    gd = FOURIER_GROUP_DIM
    p1 = n2 + PITCH_PAD
    p2 = 2 * n1 + PITCH_PAD
    return pl.pallas_call(
        functools.partial(_fft_kernel, n1=n1, n2=n2),
        grid=(b, N_FOURIER_GROUPS),
        in_specs=[
            pl.BlockSpec((1, None, s, gd), lambda i, g: (i, g, 0, 0)),
            _resident((gd, 2 * gd)),
            _resident((2 * n1, 2 * n1)),
            _resident((n1, n2, 2 * n2)),
        ],
        out_specs=pl.BlockSpec((1, 1, n1, n2, gd), lambda i, g: (i, g, 0, 0, 0)),
        out_shape=jax.ShapeDtypeStruct((b, N_FOURIER_GROUPS, n1, n2, gd), F32),
        scratch_shapes=[
            pltpu.VMEM((2, n1 * p1, gd), F32),
            pltpu.VMEM((n2 * p2, gd), F32),
        ],
        compiler_params=pltpu.CompilerParams(
            dimension_semantics=("parallel", "parallel"), vmem_limit_bytes=VMEM_LIMIT_BYTES),
        name="fourier_mix",
    )(f, fc, f1, g2)


def _ret_kernel(lg_ref, q_ref, k_ref, v_ref, g_ref, o_ref,
                sf_ref, sb_ref, p_ref, fr_ref, br_ref, *, n_chunks):
    c = CHUNK
    dk2 = 2 * RET_QK_DIM
    dv = RET_V_DIM
    pair = pl.program_id(1)
    lgf = [lg_ref[0, 2 * pair + h] for h in range(2)]
    lgb = [lg_ref[1, 2 * pair + h] for h in range(2)]

    row = lax.broadcasted_iota(jnp.int32, (c, c), 0)
    col = lax.broadcasted_iota(jnp.int32, (c, c), 1)
    diff = (row - col).astype(F32)
    decay2 = jnp.concatenate(
        [jnp.where(diff >= 0.0,
                   jnp.exp(lgf[h] * jnp.maximum(diff, 0.0)),
                   jnp.exp(lgb[h] * jnp.maximum(-diff, 0.0))) for h in range(2)], axis=1)

    pos = lax.broadcasted_iota(jnp.int32, (c, dk2), 0).astype(F32)
    head_a = lax.broadcasted_iota(jnp.int32, (c, dk2), 1) < RET_QK_DIM

    def per_head(fa, fb):
        return jnp.where(head_a, fa, fb)

    q_dec_f = per_head(jnp.exp(lgf[0] * (pos + 1.0)), jnp.exp(lgf[1] * (pos + 1.0)))
    q_dec_b = per_head(jnp.exp(lgb[0] * (c - pos)), jnp.exp(lgb[1] * (c - pos)))
    k_dec_f = per_head(jnp.exp(lgf[0] * (c - 1.0 - pos)), jnp.exp(lgf[1] * (c - 1.0 - pos)))
    k_dec_b = per_head(jnp.exp(lgb[0] * pos), jnp.exp(lgb[1] * pos))

    srow = lax.broadcasted_iota(jnp.int32, (dk2, 2 * dv), 0) < RET_QK_DIM
    scol = lax.broadcasted_iota(jnp.int32, (dk2, 2 * dv), 1) < dv
    block_diag = srow == scol
    chunk_f = jnp.where(scol, jnp.exp(lgf[0] * c), jnp.exp(lgf[1] * c))
    chunk_b = jnp.where(scol, jnp.exp(lgb[0] * c), jnp.exp(lgb[1] * c))

    def kv_update(state, k_dec, chunk_dec, kc, vc):
        kd = (kc.astype(F32) * k_dec).astype(BF16)
        kv = lax.dot_general(kd, vc, (((0,), (0,)), ((), ())), preferred_element_type=F32)
        return chunk_dec * state + jnp.where(block_diag, kv, 0.0)

    fr_ref[...] = jnp.zeros_like(fr_ref)
    br_ref[...] = jnp.zeros_like(br_ref)

    def scan_step(t, carry):
        for run_ref, out_ref, i, k_dec, chunk_dec in (
                (fr_ref, sf_ref, t, k_dec_f, chunk_f),
                (br_ref, sb_ref, n_chunks - 1 - t, k_dec_b, chunk_b)):
            r0 = pl.multiple_of(i * c, c)
            state = run_ref[...]
            out_ref[i] = state.astype(BF16)
            run_ref[...] = kv_update(state, k_dec, chunk_dec,
                                     k_ref[0, pl.ds(r0, c), :], v_ref[0, pl.ds(r0, c), :])
        r0 = pl.multiple_of(t * c, c)
        kc = k_ref[0, pl.ds(r0, c), :]
        zero = jnp.zeros_like(kc)
        k_heads = jnp.concatenate([jnp.where(head_a, kc, zero), jnp.where(head_a, zero, kc)], axis=0)
        scores = lax.dot_general(q_ref[0, pl.ds(r0, c), :], k_heads, (((1,), (1,)), ((), ())),
                                 preferred_element_type=F32)
        p_ref[t] = (scores * decay2).astype(BF16)
        return carry

    lax.fori_loop(0, n_chunks, scan_step, 0, unroll=4)

    def out_step(i, carry):
        r0 = pl.multiple_of(i * c, c)
        qc = q_ref[0, pl.ds(r0, c), :]
        vc = v_ref[0, pl.ds(r0, c), :]
        qf = qc.astype(F32)
        q_inter = jnp.concatenate([(qf * q_dec_f).astype(BF16), (qf * q_dec_b).astype(BF16)], axis=1)
        states = jnp.concatenate([sf_ref[i], sb_ref[i]], axis=0)
        inter = jnp.dot(q_inter, states, preferred_element_type=F32)
        outs = []
        for h in range(2):
            o = jnp.dot(p_ref[i, :, h * c:(h + 1) * c], vc[:, h * dv:(h + 1) * dv],
                        preferred_element_type=F32)
            o = o + inter[:, h * dv:(h + 1) * dv]
            mu = jnp.mean(o, axis=-1, keepdims=True)
            d = o - mu
            var = jnp.mean(d * d, axis=-1, keepdims=True)
            outs.append(d * lax.rsqrt(var + GN_EPS))
        gate = g_ref[0, pl.ds(r0, c), :].astype(F32)
        o_ref[0, pl.ds(r0, c), :] = (jnp.concatenate(outs, axis=1) * (gate * jax.nn.sigmoid(gate))).astype(BF16)
        return carry

    lax.fori_loop(0, n_chunks, out_step, 0, unroll=4)


def _retention(q, k, v, g, log_gamma):
    b, _, s, _ = q.shape
    n_chunks = s // CHUNK
    n_pairs = N_RET_HEADS // 2
    dk2 = 2 * RET_QK_DIM
    dv2 = 2 * RET_V_DIM
    return pl.pallas_call(
        functools.partial(_ret_kernel, n_chunks=n_chunks),
        grid=(b, n_pairs),
        in_specs=[
            pl.BlockSpec(memory_space=pltpu.SMEM),
            pl.BlockSpec((1, None, s, dk2), lambda i, p: (i, p, 0, 0)),
            pl.BlockSpec((1, None, s, dk2), lambda i, p: (i, p, 0, 0)),
            pl.BlockSpec((1, None, s, dv2), lambda i, p: (i, p, 0, 0)),
            pl.BlockSpec((1, None, s, dv2), lambda i, p: (i, p, 0, 0)),
        ],
        out_specs=pl.BlockSpec((1, None, s, dv2), lambda i, p: (i, p, 0, 0)),
        out_shape=jax.ShapeDtypeStruct((b, n_pairs, s, dv2), BF16),
        scratch_shapes=[
            pltpu.VMEM((n_chunks, dk2, dv2), BF16),
            pltpu.VMEM((n_chunks, dk2, dv2), BF16),
            pltpu.VMEM((n_chunks, CHUNK, 2 * CHUNK), BF16),
            pltpu.VMEM((dk2, dv2), F32),
            pltpu.VMEM((dk2, dv2), F32),
        ],
        compiler_params=pltpu.CompilerParams(
            dimension_semantics=("parallel", "parallel"), vmem_limit_bytes=VMEM_LIMIT_BYTES),
        name="retention",
    )(log_gamma, q, k, v, g)


def _merge_kernel(x_ref, yf_ref, ret_ref, ga_ref, gb_ref, wf_ref, wr_ref, wo_ref, o_ref):
    n_groups, k2_tile = yf_ref.shape[1], yf_ref.shape[3]
    yf = jnp.concatenate(
        [jnp.concatenate([yf_ref[0, g, :, k2l, :] for g in range(n_groups)], axis=1)
         for k2l in range(k2_tile)], axis=0)
    a = jnp.dot(yf.astype(BF16), wf_ref[...], preferred_element_type=F32)
    ret = jnp.concatenate([ret_ref[0, p] for p in range(ret_ref.shape[1])], axis=1)
    bb = jnp.dot(ret, wr_ref[...], preferred_element_type=F32)
    merged = (jax.nn.sigmoid(ga_ref[0].astype(F32)) * a
              + jax.nn.sigmoid(gb_ref[0].astype(F32)) * bb)
    o_ref[0] = x_ref[0] + jnp.dot(merged.astype(BF16), wo_ref[...], preferred_element_type=F32)


def _merge(x, y_four, y_ret, g_a, g_b, wf, wr, wo):
    b, s, d = x.shape
    _, n_groups, n1, n2, gd = y_four.shape
    k2_tile = SUBLANES
    tm = n1 * k2_tile

    def tile(w):
        return pl.BlockSpec((1, tm, w), lambda i, j: (i, j, 0))

    return pl.pallas_call(
        _merge_kernel,
        grid=(b, s // tm),
        in_specs=[tile(d),
                  pl.BlockSpec((1, n_groups, n1, k2_tile, gd), lambda i, j: (i, 0, 0, j, 0)),
                  pl.BlockSpec((1,) + y_ret.shape[1:2] + (tm,) + y_ret.shape[3:], lambda i, j: (i, 0, j, 0)),
                  tile(d), tile(d),
                  _resident(wf.shape), _resident(wr.shape), _resident(wo.shape)],
        out_specs=tile(d),
        out_shape=jax.ShapeDtypeStruct((b, s, d), F32),
        compiler_params=pltpu.CompilerParams(
            dimension_semantics=("parallel", "parallel"), vmem_limit_bytes=VMEM_LIMIT_BYTES),
        name="merge",
    )(x, y_four, y_ret, g_a, g_b, wf, wr, wo)


def _ffn_kernel(xp_ref, x_ref, xn_ref, g2_ref, wu_ref, cw_ref, cb_ref, wd_ref, gf_ref, o_ref,
                hg_ref, act_ref):
    tm = x_ref.shape[1]
    halo = SUBLANES
    j = pl.program_id(1)
    last = pl.num_programs(1) - 1
    x = x_ref[0]
    xx = jnp.concatenate([x, xp_ref[0], xn_ref[0]], axis=0)
    u = _rms(xx, g2_ref[...]).astype(BF16)
    u_mid = u[0:tm]
    keep_top = (j > 0).astype(F32)
    keep_bot = (j < last).astype(F32)

    for ci in range(D_FF // FF_CHUNK):
        lo = ci * FF_CHUNK
        slot = ci % 2
        hg = jnp.dot(u, wu_ref[:, lo:lo + FF_CHUNK], preferred_element_type=F32)
        hv = jnp.dot(u_mid, wu_ref[:, D_FF + lo:D_FF + lo + FF_CHUNK], preferred_element_type=F32)
        mid = hg[0:tm]
        hg_ref[slot, 0:halo, :] = hg[tm:tm + halo] * keep_top
        hg_ref[slot, halo:halo + tm, :] = mid
        hg_ref[slot, halo + tm:, :] = hg[tm + halo:] * keep_bot
        cw = cw_ref[:, lo:lo + FF_CHUNK]
        prev = hg_ref[slot, halo - 1:halo - 1 + tm, :]
        nxt = hg_ref[slot, halo + 1:halo + 1 + tm, :]
        hc = prev * cw[0:1] + mid * cw[1:2] + nxt * cw[2:3] + cb_ref[:, lo:lo + FF_CHUNK]
        act = 0.5 * hc * (1.0 + lax.erf(hc * (2.0 ** -0.5)))
        act_ref[:, lo:lo + FF_CHUNK] = (act * hv).astype(BF16)
    y = jnp.dot(act_ref[...], wd_ref[...], preferred_element_type=F32)
    o_ref[0] = _rms(x + y, gf_ref[...])


def _ffn(x1, norm2_g, w_up, conv_w, conv_b, w_down, final_g):
    b, s, d = x1.shape
    tm = TOKEN_TILE
    hb = tm // SUBLANES
    n_hblocks = s // SUBLANES
    return pl.pallas_call(
        _ffn_kernel,
        grid=(b, s // tm),
        in_specs=[
            pl.BlockSpec((1, SUBLANES, d), lambda i, j: (i, jnp.maximum(j * hb - 1, 0), 0)),
            pl.BlockSpec((1, tm, d), lambda i, j: (i, j, 0)),
            pl.BlockSpec((1, SUBLANES, d), lambda i, j: (i, jnp.minimum((j + 1) * hb, n_hblocks - 1), 0)),
            _resident((1, d)),
            _resident(w_up.shape),
            _resident(conv_w.shape),
            _resident(conv_b.shape),
            _resident(w_down.shape),
            _resident((1, d)),
        ],
        out_specs=pl.BlockSpec((1, tm, d), lambda i, j: (i, j, 0)),
        out_shape=jax.ShapeDtypeStruct((b, s, d), F32),
        scratch_shapes=[pltpu.VMEM((2, tm + 2 * SUBLANES, FF_CHUNK), F32),
                        pltpu.VMEM((tm, D_FF), BF16)],
        compiler_params=pltpu.CompilerParams(
            dimension_semantics=("parallel", "parallel"), vmem_limit_bytes=VMEM_LIMIT_BYTES),
        name="convglu_ffn",
    )(x1, x1, x1, norm2_g, w_up, conv_w, conv_b, w_down, final_g)


def _rotary_tables(s):
    half = RET_QK_DIM // 2
    inv = ROPE_THETA ** (-jnp.arange(0, RET_QK_DIM, 2, dtype=F32) / RET_QK_DIM)
    ang = jnp.arange(s, dtype=F32)[:, None] * inv[None, :]
    cos, sin = jnp.cos(ang), jnp.sin(ang)
    reps = LANES // RET_QK_DIM
    cos_tab = jnp.concatenate([cos, cos] * reps, axis=1)
    sin_tab = jnp.concatenate([-sin, sin] * reps, axis=1)
    assert cos_tab.shape == (s, LANES) and half * 2 * reps == LANES
    return cos_tab, sin_tab


def _encoder_layer(x, p):
    b, s, d = x.shape
    cos_tab, sin_tab = _rotary_tables(s)
    f, q, k, v, g_ret, g_a, g_b = _inproj(x, p["norm1_g"], p["w_in"], cos_tab, sin_tab)
    y_four = _fourier_mix(f)
    y_ret = _retention(q, k, v, g_ret, p["log_gamma"])
    return _merge(x, y_four, y_ret, g_a, g_b, p["w_four_proj"], p["w_ret_proj"], p["w_out"])


def kernel(x_prompt, x_sample, norm1_g, w_in, w_four_proj, w_ret_proj, w_out, ret_decay_logit,
           norm2_g, w_up, conv_w, conv_b, w_down, final_norm_g):
    depth = w_in.shape[0]
    assert depth == 1, "the final RMSNorm is fused into the (single) layer's FFN kernel"
    layers = []
    for l in range(depth):
        layers.append(dict(
            norm1_g=norm1_g[l][None, :],
            w_in=w_in[l].astype(BF16),
            w_four_proj=w_four_proj[l].astype(BF16),
            w_ret_proj=w_ret_proj[l].astype(BF16),
            w_out=w_out[l].astype(BF16),
            log_gamma=jax.nn.log_sigmoid(ret_decay_logit[l].astype(F32)),
            norm2_g=norm2_g[l][None, :],
            w_up=w_up[l].astype(BF16),
            conv_w=conv_w[l],
            conv_b=conv_b[l][None, :],
            w_down=w_down[l].astype(BF16),
        ))
    final_g = final_norm_g[None, :]

    def trunk(x):
        p = layers[0]
        x1 = _encoder_layer(x, p)
        return _ffn(x1, p["norm2_g"], p["w_up"], p["conv_w"], p["conv_b"], p["w_down"], final_g)

    return (trunk(x_prompt), trunk(x_sample))
```

```python
import functools

import numpy as np
import jax
import jax.numpy as jnp
from jax import lax
from jax.experimental import pallas as pl
from jax.experimental.pallas import tpu as pltpu

D_MODEL = 1024
N_FOURIER_GROUPS = 4
FOURIER_GROUP_DIM = 128
FOURIER_WIDTH = N_FOURIER_GROUPS * FOURIER_GROUP_DIM
N_RET_HEADS = 8
RET_QK_DIM = 64
RET_V_DIM = 128
RET_QK_WIDTH = N_RET_HEADS * RET_QK_DIM
RET_V_WIDTH = N_RET_HEADS * RET_V_DIM
CHUNK = 128
ROPE_THETA = 10000.0
D_FF = 2816
NORM_EPS = 1e-6
GN_EPS = 1e-5
IN_SPLITS = (FOURIER_WIDTH, RET_QK_WIDTH, RET_QK_WIDTH, RET_V_WIDTH, RET_V_WIDTH, D_MODEL, D_MODEL)
IN_OFFSETS = tuple(int(o) for o in np.cumsum((0,) + IN_SPLITS))
IN_WIDTH = IN_OFFSETS[-1]

LANES = 128
SUBLANES = 8
VMEM_LIMIT_BYTES = 56 * 1024 * 1024

F32 = jnp.float32
BF16 = jnp.bfloat16

TOKEN_TILE = 1024
FF_CHUNK = 256
FFT_N1 = 128
FFT_STAGE0_ROWS = 512
MERGE_COLS = 256
PITCH_PAD = SUBLANES


def _resident(shape):
    return pl.BlockSpec(shape, lambda *_: (0,) * len(shape), pipeline_mode=pl.Buffered(1))


def _rms(x, g):
    ms = jnp.mean(x * x, axis=-1, keepdims=True)
    return x * lax.rsqrt(ms + NORM_EPS) * g


def _inproj_kernel(x_ref, g_ref, w_ref, cos_ref, sin_ref,
                   f_ref, q_ref, k_ref, v_ref, gr_ref, ga_ref, gb_ref):
    u = _rms(x_ref[0], g_ref[...]).astype(BF16)

    def proj(i):
        return jnp.dot(u, w_ref[:, IN_OFFSETS[i]:IN_OFFSETS[i + 1]], preferred_element_type=F32)

    reps = RET_QK_WIDTH // LANES
    cos = jnp.concatenate([cos_ref[...]] * reps, axis=1)
    sin = jnp.concatenate([sin_ref[...]] * reps, axis=1)
    lane = lax.broadcasted_iota(jnp.int32, cos.shape, 1)
    first_half = (lane % RET_QK_DIM) < (RET_QK_DIM // 2)

    def rotary(t):
        partner = jnp.where(first_half,
                            pltpu.roll(t, RET_QK_WIDTH - RET_QK_DIM // 2, 1),
                            pltpu.roll(t, RET_QK_DIM // 2, 1))
        return t * cos + partner * sin

    def store_split(ref, val):
        parts, w = ref.shape[1], ref.shape[3]
        for p in range(parts):
            ref[0, p] = val[:, p * w:(p + 1) * w].astype(BF16)

    store_split(f_ref, proj(0))
    store_split(q_ref, rotary(proj(1)) * (RET_QK_DIM ** -0.5))
    store_split(k_ref, rotary(proj(2)))
    store_split(v_ref, proj(3))
    store_split(gr_ref, proj(4))
    ga_ref[0] = proj(5).astype(BF16)
    gb_ref[0] = proj(6).astype(BF16)


def _inproj(x, norm_g, w_in_bf16, cos_tab, sin_tab):
    b, s, d = x.shape
    tm = TOKEN_TILE
    parts = (N_FOURIER_GROUPS,) + (N_RET_HEADS // 2,) * 4
    out_shape = [jax.ShapeDtypeStruct((b, p, s, w // p), BF16) for p, w in zip(parts, IN_SPLITS[:5])]
    out_specs = [pl.BlockSpec((1, p, tm, w // p), lambda i, j: (i, 0, j, 0))
                 for p, w in zip(parts, IN_SPLITS[:5])]
    out_shape += [jax.ShapeDtypeStruct((b, s, w), BF16) for w in IN_SPLITS[5:]]
    out_specs += [pl.BlockSpec((1, tm, w), lambda i, j: (i, j, 0)) for w in IN_SPLITS[5:]]
    return pl.pallas_call(
        _inproj_kernel,
        grid=(b, s // tm),
        in_specs=[
            pl.BlockSpec((1, tm, d), lambda i, j: (i, j, 0)),
            _resident((1, d)),
            _resident((d, IN_WIDTH)),
            pl.BlockSpec((tm, LANES), lambda i, j: (j, 0)),
            pl.BlockSpec((tm, LANES), lambda i, j: (j, 0)),
        ],
        out_specs=out_specs,
        out_shape=out_shape,
        compiler_params=pltpu.CompilerParams(
            dimension_semantics=("parallel", "parallel"), vmem_limit_bytes=VMEM_LIMIT_BYTES),
        name="inproj",
    )(x, norm_g, w_in_bf16, cos_tab, sin_tab)


def _fft_kernel(x_ref, fc_ref, f1_ref, g2_ref, o_ref, z_ref, y_ref, *, n1, n2):
    p1 = n2 + PITCH_PAD
    p2 = 2 * n1 + PITCH_PAD
    gd = FOURIER_GROUP_DIM

    slabs = FFT_STAGE0_ROWS // n2

    def stage0(i, carry):
        r0 = pl.multiple_of(i * FFT_STAGE0_ROWS, FFT_STAGE0_ROWS)
        z = jnp.dot(x_ref[0, pl.ds(r0, FFT_STAGE0_ROWS), :], fc_ref[...],
                    preferred_element_type=F32)
        for s in range(slabs):
            row = pl.multiple_of((i * slabs + s) * p1, SUBLANES)
            z_ref[0, pl.ds(row, n2), :] = z[s * n2:(s + 1) * n2, :gd]
            z_ref[1, pl.ds(row, n2), :] = z[s * n2:(s + 1) * n2, gd:]
        return carry

    lax.fori_loop(0, (n1 * n2) // FFT_STAGE0_ROWS, stage0, 0, unroll=8)

    def stage1(jj, carry):
        cols = []
        for t in range(2):
            j = 2 * jj + t
            zr = z_ref[0, pl.ds(j, n1, stride=p1), :]
            zi = z_ref[1, pl.ds(j, n1, stride=p1), :]
            cols.append(jnp.concatenate([zr, zi], axis=0).astype(BF16))
        rhs = jnp.concatenate(cols, axis=1)
        y = jnp.dot(f1_ref[...], rhs, preferred_element_type=F32)
        for t in range(2):
            row = pl.multiple_of((2 * jj + t) * p2, SUBLANES)
            y_ref[pl.ds(row, 2 * n1), :] = y[:, t * gd:(t + 1) * gd]
        return carry

    lax.fori_loop(0, n2 // 2, stage1, 0, unroll=16)

    def stage2(k1, carry):
        yr = y_ref[pl.ds(k1, n2, stride=p2), :]
        yi = y_ref[pl.ds(n1 + k1, n2, stride=p2), :]
        rhs = jnp.concatenate([yr, yi], axis=0).astype(BF16)
        out = jnp.dot(g2_ref[k1], rhs, preferred_element_type=F32)
        row = pl.multiple_of(k1 * SUBLANES, SUBLANES)
        o_ref[0, 0, :, pl.ds(row, SUBLANES), :] = out.reshape(n2 // SUBLANES, SUBLANES, gd)
        return carry

    lax.fori_loop(0, n1, stage2, 0, unroll=32)


def _fft_tables(s, n1, n2):
    c = FOURIER_GROUP_DIM
    ic = np.arange(c)
    ang_c = (2.0 * np.pi / c) * ((ic[:, None] * ic[None, :]) % c)
    fc = np.concatenate([np.cos(ang_c), -np.sin(ang_c)], axis=1) * c ** -0.5
    i1 = np.arange(n1)
    ang_1 = (2.0 * np.pi / n1) * ((i1[:, None] * i1[None, :]) % n1)
    c1, s1 = np.cos(ang_1), np.sin(ang_1)
    f1 = np.concatenate([np.concatenate([c1, s1], axis=1),
                         np.concatenate([-s1, c1], axis=1)], axis=0) * n1 ** -0.5
    k1 = np.arange(n1)[:, None, None]
    k2 = np.arange(n2)[None, :, None]
    m2 = np.arange(n2)[None, None, :]
    ang_2 = (2.0 * np.pi / s) * ((m2 * (k1 + n1 * k2)) % s)
    g2 = np.concatenate([np.cos(ang_2), np.sin(ang_2)], axis=2) * n2 ** -0.5
    return tuple(jnp.asarray(t.astype(np.float32)).astype(BF16) for t in (fc, f1, g2))


def _fourier_mix(f):
    b, _, s, _ = f.shape
    n1 = FFT_N1
    n2 = s // n1
    fc, f1, g2 = _fft_tables(s, n1, n2)
    gd = FOURIER_GROUP_DIM
    p1 = n2 + PITCH_PAD
    p2 = 2 * n1 + PITCH_PAD
    return pl.pallas_call(
        functools.partial(_fft_kernel, n1=n1, n2=n2),
        grid=(b, N_FOURIER_GROUPS),
        in_specs=[
            pl.BlockSpec((1, None, s, gd), lambda i, g: (i, g, 0, 0)),
            _resident((gd, 2 * gd)),
            _resident((2 * n1, 2 * n1)),
            _resident((n1, n2, 2 * n2)),
        ],
        out_specs=pl.BlockSpec((1, 1, n2 // SUBLANES, n1 * SUBLANES, gd), lambda i, g: (i, g, 0, 0, 0)),
        out_shape=jax.ShapeDtypeStruct((b, N_FOURIER_GROUPS, n2 // SUBLANES, n1 * SUBLANES, gd), F32),
        scratch_shapes=[
            pltpu.VMEM((2, n1 * p1, gd), F32),
            pltpu.VMEM((n2 * p2, gd), F32),
        ],
        compiler_params=pltpu.CompilerParams(
            dimension_semantics=("parallel", "parallel"), vmem_limit_bytes=VMEM_LIMIT_BYTES),
        name="fourier_mix",
    )(f, fc, f1, g2)


def _ret_kernel(lg_ref, q_ref, k_ref, v_ref, g_ref, o_ref,
                sf_ref, sb_ref, p_ref, fr_ref, br_ref, *, n_chunks):
    c = CHUNK
    dk2 = 2 * RET_QK_DIM
    dv = RET_V_DIM
    pair = pl.program_id(1)
    lgf = [lg_ref[0, 2 * pair + h] for h in range(2)]
    lgb = [lg_ref[1, 2 * pair + h] for h in range(2)]

    row = lax.broadcasted_iota(jnp.int32, (c, c), 0)
    col = lax.broadcasted_iota(jnp.int32, (c, c), 1)
    diff = (row - col).astype(F32)
    decay2 = jnp.concatenate(
        [jnp.where(diff >= 0.0,
                   jnp.exp(lgf[h] * jnp.maximum(diff, 0.0)),
                   jnp.exp(lgb[h] * jnp.maximum(-diff, 0.0))) for h in range(2)], axis=1)

    pos = lax.broadcasted_iota(jnp.int32, (c, dk2), 0).astype(F32)
    head_a = lax.broadcasted_iota(jnp.int32, (c, dk2), 1) < RET_QK_DIM

    def per_head(fa, fb):
        return jnp.where(head_a, fa, fb)

    q_dec_f = per_head(jnp.exp(lgf[0] * (pos + 1.0)), jnp.exp(lgf[1] * (pos + 1.0)))
    q_dec_b = per_head(jnp.exp(lgb[0] * (c - pos)), jnp.exp(lgb[1] * (c - pos)))
    k_dec_f = per_head(jnp.exp(lgf[0] * (c - 1.0 - pos)), jnp.exp(lgf[1] * (c - 1.0 - pos)))
    k_dec_b = per_head(jnp.exp(lgb[0] * pos), jnp.exp(lgb[1] * pos))

    srow = lax.broadcasted_iota(jnp.int32, (dk2, 2 * dv), 0) < RET_QK_DIM
    scol = lax.broadcasted_iota(jnp.int32, (dk2, 2 * dv), 1) < dv
    block_diag = srow == scol
    chunk_f = jnp.where(scol, jnp.exp(lgf[0] * c), jnp.exp(lgf[1] * c))
    chunk_b = jnp.where(scol, jnp.exp(lgb[0] * c), jnp.exp(lgb[1] * c))

    def kv_update(state, k_dec, chunk_dec, kc, vc):
        kd = (kc.astype(F32) * k_dec).astype(BF16)
        kv = lax.dot_general(kd, vc, (((0,), (0,)), ((), ())), preferred_element_type=F32)
        return chunk_dec * state + jnp.where(block_diag, kv, 0.0)

    fr_ref[...] = jnp.zeros_like(fr_ref)
    br_ref[...] = jnp.zeros_like(br_ref)

    def scan_step(t, carry):
        for run_ref, out_ref, i, k_dec, chunk_dec in (
                (fr_ref, sf_ref, t, k_dec_f, chunk_f),
                (br_ref, sb_ref, n_chunks - 1 - t, k_dec_b, chunk_b)):
            r0 = pl.multiple_of(i * c, c)
            state = run_ref[...]
            out_ref[i] = state.astype(BF16)
            run_ref[...] = kv_update(state, k_dec, chunk_dec,
                                     k_ref[0, pl.ds(r0, c), :], v_ref[0, pl.ds(r0, c), :])
        r0 = pl.multiple_of(t * c, c)
        kc = k_ref[0, pl.ds(r0, c), :]
        zero = jnp.zeros_like(kc)
        k_heads = jnp.concatenate([jnp.where(head_a, kc, zero), jnp.where(head_a, zero, kc)], axis=0)
        scores = lax.dot_general(q_ref[0, pl.ds(r0, c), :], k_heads, (((1,), (1,)), ((), ())),
                                 preferred_element_type=F32)
        p_ref[t] = (scores * decay2).astype(BF16)
        return carry

    lax.fori_loop(0, n_chunks, scan_step, 0, unroll=8)

    def out_step(i, carry):
        r0 = pl.multiple_of(i * c, c)
        qc = q_ref[0, pl.ds(r0, c), :]
        vc = v_ref[0, pl.ds(r0, c), :]
        qf = qc.astype(F32)
        q_inter = jnp.concatenate([(qf * q_dec_f).astype(BF16), (qf * q_dec_b).astype(BF16)], axis=1)
        states = jnp.concatenate([sf_ref[i], sb_ref[i]], axis=0)
        inter = jnp.dot(q_inter, states, preferred_element_type=F32)
        outs = []
        for h in range(2):
            o = jnp.dot(p_ref[i, :, h * c:(h + 1) * c], vc[:, h * dv:(h + 1) * dv],
                        preferred_element_type=F32)
            o = o + inter[:, h * dv:(h + 1) * dv]
            mu = jnp.mean(o, axis=-1, keepdims=True)
            d = o - mu
            var = jnp.mean(d * d, axis=-1, keepdims=True)
            outs.append(d * lax.rsqrt(var + GN_EPS))
        gate = g_ref[0, pl.ds(r0, c), :].astype(F32)
        o_ref[0, pl.ds(r0, c), :] = (jnp.concatenate(outs, axis=1) * (gate * jax.nn.sigmoid(gate))).astype(BF16)
        return carry

    lax.fori_loop(0, n_chunks, out_step, 0, unroll=8)


def _retention(q, k, v, g, log_gamma):
    b, _, s, _ = q.shape
    n_chunks = s // CHUNK
    n_pairs = N_RET_HEADS // 2
    dk2 = 2 * RET_QK_DIM
    dv2 = 2 * RET_V_DIM
    return pl.pallas_call(
        functools.partial(_ret_kernel, n_chunks=n_chunks),
        grid=(b, n_pairs),
        in_specs=[
            pl.BlockSpec(memory_space=pltpu.SMEM),
            pl.BlockSpec((1, None, s, dk2), lambda i, p: (i, p, 0, 0)),
            pl.BlockSpec((1, None, s, dk2), lambda i, p: (i, p, 0, 0)),
            pl.BlockSpec((1, None, s, dv2), lambda i, p: (i, p, 0, 0)),
            pl.BlockSpec((1, None, s, dv2), lambda i, p: (i, p, 0, 0)),
        ],
        out_specs=pl.BlockSpec((1, None, s, dv2), lambda i, p: (i, p, 0, 0)),
        out_shape=jax.ShapeDtypeStruct((b, n_pairs, s, dv2), BF16),
        scratch_shapes=[
            pltpu.VMEM((n_chunks, dk2, dv2), BF16),
            pltpu.VMEM((n_chunks, dk2, dv2), BF16),
            pltpu.VMEM((n_chunks, CHUNK, 2 * CHUNK), BF16),
            pltpu.VMEM((dk2, dv2), F32),
            pltpu.VMEM((dk2, dv2), F32),
        ],
        compiler_params=pltpu.CompilerParams(
            dimension_semantics=("parallel", "parallel"), vmem_limit_bytes=VMEM_LIMIT_BYTES),
        name="retention",
    )(log_gamma, q, k, v, g)


def _merge_kernel(x_ref, yf_ref, ret_ref, ga_ref, gb_ref, wf_ref, wr_ref, wo_ref, o_ref, m_ref):
    n_groups = yf_ref.shape[1]
    k2_tile = SUBLANES
    n1 = yf_ref.shape[2] // k2_tile
    yf = jnp.concatenate(
        [jnp.concatenate([yf_ref[0, g, pl.ds(k2l, n1, stride=k2_tile), :] for g in range(n_groups)], axis=1)
         for k2l in range(k2_tile)], axis=0)
    yf = yf.astype(BF16)
    ret = jnp.concatenate([ret_ref[0, p] for p in range(ret_ref.shape[1])], axis=1)
    for lo in range(0, m_ref.shape[1], MERGE_COLS):
        cols = slice(lo, lo + MERGE_COLS)
        a = jnp.dot(yf, wf_ref[:, cols], preferred_element_type=F32)
        bb = jnp.dot(ret, wr_ref[:, cols], preferred_element_type=F32)
        m_ref[:, cols] = (jax.nn.sigmoid(ga_ref[0, :, cols].astype(F32)) * a
                          + jax.nn.sigmoid(gb_ref[0, :, cols].astype(F32)) * bb).astype(BF16)
    o_ref[0] = x_ref[0] + jnp.dot(m_ref[...], wo_ref[...], preferred_element_type=F32)


def _merge(x, y_four, y_ret, g_a, g_b, wf, wr, wo):
    b, s, d = x.shape
    _, n_groups, _, tm, gd = y_four.shape

    def tile(w):
        return pl.BlockSpec((1, tm, w), lambda i, j: (i, j, 0))

    return pl.pallas_call(
        _merge_kernel,
        grid=(b, s // tm),
        in_specs=[tile(d),
                  pl.BlockSpec((1, n_groups, None, tm, gd), lambda i, j: (i, 0, j, 0, 0)),
                  pl.BlockSpec((1,) + y_ret.shape[1:2] + (tm,) + y_ret.shape[3:], lambda i, j: (i, 0, j, 0)),
                  tile(d), tile(d),
                  _resident(wf.shape), _resident(wr.shape), _resident(wo.shape)],
        out_specs=tile(d),
        out_shape=jax.ShapeDtypeStruct((b, s, d), F32),
        scratch_shapes=[pltpu.VMEM((tm, d), BF16)],
        compiler_params=pltpu.CompilerParams(
            dimension_semantics=("parallel", "parallel"), vmem_limit_bytes=VMEM_LIMIT_BYTES),
        name="merge",
    )(x, y_four, y_ret, g_a, g_b, wf, wr, wo)


def _ffn_kernel(xp_ref, x_ref, xn_ref, g2_ref, wu_ref, cw_ref, cb_ref, wd_ref, gf_ref, o_ref,
                hg0_ref, hg1_ref, act_ref):
    tm = x_ref.shape[1]
    halo = SUBLANES
    j = pl.program_id(1)
    last = pl.num_programs(1) - 1
    x = x_ref[0]
    xx = jnp.concatenate([x, xp_ref[0], xn_ref[0]], axis=0)
    u = _rms(xx, g2_ref[...]).astype(BF16)
    u_mid = u[0:tm]
    keep_top = (j > 0).astype(F32)
    keep_bot = (j < last).astype(F32)

    for ci in range(D_FF // FF_CHUNK):
        lo = ci * FF_CHUNK
        hg_ref = (hg0_ref, hg1_ref)[ci % 2]
        hg = jnp.dot(u, wu_ref[:, lo:lo + FF_CHUNK], preferred_element_type=F32)
        hv = jnp.dot(u_mid, wu_ref[:, D_FF + lo:D_FF + lo + FF_CHUNK], preferred_element_type=F32)
        mid = hg[0:tm]
        hg_ref[0:halo, :] = hg[tm:tm + halo] * keep_top
        hg_ref[halo:halo + tm, :] = mid
        hg_ref[halo + tm:, :] = hg[tm + halo:] * keep_bot
        cw = cw_ref[:, lo:lo + FF_CHUNK]
        prev = hg_ref[halo - 1:halo - 1 + tm, :]
        nxt = hg_ref[halo + 1:halo + 1 + tm, :]
        hc = prev * cw[0:1] + mid * cw[1:2] + nxt * cw[2:3] + cb_ref[:, lo:lo + FF_CHUNK]
        act = 0.5 * hc * (1.0 + lax.erf(hc * (2.0 ** -0.5)))
        act_ref[:, lo:lo + FF_CHUNK] = (act * hv).astype(BF16)
    y = jnp.dot(act_ref[...], wd_ref[...], preferred_element_type=F32)
    o_ref[0] = _rms(x + y, gf_ref[...])


def _ffn(x1, norm2_g, w_up, conv_w, conv_b, w_down, final_g):
    b, s, d = x1.shape
    tm = TOKEN_TILE
    hb = tm // SUBLANES
    n_hblocks = s // SUBLANES
    return pl.pallas_call(
        _ffn_kernel,
        grid=(b, s // tm),
        in_specs=[
            pl.BlockSpec((1, SUBLANES, d), lambda i, j: (i, jnp.maximum(j * hb - 1, 0), 0)),
            pl.BlockSpec((1, tm, d), lambda i, j: (i, j, 0)),
            pl.BlockSpec((1, SUBLANES, d), lambda i, j: (i, jnp.minimum((j + 1) * hb, n_hblocks - 1), 0)),
            _resident((1, d)),
            _resident(w_up.shape),
            _resident(conv_w.shape),
            _resident(conv_b.shape),
            _resident(w_down.shape),
            _resident((1, d)),
        ],
        out_specs=pl.BlockSpec((1, tm, d), lambda i, j: (i, j, 0)),
        out_shape=jax.ShapeDtypeStruct((b, s, d), F32),
        scratch_shapes=[pltpu.VMEM((tm + 2 * SUBLANES, FF_CHUNK), F32),
                        pltpu.VMEM((tm + 2 * SUBLANES, FF_CHUNK), F32),
                        pltpu.VMEM((tm, D_FF), BF16)],
        compiler_params=pltpu.CompilerParams(
            dimension_semantics=("parallel", "parallel"), vmem_limit_bytes=VMEM_LIMIT_BYTES),
        name="convglu_ffn",
    )(x1, x1, x1, norm2_g, w_up, conv_w, conv_b, w_down, final_g)


def _rotary_tables(s):
    inv = ROPE_THETA ** (-np.arange(0, RET_QK_DIM, 2, dtype=np.float64) / RET_QK_DIM)
    ang = np.arange(s, dtype=np.float64)[:, None] * inv[None, :]
    cos, sin = np.cos(ang), np.sin(ang)
    reps = LANES // RET_QK_DIM
    cos_tab = np.concatenate([cos, cos] * reps, axis=1).astype(np.float32)
    sin_tab = np.concatenate([-sin, sin] * reps, axis=1).astype(np.float32)
    assert cos_tab.shape == (s, LANES)
    return jnp.asarray(cos_tab), jnp.asarray(sin_tab)


def _encoder_layer(x, p):
    b, s, d = x.shape
    cos_tab, sin_tab = _rotary_tables(s)
    f, q, k, v, g_ret, g_a, g_b = _inproj(x, p["norm1_g"], p["w_in"], cos_tab, sin_tab)
    y_four = _fourier_mix(f)
    y_ret = _retention(q, k, v, g_ret, p["log_gamma"])
    return _merge(x, y_four, y_ret, g_a, g_b, p["w_four_proj"], p["w_ret_proj"], p["w_out"])


def kernel(x_prompt, x_sample, norm1_g, w_in, w_four_proj, w_ret_proj, w_out, ret_decay_logit,
           norm2_g, w_up, conv_w, conv_b, w_down, final_norm_g):
    depth = w_in.shape[0]
    assert depth == 1, "the final RMSNorm is fused into the (single) layer's FFN kernel"
    layers = []
    for l in range(depth):
        layers.append(dict(
            norm1_g=norm1_g[l][None, :],
            w_in=w_in[l].astype(BF16),
            w_four_proj=w_four_proj[l].astype(BF16),
            w_ret_proj=w_ret_proj[l].astype(BF16),
            w_out=w_out[l].astype(BF16),
            log_gamma=jax.nn.log_sigmoid(ret_decay_logit[l].astype(F32)),
            norm2_g=norm2_g[l][None, :],
            w_up=w_up[l].astype(BF16),
            conv_w=conv_w[l],
            conv_b=conv_b[l][None, :],
            w_down=w_down[l].astype(BF16),
        ))
    final_g = final_norm_g[None, :]

    def trunk(x):
        p = layers[0]
        x1 = _encoder_layer(x, p)
        return _ffn(x1, p["norm2_g"], p["w_up"], p["conv_w"], p["conv_b"], p["w_down"], final_g)

    return (trunk(x_prompt), trunk(x_sample))
```

```python
import functools

import numpy as np
import jax
import jax.numpy as jnp
from jax import lax
from jax.experimental import pallas as pl
from jax.experimental.pallas import tpu as pltpu

D_MODEL = 1024
N_FOURIER_GROUPS = 4
FOURIER_GROUP_DIM = 128
FOURIER_WIDTH = N_FOURIER_GROUPS * FOURIER_GROUP_DIM
N_RET_HEADS = 8
RET_QK_DIM = 64
RET_V_DIM = 128
RET_QK_WIDTH = N_RET_HEADS * RET_QK_DIM
RET_V_WIDTH = N_RET_HEADS * RET_V_DIM
CHUNK = 128
ROPE_THETA = 10000.0
D_FF = 2816
NORM_EPS = 1e-6
GN_EPS = 1e-5
IN_SPLITS = (FOURIER_WIDTH, RET_QK_WIDTH, RET_QK_WIDTH, RET_V_WIDTH, RET_V_WIDTH, D_MODEL, D_MODEL)
IN_OFFSETS = tuple(int(o) for o in np.cumsum((0,) + IN_SPLITS))
IN_WIDTH = IN_OFFSETS[-1]

LANES = 128
SUBLANES = 8
VMEM_LIMIT_BYTES = 56 * 1024 * 1024

F32 = jnp.float32
BF16 = jnp.bfloat16

TOKEN_TILE = 1024
FF_CHUNK = 256
FFT_N1 = 128
FFT_STAGE0_ROWS = 512
MERGE_COLS = 256
PITCH_PAD = SUBLANES


def _resident(shape):
    return pl.BlockSpec(shape, lambda *_: (0,) * len(shape), pipeline_mode=pl.Buffered(1))


def _rms(x, g):
    ms = jnp.mean(x * x, axis=-1, keepdims=True)
    return x * lax.rsqrt(ms + NORM_EPS) * g


def _inproj_kernel(x_ref, g_ref, w_ref, cos_ref, sin_ref,
                   f_ref, q_ref, k_ref, v_ref, gr_ref, ga_ref, gb_ref):
    u = _rms(x_ref[0], g_ref[...]).astype(BF16)

    def proj(i):
        return jnp.dot(u, w_ref[:, IN_OFFSETS[i]:IN_OFFSETS[i + 1]], preferred_element_type=F32)

    reps = RET_QK_WIDTH // LANES
    cos = jnp.concatenate([cos_ref[...]] * reps, axis=1)
    sin = jnp.concatenate([sin_ref[...]] * reps, axis=1)
    lane = lax.broadcasted_iota(jnp.int32, cos.shape, 1)
    first_half = (lane % RET_QK_DIM) < (RET_QK_DIM // 2)

    def rotary(t):
        partner = jnp.where(first_half,
                            pltpu.roll(t, RET_QK_WIDTH - RET_QK_DIM // 2, 1),
                            pltpu.roll(t, RET_QK_DIM // 2, 1))
        return t * cos + partner * sin

    def store_split(ref, val):
        parts, w = ref.shape[1], ref.shape[3]
        for p in range(parts):
            ref[0, p] = val[:, p * w:(p + 1) * w].astype(BF16)

    store_split(f_ref, proj(0))
    store_split(q_ref, rotary(proj(1)) * (RET_QK_DIM ** -0.5))
    store_split(k_ref, rotary(proj(2)))
    store_split(v_ref, proj(3))
    store_split(gr_ref, proj(4))
    ga_ref[0] = proj(5).astype(BF16)
    gb_ref[0] = proj(6).astype(BF16)


def _inproj(x, norm_g, w_in_bf16, cos_tab, sin_tab):
    b, s, d = x.shape
    tm = TOKEN_TILE
    parts = (N_FOURIER_GROUPS,) + (N_RET_HEADS // 2,) * 4
    out_shape = [jax.ShapeDtypeStruct((b, p, s, w // p), BF16) for p, w in zip(parts, IN_SPLITS[:5])]
    out_specs = [pl.BlockSpec((1, p, tm, w // p), lambda i, j: (i, 0, j, 0))
                 for p, w in zip(parts, IN_SPLITS[:5])]
    out_shape += [jax.ShapeDtypeStruct((b, s, w), BF16) for w in IN_SPLITS[5:]]
    out_specs += [pl.BlockSpec((1, tm, w), lambda i, j: (i, j, 0)) for w in IN_SPLITS[5:]]
    return pl.pallas_call(
        _inproj_kernel,
        grid=(b, s // tm),
        in_specs=[
            pl.BlockSpec((1, tm, d), lambda i, j: (i, j, 0)),
            _resident((1, d)),
            _resident((d, IN_WIDTH)),
            pl.BlockSpec((tm, LANES), lambda i, j: (j, 0)),
            pl.BlockSpec((tm, LANES), lambda i, j: (j, 0)),
        ],
        out_specs=out_specs,
        out_shape=out_shape,
        compiler_params=pltpu.CompilerParams(
            dimension_semantics=("parallel", "parallel"), vmem_limit_bytes=VMEM_LIMIT_BYTES),
        name="inproj",
    )(x, norm_g, w_in_bf16, cos_tab, sin_tab)


def _fft_kernel(x_ref, fc_ref, f1_ref, g2_ref, o_ref, z_ref, y_ref, *, n1, n2):
    p1 = n2 + PITCH_PAD
    p2 = 2 * n1 + PITCH_PAD
    gd = FOURIER_GROUP_DIM

    slabs = FFT_STAGE0_ROWS // n2

    def stage0(i, carry):
        r0 = pl.multiple_of(i * FFT_STAGE0_ROWS, FFT_STAGE0_ROWS)
        z = jnp.dot(x_ref[0, pl.ds(r0, FFT_STAGE0_ROWS), :], fc_ref[...],
                    preferred_element_type=F32)
        for s in range(slabs):
            row = pl.multiple_of((i * slabs + s) * p1, SUBLANES)
            z_ref[0, pl.ds(row, n2), :] = z[s * n2:(s + 1) * n2, :gd]
            z_ref[1, pl.ds(row, n2), :] = z[s * n2:(s + 1) * n2, gd:]
        return carry

    lax.fori_loop(0, (n1 * n2) // FFT_STAGE0_ROWS, stage0, 0, unroll=16)

    def stage1(jj, carry):
        cols = []
        for t in range(2):
            j = 2 * jj + t
            zr = z_ref[0, pl.ds(j, n1, stride=p1), :]
            zi = z_ref[1, pl.ds(j, n1, stride=p1), :]
            cols.append(jnp.concatenate([zr, zi], axis=0).astype(BF16))
        rhs = jnp.concatenate(cols, axis=1)
        y = jnp.dot(f1_ref[...], rhs, preferred_element_type=F32)
        for t in range(2):
            row = pl.multiple_of((2 * jj + t) * p2, SUBLANES)
            y_ref[pl.ds(row, 2 * n1), :] = y[:, t * gd:(t + 1) * gd]
        return carry

    lax.fori_loop(0, n2 // 2, stage1, 0, unroll=32)

    def stage2(k1, carry):
        yr = y_ref[pl.ds(k1, n2, stride=p2), :]
        yi = y_ref[pl.ds(n1 + k1, n2, stride=p2), :]
        rhs = jnp.concatenate([yr, yi], axis=0).astype(BF16)
        out = jnp.dot(g2_ref[k1], rhs, preferred_element_type=F32)
        row = pl.multiple_of(k1 * SUBLANES, SUBLANES)
        o_ref[0, 0, :, pl.ds(row, SUBLANES), :] = out.reshape(n2 // SUBLANES, SUBLANES, gd)
        return carry

    lax.fori_loop(0, n1, stage2, 0, unroll=64)


def _fft_tables(s, n1, n2):
    c = FOURIER_GROUP_DIM
    ic = np.arange(c)
    ang_c = (2.0 * np.pi / c) * ((ic[:, None] * ic[None, :]) % c)
    fc = np.concatenate([np.cos(ang_c), -np.sin(ang_c)], axis=1) * c ** -0.5
    i1 = np.arange(n1)
    ang_1 = (2.0 * np.pi / n1) * ((i1[:, None] * i1[None, :]) % n1)
    c1, s1 = np.cos(ang_1), np.sin(ang_1)
    f1 = np.concatenate([np.concatenate([c1, s1], axis=1),
                         np.concatenate([-s1, c1], axis=1)], axis=0) * n1 ** -0.5
    k1 = np.arange(n1)[:, None, None]
    k2 = np.arange(n2)[None, :, None]
    m2 = np.arange(n2)[None, None, :]
    ang_2 = (2.0 * np.pi / s) * ((m2 * (k1 + n1 * k2)) % s)
    g2 = np.concatenate([np.cos(ang_2), np.sin(ang_2)], axis=2) * n2 ** -0.5
    return tuple(jnp.asarray(t.astype(np.float32)).astype(BF16) for t in (fc, f1, g2))


def _fourier_mix(f):
    b, _, s, _ = f.shape
    n1 = FFT_N1
    n2 = s // n1
    fc, f1, g2 = _fft_tables(s, n1, n2)
    gd = FOURIER_GROUP_DIM
    p1 = n2 + PITCH_PAD
    p2 = 2 * n1 + PITCH_PAD
    return pl.pallas_call(
        functools.partial(_fft_kernel, n1=n1, n2=n2),
        grid=(b, N_FOURIER_GROUPS),
        in_specs=[
            pl.BlockSpec((1, None, s, gd), lambda i, g: (i, g, 0, 0)),
            _resident((gd, 2 * gd)),
            _resident((2 * n1, 2 * n1)),
            _resident((n1, n2, 2 * n2)),
        ],
        out_specs=pl.BlockSpec((1, 1, n2 // SUBLANES, n1 * SUBLANES, gd), lambda i, g: (i, g, 0, 0, 0)),
        out_shape=jax.ShapeDtypeStruct((b, N_FOURIER_GROUPS, n2 // SUBLANES, n1 * SUBLANES, gd), F32),
        scratch_shapes=[
            pltpu.VMEM((2, n1 * p1, gd), F32),
            pltpu.VMEM((n2 * p2, gd), F32),
        ],
        compiler_params=pltpu.CompilerParams(
            dimension_semantics=("parallel", "parallel"), vmem_limit_bytes=VMEM_LIMIT_BYTES),
        name="fourier_mix",
    )(f, fc, f1, g2)


def _ret_kernel(lg_ref, q_ref, k_ref, v_ref, g_ref, o_ref,
                sf_ref, sb_ref, p_ref, fr_ref, br_ref, *, n_chunks):
    c = CHUNK
    dk2 = 2 * RET_QK_DIM
    dv = RET_V_DIM
    pair = pl.program_id(1)
    lgf = [lg_ref[0, 2 * pair + h] for h in range(2)]
    lgb = [lg_ref[1, 2 * pair + h] for h in range(2)]

    row = lax.broadcasted_iota(jnp.int32, (c, c), 0)
    col = lax.broadcasted_iota(jnp.int32, (c, c), 1)
    diff = (row - col).astype(F32)
    decay2 = jnp.concatenate(
        [jnp.where(diff >= 0.0,
                   jnp.exp(lgf[h] * jnp.maximum(diff, 0.0)),
                   jnp.exp(lgb[h] * jnp.maximum(-diff, 0.0))) for h in range(2)], axis=1)

    pos = lax.broadcasted_iota(jnp.int32, (c, dk2), 0).astype(F32)
    head_a = lax.broadcasted_iota(jnp.int32, (c, dk2), 1) < RET_QK_DIM

    def per_head(fa, fb):
        return jnp.where(head_a, fa, fb).astype(BF16)

    q_dec_f = per_head(jnp.exp(lgf[0] * (pos + 1.0)), jnp.exp(lgf[1] * (pos + 1.0)))
    q_dec_b = per_head(jnp.exp(lgb[0] * (c - pos)), jnp.exp(lgb[1] * (c - pos)))
    k_dec_f = per_head(jnp.exp(lgf[0] * (c - 1.0 - pos)), jnp.exp(lgf[1] * (c - 1.0 - pos)))
    k_dec_b = per_head(jnp.exp(lgb[0] * pos), jnp.exp(lgb[1] * pos))

    srow = lax.broadcasted_iota(jnp.int32, (dk2, 2 * dv), 0) < RET_QK_DIM
    scol = lax.broadcasted_iota(jnp.int32, (dk2, 2 * dv), 1) < dv
    block_diag = srow == scol
    chunk_f = jnp.where(scol, jnp.exp(lgf[0] * c), jnp.exp(lgf[1] * c))
    chunk_b = jnp.where(scol, jnp.exp(lgb[0] * c), jnp.exp(lgb[1] * c))

    def kv_update(state, k_dec, chunk_dec, kc, vc):
        kd = kc * k_dec
        kv = lax.dot_general(kd, vc, (((0,), (0,)), ((), ())), preferred_element_type=F32)
        return chunk_dec * state + jnp.where(block_diag, kv, 0.0)

    fr_ref[...] = jnp.zeros_like(fr_ref)
    br_ref[...] = jnp.zeros_like(br_ref)

    def scan_step(t, carry):
        for run_ref, out_ref, i, k_dec, chunk_dec in (
                (fr_ref, sf_ref, t, k_dec_f, chunk_f),
                (br_ref, sb_ref, n_chunks - 1 - t, k_dec_b, chunk_b)):
            r0 = pl.multiple_of(i * c, c)
            state = run_ref[...]
            out_ref[i] = state.astype(BF16)
            run_ref[...] = kv_update(state, k_dec, chunk_dec,
                                     k_ref[0, pl.ds(r0, c), :], v_ref[0, pl.ds(r0, c), :])
        r0 = pl.multiple_of(t * c, c)
        kc = k_ref[0, pl.ds(r0, c), :]
        zero = jnp.zeros_like(kc)
        k_heads = jnp.concatenate([jnp.where(head_a, kc, zero), jnp.where(head_a, zero, kc)], axis=0)
        scores = lax.dot_general(q_ref[0, pl.ds(r0, c), :], k_heads, (((1,), (1,)), ((), ())),
                                 preferred_element_type=F32)
        p_ref[t] = (scores * decay2).astype(BF16)
        return carry

    lax.fori_loop(0, n_chunks, scan_step, 0, unroll=16)

    def out_step(i, carry):
        r0 = pl.multiple_of(i * c, c)
        qc = q_ref[0, pl.ds(r0, c), :]
        vc = v_ref[0, pl.ds(r0, c), :]
        q_inter = jnp.concatenate([qc * q_dec_f, qc * q_dec_b], axis=1)
        states = jnp.concatenate([sf_ref[i], sb_ref[i]], axis=0)
        inter = jnp.dot(q_inter, states, preferred_element_type=F32)
        outs = []
        for h in range(2):
            o = jnp.dot(p_ref[i, :, h * c:(h + 1) * c], vc[:, h * dv:(h + 1) * dv],
                        preferred_element_type=F32)
            o = o + inter[:, h * dv:(h + 1) * dv]
            mu = jnp.mean(o, axis=-1, keepdims=True)
            d = o - mu
            var = jnp.mean(d * d, axis=-1, keepdims=True)
            outs.append(d * lax.rsqrt(var + GN_EPS))
        gate = g_ref[0, pl.ds(r0, c), :]
        o_ref[0, pl.ds(r0, c), :] = jnp.concatenate(outs, axis=1).astype(BF16) * (gate * jax.nn.sigmoid(gate))
        return carry

    lax.fori_loop(0, n_chunks, out_step, 0, unroll=8)


def _retention(q, k, v, g, log_gamma):
    b, _, s, _ = q.shape
    n_chunks = s // CHUNK
    n_pairs = N_RET_HEADS // 2
    dk2 = 2 * RET_QK_DIM
    dv2 = 2 * RET_V_DIM
    return pl.pallas_call(
        functools.partial(_ret_kernel, n_chunks=n_chunks),
        grid=(b, n_pairs),
        in_specs=[
            pl.BlockSpec(memory_space=pltpu.SMEM),
            pl.BlockSpec((1, None, s, dk2), lambda i, p: (i, p, 0, 0)),
            pl.BlockSpec((1, None, s, dk2), lambda i, p: (i, p, 0, 0)),
            pl.BlockSpec((1, None, s, dv2), lambda i, p: (i, p, 0, 0)),
            pl.BlockSpec((1, None, s, dv2), lambda i, p: (i, p, 0, 0)),
        ],
        out_specs=pl.BlockSpec((1, None, s, dv2), lambda i, p: (i, p, 0, 0)),
        out_shape=jax.ShapeDtypeStruct((b, n_pairs, s, dv2), BF16),
        scratch_shapes=[
            pltpu.VMEM((n_chunks, dk2, dv2), BF16),
            pltpu.VMEM((n_chunks, dk2, dv2), BF16),
            pltpu.VMEM((n_chunks, CHUNK, 2 * CHUNK), BF16),
            pltpu.VMEM((dk2, dv2), F32),
            pltpu.VMEM((dk2, dv2), F32),
        ],
        compiler_params=pltpu.CompilerParams(
            dimension_semantics=("parallel", "parallel"), vmem_limit_bytes=VMEM_LIMIT_BYTES),
        name="retention",
    )(log_gamma, q, k, v, g)


def _merge_kernel(x_ref, yf_ref, ret_ref, ga_ref, gb_ref, wf_ref, wr_ref, wo_ref, o_ref, m_ref):
    n_groups = yf_ref.shape[1]
    k2_tile = SUBLANES
    n1 = yf_ref.shape[2] // k2_tile
    yf = jnp.concatenate(
        [jnp.concatenate([yf_ref[0, g, pl.ds(k2l, n1, stride=k2_tile), :] for g in range(n_groups)], axis=1)
         for k2l in range(k2_tile)], axis=0)
    yf = yf.astype(BF16)
    ret = jnp.concatenate([ret_ref[0, p] for p in range(ret_ref.shape[1])], axis=1)
    for lo in range(0, m_ref.shape[1], MERGE_COLS):
        cols = slice(lo, lo + MERGE_COLS)
        a = jnp.dot(yf, wf_ref[:, cols], preferred_element_type=F32)
        bb = jnp.dot(ret, wr_ref[:, cols], preferred_element_type=F32)
        m_ref[:, cols] = (jax.nn.sigmoid(ga_ref[0, :, cols].astype(F32)) * a
                          + jax.nn.sigmoid(gb_ref[0, :, cols].astype(F32)) * bb).astype(BF16)
    o_ref[0] = x_ref[0] + jnp.dot(m_ref[...], wo_ref[...], preferred_element_type=F32)


def _merge(x, y_four, y_ret, g_a, g_b, wf, wr, wo):
    b, s, d = x.shape
    _, n_groups, _, tm, gd = y_four.shape

    def tile(w):
        return pl.BlockSpec((1, tm, w), lambda i, j: (i, j, 0))

    return pl.pallas_call(
        _merge_kernel,
        grid=(b, s // tm),
        in_specs=[tile(d),
                  pl.BlockSpec((1, n_groups, None, tm, gd), lambda i, j: (i, 0, j, 0, 0)),
                  pl.BlockSpec((1,) + y_ret.shape[1:2] + (tm,) + y_ret.shape[3:], lambda i, j: (i, 0, j, 0)),
                  tile(d), tile(d),
                  _resident(wf.shape), _resident(wr.shape), _resident(wo.shape)],
        out_specs=tile(d),
        out_shape=jax.ShapeDtypeStruct((b, s, d), F32),
        scratch_shapes=[pltpu.VMEM((tm, d), BF16)],
        compiler_params=pltpu.CompilerParams(
            dimension_semantics=("parallel", "parallel"), vmem_limit_bytes=VMEM_LIMIT_BYTES),
        name="merge",
    )(x, y_four, y_ret, g_a, g_b, wf, wr, wo)


def _ffn_kernel(xp_ref, x_ref, xn_ref, g2_ref, wu_ref, cw_ref, cb_ref, wd_ref, gf_ref, o_ref,
                hg0_ref, hg1_ref, act_ref):
    tm = x_ref.shape[1]
    halo = SUBLANES
    j = pl.program_id(1)
    last = pl.num_programs(1) - 1
    x = x_ref[0]
    xx = jnp.concatenate([x, xp_ref[0], xn_ref[0]], axis=0)
    u = _rms(xx, g2_ref[...]).astype(BF16)
    u_mid = u[0:tm]
    keep_top = (j > 0).astype(F32)
    keep_bot = (j < last).astype(F32)

    for ci in range(D_FF // FF_CHUNK):
        lo = ci * FF_CHUNK
        hg_ref = (hg0_ref, hg1_ref)[ci % 2]
        hg = jnp.dot(u, wu_ref[:, lo:lo + FF_CHUNK], preferred_element_type=F32)
        hv = jnp.dot(u_mid, wu_ref[:, D_FF + lo:D_FF + lo + FF_CHUNK], preferred_element_type=F32)
        mid = hg[0:tm]
        hg_ref[0:halo, :] = hg[tm:tm + halo] * keep_top
        hg_ref[halo:halo + tm, :] = mid
        hg_ref[halo + tm:, :] = hg[tm + halo:] * keep_bot
        cw = cw_ref[:, lo:lo + FF_CHUNK]
        prev = hg_ref[halo - 1:halo - 1 + tm, :]
        nxt = hg_ref[halo + 1:halo + 1 + tm, :]
        hc = prev * cw[0:1] + mid * cw[1:2] + nxt * cw[2:3] + cb_ref[:, lo:lo + FF_CHUNK]
        act = 0.5 * hc * (1.0 + lax.erf(hc * (2.0 ** -0.5)))
        act_ref[:, lo:lo + FF_CHUNK] = (act * hv).astype(BF16)
    y = jnp.dot(act_ref[...], wd_ref[...], preferred_element_type=F32)
    o_ref[0] = _rms(x + y, gf_ref[...])


def _ffn(x1, norm2_g, w_up, conv_w, conv_b, w_down, final_g):
    b, s, d = x1.shape
    tm = TOKEN_TILE
    hb = tm // SUBLANES
    n_hblocks = s // SUBLANES
    return pl.pallas_call(
        _ffn_kernel,
        grid=(b, s // tm),
        in_specs=[
            pl.BlockSpec((1, SUBLANES, d), lambda i, j: (i, jnp.maximum(j * hb - 1, 0), 0)),
            pl.BlockSpec((1, tm, d), lambda i, j: (i, j, 0)),
            pl.BlockSpec((1, SUBLANES, d), lambda i, j: (i, jnp.minimum((j + 1) * hb, n_hblocks - 1), 0)),
            _resident((1, d)),
            _resident(w_up.shape),
            _resident(conv_w.shape),
            _resident(conv_b.shape),
            _resident(w_down.shape),
            _resident((1, d)),
        ],
        out_specs=pl.BlockSpec((1, tm, d), lambda i, j: (i, j, 0)),
        out_shape=jax.ShapeDtypeStruct((b, s, d), F32),
        scratch_shapes=[pltpu.VMEM((tm + 2 * SUBLANES, FF_CHUNK), F32),
                        pltpu.VMEM((tm + 2 * SUBLANES, FF_CHUNK), F32),
                        pltpu.VMEM((tm, D_FF), BF16)],
        compiler_params=pltpu.CompilerParams(
            dimension_semantics=("parallel", "parallel"), vmem_limit_bytes=VMEM_LIMIT_BYTES),
        name="convglu_ffn",
    )(x1, x1, x1, norm2_g, w_up, conv_w, conv_b, w_down, final_g)


def _rotary_tables(s):
    inv = ROPE_THETA ** (-np.arange(0, RET_QK_DIM, 2, dtype=np.float64) / RET_QK_DIM)
    ang = np.arange(s, dtype=np.float64)[:, None] * inv[None, :]
    cos, sin = np.cos(ang), np.sin(ang)
    reps = LANES // RET_QK_DIM
    cos_tab = np.concatenate([cos, cos] * reps, axis=1).astype(np.float32)
    sin_tab = np.concatenate([-sin, sin] * reps, axis=1).astype(np.float32)
    assert cos_tab.shape == (s, LANES)
    return jnp.asarray(cos_tab), jnp.asarray(sin_tab)


def _encoder_layer(x, p):
    b, s, d = x.shape
    cos_tab, sin_tab = _rotary_tables(s)
    f, q, k, v, g_ret, g_a, g_b = _inproj(x, p["norm1_g"], p["w_in"], cos_tab, sin_tab)
    y_four = _fourier_mix(f)
    y_ret = _retention(q, k, v, g_ret, p["log_gamma"])
    return _merge(x, y_four, y_ret, g_a, g_b, p["w_four_proj"], p["w_ret_proj"], p["w_out"])


def kernel(x_prompt, x_sample, norm1_g, w_in, w_four_proj, w_ret_proj, w_out, ret_decay_logit,
           norm2_g, w_up, conv_w, conv_b, w_down, final_norm_g):
    depth = w_in.shape[0]
    assert depth == 1, "the final RMSNorm is fused into the (single) layer's FFN kernel"
    layers = []
    for l in range(depth):
        layers.append(dict(
            norm1_g=norm1_g[l][None, :],
            w_in=w_in[l].astype(BF16),
            w_four_proj=w_four_proj[l].astype(BF16),
            w_ret_proj=w_ret_proj[l].astype(BF16),
            w_out=w_out[l].astype(BF16),
            log_gamma=jax.nn.log_sigmoid(ret_decay_logit[l].astype(F32)),
            norm2_g=norm2_g[l][None, :],
            w_up=w_up[l].astype(BF16),
            conv_w=conv_w[l],
            conv_b=conv_b[l][None, :],
            w_down=w_down[l].astype(BF16),
        ))
    final_g = final_norm_g[None, :]

    def trunk(x):
        p = layers[0]
        x1 = _encoder_layer(x, p)
        return _ffn(x1, p["norm2_g"], p["w_up"], p["conv_w"], p["conv_b"], p["w_down"], final_g)

    return (trunk(x_prompt), trunk(x_sample))
```

```python
import functools

import numpy as np
import jax
import jax.numpy as jnp
from jax import lax
from jax.experimental import pallas as pl
from jax.experimental.pallas import tpu as pltpu

D_MODEL = 1024
N_FOURIER_GROUPS = 4
FOURIER_GROUP_DIM = 128
FOURIER_WIDTH = N_FOURIER_GROUPS * FOURIER_GROUP_DIM
N_RET_HEADS = 8
RET_QK_DIM = 64
RET_V_DIM = 128
RET_QK_WIDTH = N_RET_HEADS * RET_QK_DIM
RET_V_WIDTH = N_RET_HEADS * RET_V_DIM
CHUNK = 128
ROPE_THETA = 10000.0
D_FF = 2816
NORM_EPS = 1e-6
GN_EPS = 1e-5
IN_SPLITS = (FOURIER_WIDTH, RET_QK_WIDTH, RET_QK_WIDTH, RET_V_WIDTH, RET_V_WIDTH, D_MODEL, D_MODEL)
IN_OFFSETS = tuple(int(o) for o in np.cumsum((0,) + IN_SPLITS))
IN_WIDTH = IN_OFFSETS[-1]

LANES = 128
SUBLANES = 8
VMEM_LIMIT_BYTES = 56 * 1024 * 1024

F32 = jnp.float32
BF16 = jnp.bfloat16

TOKEN_TILE = 1024
FF_CHUNK = 256
FFT_N1 = 128
FFT_STAGE0_ROWS = 512
MERGE_COLS = 256
PITCH_PAD = SUBLANES


def _resident(shape):
    return pl.BlockSpec(shape, lambda *_: (0,) * len(shape), pipeline_mode=pl.Buffered(1))


def _rms(x, g):
    ms = jnp.mean(x * x, axis=-1, keepdims=True)
    return x * lax.rsqrt(ms + NORM_EPS) * g


def _inproj_kernel(x_ref, g_ref, w_ref, cos_ref, sin_ref,
                   f_ref, q_ref, k_ref, v_ref, gr_ref, ga_ref, gb_ref):
    u = _rms(x_ref[0], g_ref[...]).astype(BF16)

    def proj(i):
        return jnp.dot(u, w_ref[:, IN_OFFSETS[i]:IN_OFFSETS[i + 1]], preferred_element_type=F32)

    reps = RET_QK_WIDTH // LANES
    cos = jnp.concatenate([cos_ref[...]] * reps, axis=1)
    sin = jnp.concatenate([sin_ref[...]] * reps, axis=1)
    lane = lax.broadcasted_iota(jnp.int32, cos.shape, 1)
    first_half = (lane % RET_QK_DIM) < (RET_QK_DIM // 2)

    def rotary(t):
        partner = jnp.where(first_half,
                            pltpu.roll(t, RET_QK_WIDTH - RET_QK_DIM // 2, 1),
                            pltpu.roll(t, RET_QK_DIM // 2, 1))
        return t * cos + partner * sin

    def store_split(ref, val):
        parts, w = ref.shape[1], ref.shape[3]
        for p in range(parts):
            ref[0, p] = val[:, p * w:(p + 1) * w].astype(BF16)

    store_split(f_ref, proj(0))
    store_split(q_ref, rotary(proj(1)) * (RET_QK_DIM ** -0.5))
    store_split(k_ref, rotary(proj(2)))
    store_split(v_ref, proj(3))
    store_split(gr_ref, proj(4))
    ga_ref[0] = proj(5).astype(BF16)
    gb_ref[0] = proj(6).astype(BF16)


def _inproj(x, norm_g, w_in_bf16, cos_tab, sin_tab):
    b, s, d = x.shape
    tm = TOKEN_TILE
    parts = (N_FOURIER_GROUPS,) + (N_RET_HEADS // 2,) * 4
    out_shape = [jax.ShapeDtypeStruct((b, p, s, w // p), BF16) for p, w in zip(parts, IN_SPLITS[:5])]
    out_specs = [pl.BlockSpec((1, p, tm, w // p), lambda i, j: (i, 0, j, 0))
                 for p, w in zip(parts, IN_SPLITS[:5])]
    out_shape += [jax.ShapeDtypeStruct((b, s, w), BF16) for w in IN_SPLITS[5:]]
    out_specs += [pl.BlockSpec((1, tm, w), lambda i, j: (i, j, 0)) for w in IN_SPLITS[5:]]
    return pl.pallas_call(
        _inproj_kernel,
        grid=(b, s // tm),
        in_specs=[
            pl.BlockSpec((1, tm, d), lambda i, j: (i, j, 0)),
            _resident((1, d)),
            _resident((d, IN_WIDTH)),
            pl.BlockSpec((tm, LANES), lambda i, j: (j, 0)),
            pl.BlockSpec((tm, LANES), lambda i, j: (j, 0)),
        ],
        out_specs=out_specs,
        out_shape=out_shape,
        compiler_params=pltpu.CompilerParams(
            dimension_semantics=("parallel", "parallel"), vmem_limit_bytes=VMEM_LIMIT_BYTES),
        name="inproj",
    )(x, norm_g, w_in_bf16, cos_tab, sin_tab)


def _fft_kernel(x_ref, fc_ref, f1_ref, g2_ref, o_ref, z_ref, y_ref, *, n1, n2):
    p1 = n2 + PITCH_PAD
    p2 = 2 * n1 + PITCH_PAD
    gd = FOURIER_GROUP_DIM

    slabs = FFT_STAGE0_ROWS // n2

    def stage0(i, carry):
        r0 = pl.multiple_of(i * FFT_STAGE0_ROWS, FFT_STAGE0_ROWS)
        z = jnp.dot(x_ref[0, pl.ds(r0, FFT_STAGE0_ROWS), :], fc_ref[...],
                    preferred_element_type=F32)
        for s in range(slabs):
            row = pl.multiple_of((i * slabs + s) * p1, SUBLANES)
            z_ref[0, pl.ds(row, n2), :] = z[s * n2:(s + 1) * n2, :gd]
            z_ref[1, pl.ds(row, n2), :] = z[s * n2:(s + 1) * n2, gd:]
        return carry

    lax.fori_loop(0, (n1 * n2) // FFT_STAGE0_ROWS, stage0, 0, unroll=16)

    def stage1(jj, carry):
        cols = []
        for t in range(2):
            j = 2 * jj + t
            zr = z_ref[0, pl.ds(j, n1, stride=p1), :]
            zi = z_ref[1, pl.ds(j, n1, stride=p1), :]
            cols.append(jnp.concatenate([zr, zi], axis=0).astype(BF16))
        rhs = jnp.concatenate(cols, axis=1)
        y = jnp.dot(f1_ref[...], rhs, preferred_element_type=F32)
        for t in range(2):
            row = pl.multiple_of((2 * jj + t) * p2, SUBLANES)
            y_ref[pl.ds(row, 2 * n1), :] = y[:, t * gd:(t + 1) * gd]
        return carry

    lax.fori_loop(0, n2 // 2, stage1, 0, unroll=32)

    def stage2(k1, carry):
        yr = y_ref[pl.ds(k1, n2, stride=p2), :]
        yi = y_ref[pl.ds(n1 + k1, n2, stride=p2), :]
        rhs = jnp.concatenate([yr, yi], axis=0).astype(BF16)
        out = jnp.dot(g2_ref[k1], rhs, preferred_element_type=F32)
        row = pl.multiple_of(k1 * SUBLANES, SUBLANES)
        o_ref[0, 0, :, pl.ds(row, SUBLANES), :] = out.reshape(n2 // SUBLANES, SUBLANES, gd)
        return carry

    lax.fori_loop(0, n1, stage2, 0, unroll=64)


def _fft_tables(s, n1, n2):
    c = FOURIER_GROUP_DIM
    ic = np.arange(c)
    ang_c = (2.0 * np.pi / c) * ((ic[:, None] * ic[None, :]) % c)
    fc = np.concatenate([np.cos(ang_c), -np.sin(ang_c)], axis=1) * c ** -0.5
    i1 = np.arange(n1)
    ang_1 = (2.0 * np.pi / n1) * ((i1[:, None] * i1[None, :]) % n1)
    c1, s1 = np.cos(ang_1), np.sin(ang_1)
    f1 = np.concatenate([np.concatenate([c1, s1], axis=1),
                         np.concatenate([-s1, c1], axis=1)], axis=0) * n1 ** -0.5
    k1 = np.arange(n1)[:, None, None]
    k2 = np.arange(n2)[None, :, None]
    m2 = np.arange(n2)[None, None, :]
    ang_2 = (2.0 * np.pi / s) * ((m2 * (k1 + n1 * k2)) % s)
    g2 = np.concatenate([np.cos(ang_2), np.sin(ang_2)], axis=2) * n2 ** -0.5
    return tuple(jnp.asarray(t.astype(np.float32)).astype(BF16) for t in (fc, f1, g2))


def _fourier_mix(f):
    b, _, s, _ = f.shape
    n1 = FFT_N1
    n2 = s // n1
    fc, f1, g2 = _fft_tables(s, n1, n2)
    gd = FOURIER_GROUP_DIM
    p1 = n2 + PITCH_PAD
    p2 = 2 * n1 + PITCH_PAD
    return pl.pallas_call(
        functools.partial(_fft_kernel, n1=n1, n2=n2),
        grid=(b, N_FOURIER_GROUPS),
        in_specs=[
            pl.BlockSpec((1, None, s, gd), lambda i, g: (i, g, 0, 0)),
            _resident((gd, 2 * gd)),
            _resident((2 * n1, 2 * n1)),
            _resident((n1, n2, 2 * n2)),
        ],
        out_specs=pl.BlockSpec((1, 1, n2 // SUBLANES, n1 * SUBLANES, gd), lambda i, g: (i, g, 0, 0, 0)),
        out_shape=jax.ShapeDtypeStruct((b, N_FOURIER_GROUPS, n2 // SUBLANES, n1 * SUBLANES, gd), F32),
        scratch_shapes=[
            pltpu.VMEM((2, n1 * p1, gd), F32),
            pltpu.VMEM((n2 * p2, gd), F32),
        ],
        compiler_params=pltpu.CompilerParams(
            dimension_semantics=("parallel", "parallel"), vmem_limit_bytes=VMEM_LIMIT_BYTES),
        name="fourier_mix",
    )(f, fc, f1, g2)


def _ret_kernel(lg_ref, q_ref, k_ref, v_ref, g_ref, o_ref,
                sf_ref, sb_ref, p_ref, fr_ref, br_ref, *, n_chunks):
    c = CHUNK
    dk2 = 2 * RET_QK_DIM
    dv = RET_V_DIM
    pair = pl.program_id(1)
    lgf = [lg_ref[0, 2 * pair + h] for h in range(2)]
    lgb = [lg_ref[1, 2 * pair + h] for h in range(2)]

    row = lax.broadcasted_iota(jnp.int32, (c, c), 0)
    col = lax.broadcasted_iota(jnp.int32, (c, c), 1)
    diff = (row - col).astype(F32)
    decay2 = jnp.concatenate(
        [jnp.where(diff >= 0.0,
                   jnp.exp(lgf[h] * jnp.maximum(diff, 0.0)),
                   jnp.exp(lgb[h] * jnp.maximum(-diff, 0.0))) for h in range(2)], axis=1)

    pos = lax.broadcasted_iota(jnp.int32, (c, dk2), 0).astype(F32)
    head_a = lax.broadcasted_iota(jnp.int32, (c, dk2), 1) < RET_QK_DIM

    def per_head(fa, fb):
        return jnp.where(head_a, fa, fb).astype(BF16)

    q_dec_f = per_head(jnp.exp(lgf[0] * (pos + 1.0)), jnp.exp(lgf[1] * (pos + 1.0)))
    q_dec_b = per_head(jnp.exp(lgb[0] * (c - pos)), jnp.exp(lgb[1] * (c - pos)))
    k_dec_f = per_head(jnp.exp(lgf[0] * (c - 1.0 - pos)), jnp.exp(lgf[1] * (c - 1.0 - pos)))
    k_dec_b = per_head(jnp.exp(lgb[0] * pos), jnp.exp(lgb[1] * pos))

    srow = lax.broadcasted_iota(jnp.int32, (dk2, 2 * dv), 0) < RET_QK_DIM
    scol = lax.broadcasted_iota(jnp.int32, (dk2, 2 * dv), 1) < dv
    block_diag = srow == scol
    chunk_f = jnp.where(scol, jnp.exp(lgf[0] * c), jnp.exp(lgf[1] * c))
    chunk_b = jnp.where(scol, jnp.exp(lgb[0] * c), jnp.exp(lgb[1] * c))

    def kv_update(state, k_dec, chunk_dec, kc, vc):
        kd = kc * k_dec
        kv = lax.dot_general(kd, vc, (((0,), (0,)), ((), ())), preferred_element_type=F32)
        return chunk_dec * state + jnp.where(block_diag, kv, 0.0)

    fr_ref[...] = jnp.zeros_like(fr_ref)
    br_ref[...] = jnp.zeros_like(br_ref)

    def scan_step(t, carry):
        for run_ref, out_ref, i, k_dec, chunk_dec in (
                (fr_ref, sf_ref, t, k_dec_f, chunk_f),
                (br_ref, sb_ref, n_chunks - 1 - t, k_dec_b, chunk_b)):
            r0 = pl.multiple_of(i * c, c)
            state = run_ref[...]
            out_ref[i] = state.astype(BF16)
            run_ref[...] = kv_update(state, k_dec, chunk_dec,
                                     k_ref[0, pl.ds(r0, c), :], v_ref[0, pl.ds(r0, c), :])
        r0 = pl.multiple_of(t * c, c)
        kc = k_ref[0, pl.ds(r0, c), :]
        zero = jnp.zeros_like(kc)
        k_heads = jnp.concatenate([jnp.where(head_a, kc, zero), jnp.where(head_a, zero, kc)], axis=0)
        scores = lax.dot_general(q_ref[0, pl.ds(r0, c), :], k_heads, (((1,), (1,)), ((), ())),
                                 preferred_element_type=F32)
        p_ref[t] = (scores * decay2).astype(BF16)
        return carry

    lax.fori_loop(0, n_chunks, scan_step, 0, unroll=16)

    def out_step(i, carry):
        r0 = pl.multiple_of(i * c, c)
        qc = q_ref[0, pl.ds(r0, c), :]
        vc = v_ref[0, pl.ds(r0, c), :]
        q_inter = jnp.concatenate([qc * q_dec_f, qc * q_dec_b], axis=1)
        states = jnp.concatenate([sf_ref[i], sb_ref[i]], axis=0)
        inter = jnp.dot(q_inter, states, preferred_element_type=F32)
        outs = []
        for h in range(2):
            o = jnp.dot(p_ref[i, :, h * c:(h + 1) * c], vc[:, h * dv:(h + 1) * dv],
                        preferred_element_type=F32)
            o = o + inter[:, h * dv:(h + 1) * dv]
            mu = jnp.mean(o, axis=-1, keepdims=True)
            d = o - mu
            var = jnp.mean(d * d, axis=-1, keepdims=True)
            outs.append(d * lax.rsqrt(var + GN_EPS))
        gate = g_ref[0, pl.ds(r0, c), :]
        o_ref[0, pl.ds(r0, c), :] = jnp.concatenate(outs, axis=1).astype(BF16) * (gate * jax.nn.sigmoid(gate))
        return carry

    lax.fori_loop(0, n_chunks, out_step, 0, unroll=8)


def _retention(q, k, v, g, log_gamma):
    b, _, s, _ = q.shape
    n_chunks = s // CHUNK
    n_pairs = N_RET_HEADS // 2
    dk2 = 2 * RET_QK_DIM
    dv2 = 2 * RET_V_DIM
    return pl.pallas_call(
        functools.partial(_ret_kernel, n_chunks=n_chunks),
        grid=(b, n_pairs),
        in_specs=[
            pl.BlockSpec(memory_space=pltpu.SMEM),
            pl.BlockSpec((1, None, s, dk2), lambda i, p: (i, p, 0, 0)),
            pl.BlockSpec((1, None, s, dk2), lambda i, p: (i, p, 0, 0)),
            pl.BlockSpec((1, None, s, dv2), lambda i, p: (i, p, 0, 0)),
            pl.BlockSpec((1, None, s, dv2), lambda i, p: (i, p, 0, 0)),
        ],
        out_specs=pl.BlockSpec((1, None, s, dv2), lambda i, p: (i, p, 0, 0)),
        out_shape=jax.ShapeDtypeStruct((b, n_pairs, s, dv2), BF16),
        scratch_shapes=[
            pltpu.VMEM((n_chunks, dk2, dv2), BF16),
            pltpu.VMEM((n_chunks, dk2, dv2), BF16),
            pltpu.VMEM((n_chunks, CHUNK, 2 * CHUNK), BF16),
            pltpu.VMEM((dk2, dv2), F32),
            pltpu.VMEM((dk2, dv2), F32),
        ],
        compiler_params=pltpu.CompilerParams(
            dimension_semantics=("parallel", "parallel"), vmem_limit_bytes=VMEM_LIMIT_BYTES),
        name="retention",
    )(log_gamma, q, k, v, g)


def _merge_kernel(x_ref, yf_ref, ret_ref, ga_ref, gb_ref, wf_ref, wr_ref, wo_ref, o_ref, m_ref):
    n_groups = yf_ref.shape[1]
    k2_tile = SUBLANES
    n1 = yf_ref.shape[2] // k2_tile
    yf = jnp.concatenate(
        [jnp.concatenate([yf_ref[0, g, pl.ds(k2l, n1, stride=k2_tile), :] for g in range(n_groups)], axis=1)
         for k2l in range(k2_tile)], axis=0)
    yf = yf.astype(BF16)
    ret = jnp.concatenate([ret_ref[0, p] for p in range(ret_ref.shape[1])], axis=1)
    for lo in range(0, m_ref.shape[1], MERGE_COLS):
        cols = slice(lo, lo + MERGE_COLS)
        a = jnp.dot(yf, wf_ref[:, cols], preferred_element_type=F32)
        bb = jnp.dot(ret, wr_ref[:, cols], preferred_element_type=F32)
        m_ref[:, cols] = (jax.nn.sigmoid(ga_ref[0, :, cols].astype(F32)) * a
                          + jax.nn.sigmoid(gb_ref[0, :, cols].astype(F32)) * bb).astype(BF16)
    o_ref[0] = x_ref[0] + jnp.dot(m_ref[...], wo_ref[...], preferred_element_type=F32)


def _merge(x, y_four, y_ret, g_a, g_b, wf, wr, wo):
    b, s, d = x.shape
    _, n_groups, _, tm, gd = y_four.shape

    def tile(w):
        return pl.BlockSpec((1, tm, w), lambda i, j: (i, j, 0))

    return pl.pallas_call(
        _merge_kernel,
        grid=(b, s // tm),
        in_specs=[tile(d),
                  pl.BlockSpec((1, n_groups, None, tm, gd), lambda i, j: (i, 0, j, 0, 0)),
                  pl.BlockSpec((1,) + y_ret.shape[1:2] + (tm,) + y_ret.shape[3:], lambda i, j: (i, 0, j, 0)),
                  tile(d), tile(d),
                  _resident(wf.shape), _resident(wr.shape), _resident(wo.shape)],
        out_specs=tile(d),
        out_shape=jax.ShapeDtypeStruct((b, s, d), F32),
        scratch_shapes=[pltpu.VMEM((tm, d), BF16)],
        compiler_params=pltpu.CompilerParams(
            dimension_semantics=("parallel", "parallel"), vmem_limit_bytes=VMEM_LIMIT_BYTES),
        name="merge",
    )(x, y_four, y_ret, g_a, g_b, wf, wr, wo)


def _ffn_kernel(xp_ref, x_ref, xn_ref, g2_ref, wu_ref, cw_ref, cb_ref, wd_ref, gf_ref, o_ref, act_ref):
    tm = x_ref.shape[1]
    halo = SUBLANES
    j = pl.program_id(1)
    last = pl.num_programs(1) - 1
    x = x_ref[0]
    xx = jnp.concatenate([x, xp_ref[0], xn_ref[0]], axis=0)
    u = _rms(xx, g2_ref[...]).astype(BF16)
    u_mid = u[0:tm]
    keep_top = (j > 0).astype(F32)
    keep_bot = (j < last).astype(F32)

    for ci in range(D_FF // FF_CHUNK):
        lo = ci * FF_CHUNK
        hg = jnp.dot(u, wu_ref[:, lo:lo + FF_CHUNK], preferred_element_type=F32)
        hv = jnp.dot(u_mid, wu_ref[:, D_FF + lo:D_FF + lo + FF_CHUNK], preferred_element_type=F32)
        mid = hg[0:tm]
        seq = jnp.concatenate([hg[tm:tm + halo] * keep_top, mid, hg[tm + halo:] * keep_bot], axis=0)
        cw = cw_ref[:, lo:lo + FF_CHUNK]
        prev = pltpu.roll(seq, 1, 0)[halo:halo + tm]
        nxt = pltpu.roll(seq, tm + 2 * halo - 1, 0)[halo:halo + tm]
        hc = prev * cw[0:1] + mid * cw[1:2] + nxt * cw[2:3] + cb_ref[:, lo:lo + FF_CHUNK]
        act = 0.5 * hc * (1.0 + lax.erf(hc * (2.0 ** -0.5)))
        act_ref[:, lo:lo + FF_CHUNK] = (act * hv).astype(BF16)
    y = jnp.dot(act_ref[...], wd_ref[...], preferred_element_type=F32)
    o_ref[0] = _rms(x + y, gf_ref[...])


def _ffn(x1, norm2_g, w_up, conv_w, conv_b, w_down, final_g):
    b, s, d = x1.shape
    tm = TOKEN_TILE
    hb = tm // SUBLANES
    n_hblocks = s // SUBLANES
    return pl.pallas_call(
        _ffn_kernel,
        grid=(b, s // tm),
        in_specs=[
            pl.BlockSpec((1, SUBLANES, d), lambda i, j: (i, jnp.maximum(j * hb - 1, 0), 0)),
            pl.BlockSpec((1, tm, d), lambda i, j: (i, j, 0)),
            pl.BlockSpec((1, SUBLANES, d), lambda i, j: (i, jnp.minimum((j + 1) * hb, n_hblocks - 1), 0)),
            _resident((1, d)),
            _resident(w_up.shape),
            _resident(conv_w.shape),
            _resident(conv_b.shape),
            _resident(w_down.shape),
            _resident((1, d)),
        ],
        out_specs=pl.BlockSpec((1, tm, d), lambda i, j: (i, j, 0)),
        out_shape=jax.ShapeDtypeStruct((b, s, d), F32),
        scratch_shapes=[pltpu.VMEM((tm, D_FF), BF16)],
        compiler_params=pltpu.CompilerParams(
            dimension_semantics=("parallel", "parallel"), vmem_limit_bytes=VMEM_LIMIT_BYTES),
        name="convglu_ffn",
    )(x1, x1, x1, norm2_g, w_up, conv_w, conv_b, w_down, final_g)


def _rotary_tables(s):
    inv = ROPE_THETA ** (-np.arange(0, RET_QK_DIM, 2, dtype=np.float64) / RET_QK_DIM)
    ang = np.arange(s, dtype=np.float64)[:, None] * inv[None, :]
    cos, sin = np.cos(ang), np.sin(ang)
    reps = LANES // RET_QK_DIM
    cos_tab = np.concatenate([cos, cos] * reps, axis=1).astype(np.float32)
    sin_tab = np.concatenate([-sin, sin] * reps, axis=1).astype(np.float32)
    assert cos_tab.shape == (s, LANES)
    return jnp.asarray(cos_tab), jnp.asarray(sin_tab)


def _encoder_layer(x, p):
    b, s, d = x.shape
    cos_tab, sin_tab = _rotary_tables(s)
    f, q, k, v, g_ret, g_a, g_b = _inproj(x, p["norm1_g"], p["w_in"], cos_tab, sin_tab)
    y_four = _fourier_mix(f)
    y_ret = _retention(q, k, v, g_ret, p["log_gamma"])
    return _merge(x, y_four, y_ret, g_a, g_b, p["w_four_proj"], p["w_ret_proj"], p["w_out"])


def kernel(x_prompt, x_sample, norm1_g, w_in, w_four_proj, w_ret_proj, w_out, ret_decay_logit,
           norm2_g, w_up, conv_w, conv_b, w_down, final_norm_g):
    depth = w_in.shape[0]
    assert depth == 1, "the final RMSNorm is fused into the (single) layer's FFN kernel"
    layers = []
    for l in range(depth):
        layers.append(dict(
            norm1_g=norm1_g[l][None, :],
            w_in=w_in[l].astype(BF16),
            w_four_proj=w_four_proj[l].astype(BF16),
            w_ret_proj=w_ret_proj[l].astype(BF16),
            w_out=w_out[l].astype(BF16),
            log_gamma=jax.nn.log_sigmoid(ret_decay_logit[l].astype(F32)),
            norm2_g=norm2_g[l][None, :],
            w_up=w_up[l].astype(BF16),
            conv_w=conv_w[l],
            conv_b=conv_b[l][None, :],
            w_down=w_down[l].astype(BF16),
        ))
    final_g = final_norm_g[None, :]

    def trunk(x):
        p = layers[0]
        x1 = _encoder_layer(x, p)
        return _ffn(x1, p["norm2_g"], p["w_up"], p["conv_w"], p["conv_b"], p["w_down"], final_g)

    return (trunk(x_prompt), trunk(x_sample))
```

```python
import functools

import numpy as np
import jax
import jax.numpy as jnp
from jax import lax
from jax.experimental import pallas as pl
from jax.experimental.pallas import tpu as pltpu

D_MODEL = 1024
N_FOURIER_GROUPS = 4
FOURIER_GROUP_DIM = 128
FOURIER_WIDTH = N_FOURIER_GROUPS * FOURIER_GROUP_DIM
N_RET_HEADS = 8
RET_QK_DIM = 64
RET_V_DIM = 128
RET_QK_WIDTH = N_RET_HEADS * RET_QK_DIM
RET_V_WIDTH = N_RET_HEADS * RET_V_DIM
CHUNK = 128
ROPE_THETA = 10000.0
D_FF = 2816
NORM_EPS = 1e-6
GN_EPS = 1e-5
INV_SQRT2 = 2.0 ** -0.5
IN_SPLITS = (FOURIER_WIDTH, RET_QK_WIDTH, RET_QK_WIDTH, RET_V_WIDTH, RET_V_WIDTH, D_MODEL, D_MODEL)
IN_OFFSETS = tuple(int(o) for o in np.cumsum((0,) + IN_SPLITS))
IN_WIDTH = IN_OFFSETS[-1]

LANES = 128
SUBLANES = 8
VMEM_LIMIT_BYTES = 56 * 1024 * 1024

F32 = jnp.float32
BF16 = jnp.bfloat16

TOKEN_TILE = 1024
FF_CHUNK = 256
FFT_N1 = 128
FFT_STAGE0_ROWS = 512
MERGE_COLS = 256
PITCH_PAD = SUBLANES


def _resident(shape):
    return pl.BlockSpec(shape, lambda *_: (0,) * len(shape), pipeline_mode=pl.Buffered(1))


def _rms(x, g):
    ms = jnp.mean(x * x, axis=-1, keepdims=True)
    return x * lax.rsqrt(ms + NORM_EPS) * g


def _inproj_kernel(x_ref, g_ref, w_ref, cos_ref, sin_ref,
                   f_ref, q_ref, k_ref, v_ref, gr_ref, ga_ref, gb_ref):
    u = _rms(x_ref[0], g_ref[...]).astype(BF16)

    def proj(i):
        return jnp.dot(u, w_ref[:, IN_OFFSETS[i]:IN_OFFSETS[i + 1]], preferred_element_type=F32)

    reps = RET_QK_WIDTH // LANES
    cos = jnp.concatenate([cos_ref[...]] * reps, axis=1)
    sin = jnp.concatenate([sin_ref[...]] * reps, axis=1)
    lane = lax.broadcasted_iota(jnp.int32, cos.shape, 1)
    first_half = (lane % RET_QK_DIM) < (RET_QK_DIM // 2)

    def rotary(t):
        partner = jnp.where(first_half,
                            pltpu.roll(t, RET_QK_WIDTH - RET_QK_DIM // 2, 1),
                            pltpu.roll(t, RET_QK_DIM // 2, 1))
        return t * cos + partner * sin

    def store_split(ref, val):
        parts, w = ref.shape[1], ref.shape[3]
        for p in range(parts):
            ref[0, p] = val[:, p * w:(p + 1) * w].astype(BF16)

    store_split(f_ref, proj(0))
    store_split(q_ref, rotary(proj(1)) * (RET_QK_DIM ** -0.5))
    store_split(k_ref, rotary(proj(2)))
    store_split(v_ref, proj(3))
    g_ret = proj(4)
    store_split(gr_ref, g_ret * jax.nn.sigmoid(g_ret))
    ga_ref[0] = proj(5).astype(BF16)
    gb_ref[0] = proj(6).astype(BF16)


def _inproj(x, norm_g, w_in_bf16, cos_tab, sin_tab):
    b, s, d = x.shape
    tm = TOKEN_TILE
    parts = (N_FOURIER_GROUPS,) + (N_RET_HEADS // 2,) * 4
    out_shape = [jax.ShapeDtypeStruct((b, p, s, w // p), BF16) for p, w in zip(parts, IN_SPLITS[:5])]
    out_specs = [pl.BlockSpec((1, p, tm, w // p), lambda i, j: (i, 0, j, 0))
                 for p, w in zip(parts, IN_SPLITS[:5])]
    out_shape += [jax.ShapeDtypeStruct((b, s, w), BF16) for w in IN_SPLITS[5:]]
    out_specs += [pl.BlockSpec((1, tm, w), lambda i, j: (i, j, 0)) for w in IN_SPLITS[5:]]
    return pl.pallas_call(
        _inproj_kernel,
        grid=(b, s // tm),
        in_specs=[
            pl.BlockSpec((1, tm, d), lambda i, j: (i, j, 0)),
            _resident((1, d)),
            _resident((d, IN_WIDTH)),
            pl.BlockSpec((tm, LANES), lambda i, j: (j, 0)),
            pl.BlockSpec((tm, LANES), lambda i, j: (j, 0)),
        ],
        out_specs=out_specs,
        out_shape=out_shape,
        compiler_params=pltpu.CompilerParams(
            dimension_semantics=("parallel", "parallel"), vmem_limit_bytes=VMEM_LIMIT_BYTES),
        name="inproj",
    )(x, norm_g, w_in_bf16, cos_tab, sin_tab)


def _fft_kernel(x_ref, fc_ref, f1_ref, g2_ref, o_ref, z_ref, y_ref, *, n1, n2):
    p1 = n2 + PITCH_PAD
    p2 = 2 * n1 + PITCH_PAD
    gd = FOURIER_GROUP_DIM

    slabs = FFT_STAGE0_ROWS // n2

    def stage0(i, carry):
        r0 = pl.multiple_of(i * FFT_STAGE0_ROWS, FFT_STAGE0_ROWS)
        z = jnp.dot(x_ref[0, pl.ds(r0, FFT_STAGE0_ROWS), :], fc_ref[...],
                    preferred_element_type=F32)
        for s in range(slabs):
            row = pl.multiple_of((i * slabs + s) * p1, SUBLANES)
            z_ref[0, pl.ds(row, n2), :] = z[s * n2:(s + 1) * n2, :gd]
            z_ref[1, pl.ds(row, n2), :] = z[s * n2:(s + 1) * n2, gd:]
        return carry

    lax.fori_loop(0, (n1 * n2) // FFT_STAGE0_ROWS, stage0, 0, unroll=16)

    def stage1(jj, carry):
        cols = []
        for t in range(2):
            j = 2 * jj + t
            zr = z_ref[0, pl.ds(j, n1, stride=p1), :]
            zi = z_ref[1, pl.ds(j, n1, stride=p1), :]
            cols.append(jnp.concatenate([zr, zi], axis=0).astype(BF16))
        rhs = jnp.concatenate(cols, axis=1)
        y = jnp.dot(f1_ref[...], rhs, preferred_element_type=F32)
        for t in range(2):
            row = pl.multiple_of((2 * jj + t) * p2, SUBLANES)
            y_ref[pl.ds(row, 2 * n1), :] = y[:, t * gd:(t + 1) * gd]
        return carry

    lax.fori_loop(0, n2 // 2, stage1, 0, unroll=32)

    def stage2(k1, carry):
        yr = y_ref[pl.ds(k1, n2, stride=p2), :]
        yi = y_ref[pl.ds(n1 + k1, n2, stride=p2), :]
        rhs = jnp.concatenate([yr, yi], axis=0).astype(BF16)
        out = jnp.dot(g2_ref[k1], rhs, preferred_element_type=F32)
        row = pl.multiple_of(k1 * SUBLANES, SUBLANES)
        o_ref[0, 0, :, pl.ds(row, SUBLANES), :] = out.reshape(n2 // SUBLANES, SUBLANES, gd)
        return carry

    lax.fori_loop(0, n1, stage2, 0, unroll=128)


def _fft_tables(s, n1, n2):
    c = FOURIER_GROUP_DIM
    ic = np.arange(c)
    ang_c = (2.0 * np.pi / c) * ((ic[:, None] * ic[None, :]) % c)
    fc = np.concatenate([np.cos(ang_c), -np.sin(ang_c)], axis=1) * c ** -0.5
    i1 = np.arange(n1)
    ang_1 = (2.0 * np.pi / n1) * ((i1[:, None] * i1[None, :]) % n1)
    c1, s1 = np.cos(ang_1), np.sin(ang_1)
    f1 = np.concatenate([np.concatenate([c1, s1], axis=1),
                         np.concatenate([-s1, c1], axis=1)], axis=0) * n1 ** -0.5
    k1 = np.arange(n1)[:, None, None]
    k2 = np.arange(n2)[None, :, None]
    m2 = np.arange(n2)[None, None, :]
    ang_2 = (2.0 * np.pi / s) * ((m2 * (k1 + n1 * k2)) % s)
    g2 = np.concatenate([np.cos(ang_2), np.sin(ang_2)], axis=2) * n2 ** -0.5
    return tuple(jnp.asarray(t.astype(np.float32)).astype(BF16) for t in (fc, f1, g2))


def _fourier_mix(f):
    b, _, s, _ = f.shape
    n1 = FFT_N1
    n2 = s // n1
    fc, f1, g2 = _fft_tables(s, n1, n2)
    gd = FOURIER_GROUP_DIM
    p1 = n2 + PITCH_PAD
    p2 = 2 * n1 + PITCH_PAD
    return pl.pallas_call(
        functools.partial(_fft_kernel, n1=n1, n2=n2),
        grid=(b, N_FOURIER_GROUPS),
        in_specs=[
            pl.BlockSpec((1, None, s, gd), lambda i, g: (i, g, 0, 0)),
            _resident((gd, 2 * gd)),
            _resident((2 * n1, 2 * n1)),
            _resident((n1, n2, 2 * n2)),
        ],
        out_specs=pl.BlockSpec((1, 1, n2 // SUBLANES, n1 * SUBLANES, gd), lambda i, g: (i, g, 0, 0, 0)),
        out_shape=jax.ShapeDtypeStruct((b, N_FOURIER_GROUPS, n2 // SUBLANES, n1 * SUBLANES, gd), F32),
        scratch_shapes=[
            pltpu.VMEM((2, n1 * p1, gd), F32),
            pltpu.VMEM((n2 * p2, gd), F32),
        ],
        compiler_params=pltpu.CompilerParams(
            dimension_semantics=("parallel", "parallel"), vmem_limit_bytes=VMEM_LIMIT_BYTES),
        name="fourier_mix",
    )(f, fc, f1, g2)


def _ret_kernel(lg_ref, q_ref, k_ref, v_ref, g_ref, o_ref,
                sf_ref, sb_ref, p_ref, fr_ref, br_ref, *, n_chunks):
    c = CHUNK
    dk2 = 2 * RET_QK_DIM
    dv = RET_V_DIM
    pair = pl.program_id(1)
    lgf = [lg_ref[0, 2 * pair + h] for h in range(2)]
    lgb = [lg_ref[1, 2 * pair + h] for h in range(2)]

    row = lax.broadcasted_iota(jnp.int32, (c, c), 0)
    col = lax.broadcasted_iota(jnp.int32, (c, c), 1)
    diff = (row - col).astype(F32)
    decay2 = jnp.concatenate(
        [jnp.where(diff >= 0.0,
                   jnp.exp(lgf[h] * jnp.maximum(diff, 0.0)),
                   jnp.exp(lgb[h] * jnp.maximum(-diff, 0.0))) for h in range(2)], axis=1)

    pos = lax.broadcasted_iota(jnp.int32, (c, dk2), 0).astype(F32)
    head_a = lax.broadcasted_iota(jnp.int32, (c, dk2), 1) < RET_QK_DIM

    def per_head(fa, fb):
        return jnp.where(head_a, fa, fb).astype(BF16)

    q_dec_f = per_head(jnp.exp(lgf[0] * (pos + 1.0)), jnp.exp(lgf[1] * (pos + 1.0)))
    q_dec_b = per_head(jnp.exp(lgb[0] * (c - pos)), jnp.exp(lgb[1] * (c - pos)))
    k_dec_f = per_head(jnp.exp(lgf[0] * (c - 1.0 - pos)), jnp.exp(lgf[1] * (c - 1.0 - pos)))
    k_dec_b = per_head(jnp.exp(lgb[0] * pos), jnp.exp(lgb[1] * pos))

    srow = lax.broadcasted_iota(jnp.int32, (dk2, 2 * dv), 0) < RET_QK_DIM
    scol = lax.broadcasted_iota(jnp.int32, (dk2, 2 * dv), 1) < dv
    block_diag = srow == scol
    chunk_f = jnp.where(scol, jnp.exp(lgf[0] * c), jnp.exp(lgf[1] * c))
    chunk_b = jnp.where(scol, jnp.exp(lgb[0] * c), jnp.exp(lgb[1] * c))

    def kv_update(state, k_dec, chunk_dec, kc, vc):
        kd = kc * k_dec
        kv = lax.dot_general(kd, vc, (((0,), (0,)), ((), ())), preferred_element_type=F32)
        return chunk_dec * state + jnp.where(block_diag, kv, 0.0)

    fr_ref[...] = jnp.zeros_like(fr_ref)
    br_ref[...] = jnp.zeros_like(br_ref)

    def scan_step(t, carry):
        for run_ref, out_ref, i, k_dec, chunk_dec in (
                (fr_ref, sf_ref, t, k_dec_f, chunk_f),
                (br_ref, sb_ref, n_chunks - 1 - t, k_dec_b, chunk_b)):
            r0 = pl.multiple_of(i * c, c)
            state = run_ref[...]
            out_ref[i] = state.astype(BF16)
            run_ref[...] = kv_update(state, k_dec, chunk_dec,
                                     k_ref[0, pl.ds(r0, c), :], v_ref[0, pl.ds(r0, c), :])
        r0 = pl.multiple_of(t * c, c)
        kc = k_ref[0, pl.ds(r0, c), :]
        zero = jnp.zeros_like(kc)
        k_heads = jnp.concatenate([jnp.where(head_a, kc, zero), jnp.where(head_a, zero, kc)], axis=0)
        scores = lax.dot_general(q_ref[0, pl.ds(r0, c), :], k_heads, (((1,), (1,)), ((), ())),
                                 preferred_element_type=F32)
        p_ref[t] = (scores * decay2).astype(BF16)
        return carry

    lax.fori_loop(0, n_chunks, scan_step, 0, unroll=16)

    def out_step(i, carry):
        r0 = pl.multiple_of(i * c, c)
        qc = q_ref[0, pl.ds(r0, c), :]
        vc = v_ref[0, pl.ds(r0, c), :]
        q_inter = jnp.concatenate([qc * q_dec_f, qc * q_dec_b], axis=1)
        states = jnp.concatenate([sf_ref[i], sb_ref[i]], axis=0)
        inter = jnp.dot(q_inter, states, preferred_element_type=F32)
        outs = []
        for h in range(2):
            o = jnp.dot(p_ref[i, :, h * c:(h + 1) * c], vc[:, h * dv:(h + 1) * dv],
                        preferred_element_type=F32)
            o = o + inter[:, h * dv:(h + 1) * dv]
            mu = jnp.mean(o, axis=-1, keepdims=True)
            d = o - mu
            var = jnp.mean(d * d, axis=-1, keepdims=True)
            outs.append(d * lax.rsqrt(var + GN_EPS))
        gate = g_ref[0, pl.ds(r0, c), :]
        o_ref[0, pl.ds(r0, c), :] = jnp.concatenate(outs, axis=1).astype(BF16) * gate
        return carry

    lax.fori_loop(0, n_chunks, out_step, 0, unroll=8)


def _retention(q, k, v, g, log_gamma):
    b, _, s, _ = q.shape
    n_chunks = s // CHUNK
    n_pairs = N_RET_HEADS // 2
    dk2 = 2 * RET_QK_DIM
    dv2 = 2 * RET_V_DIM
    return pl.pallas_call(
        functools.partial(_ret_kernel, n_chunks=n_chunks),
        grid=(b, n_pairs),
        in_specs=[
            pl.BlockSpec(memory_space=pltpu.SMEM),
            pl.BlockSpec((1, None, s, dk2), lambda i, p: (i, p, 0, 0)),
            pl.BlockSpec((1, None, s, dk2), lambda i, p: (i, p, 0, 0)),
            pl.BlockSpec((1, None, s, dv2), lambda i, p: (i, p, 0, 0)),
            pl.BlockSpec((1, None, s, dv2), lambda i, p: (i, p, 0, 0)),
        ],
        out_specs=pl.BlockSpec((1, None, s, dv2), lambda i, p: (i, p, 0, 0)),
        out_shape=jax.ShapeDtypeStruct((b, n_pairs, s, dv2), BF16),
        scratch_shapes=[
            pltpu.VMEM((n_chunks, dk2, dv2), BF16),
            pltpu.VMEM((n_chunks, dk2, dv2), BF16),
            pltpu.VMEM((n_chunks, CHUNK, 2 * CHUNK), BF16),
            pltpu.VMEM((dk2, dv2), F32),
            pltpu.VMEM((dk2, dv2), F32),
        ],
        compiler_params=pltpu.CompilerParams(
            dimension_semantics=("parallel", "parallel"), vmem_limit_bytes=VMEM_LIMIT_BYTES),
        name="retention",
    )(log_gamma, q, k, v, g)


def _merge_kernel(x_ref, yf_ref, ret_ref, ga_ref, gb_ref, wf_ref, wr_ref, wo_ref, o_ref, m_ref):
    n_groups = yf_ref.shape[1]
    k2_tile = SUBLANES
    n1 = yf_ref.shape[2] // k2_tile
    yf = jnp.concatenate(
        [jnp.concatenate([yf_ref[0, g, pl.ds(k2l, n1, stride=k2_tile), :] for g in range(n_groups)], axis=1)
         for k2l in range(k2_tile)], axis=0)
    yf = yf.astype(BF16)
    ret = jnp.concatenate([ret_ref[0, p] for p in range(ret_ref.shape[1])], axis=1)
    for lo in range(0, m_ref.shape[1], MERGE_COLS):
        cols = slice(lo, lo + MERGE_COLS)
        a = jnp.dot(yf, wf_ref[:, cols], preferred_element_type=F32)
        bb = jnp.dot(ret, wr_ref[:, cols], preferred_element_type=F32)
        m_ref[:, cols] = (jax.nn.sigmoid(ga_ref[0, :, cols].astype(F32)) * a
                          + jax.nn.sigmoid(gb_ref[0, :, cols].astype(F32)) * bb).astype(BF16)
    o_ref[0] = x_ref[0] + jnp.dot(m_ref[...], wo_ref[...], preferred_element_type=F32)


def _merge(x, y_four, y_ret, g_a, g_b, wf, wr, wo):
    b, s, d = x.shape
    _, n_groups, _, tm, gd = y_four.shape

    def tile(w):
        return pl.BlockSpec((1, tm, w), lambda i, j: (i, j, 0))

    return pl.pallas_call(
        _merge_kernel,
        grid=(b, s // tm),
        in_specs=[tile(d),
                  pl.BlockSpec((1, n_groups, None, tm, gd), lambda i, j: (i, 0, j, 0, 0)),
                  pl.BlockSpec((1,) + y_ret.shape[1:2] + (tm,) + y_ret.shape[3:], lambda i, j: (i, 0, j, 0)),
                  tile(d), tile(d),
                  _resident(wf.shape), _resident(wr.shape), _resident(wo.shape)],
        out_specs=tile(d),
        out_shape=jax.ShapeDtypeStruct((b, s, d), F32),
        scratch_shapes=[pltpu.VMEM((tm, d), BF16)],
        compiler_params=pltpu.CompilerParams(
            dimension_semantics=("parallel", "parallel"), vmem_limit_bytes=VMEM_LIMIT_BYTES),
        name="merge",
    )(x, y_four, y_ret, g_a, g_b, wf, wr, wo)


def _ffn_kernel(xp_ref, x_ref, xn_ref, g2_ref, wu_ref, cw_ref, cb_ref, wd_ref, gf_ref, o_ref, act_ref):
    tm = x_ref.shape[1]
    halo = SUBLANES
    j = pl.program_id(1)
    last = pl.num_programs(1) - 1
    x = x_ref[0]
    xx = jnp.concatenate([x, xp_ref[0], xn_ref[0]], axis=0)
    u = _rms(xx, g2_ref[...]).astype(BF16)
    u_mid = u[0:tm]
    keep_top = (j > 0).astype(F32)
    keep_bot = (j < last).astype(F32)

    for ci in range(D_FF // FF_CHUNK):
        lo = ci * FF_CHUNK
        hg = jnp.dot(u, wu_ref[:, lo:lo + FF_CHUNK], preferred_element_type=F32)
        hv = jnp.dot(u_mid, wu_ref[:, D_FF + lo:D_FF + lo + FF_CHUNK], preferred_element_type=F32)
        mid = hg[0:tm]
        seq = jnp.concatenate([hg[tm:tm + halo] * keep_top, mid, hg[tm + halo:] * keep_bot], axis=0)
        cw = cw_ref[:, lo:lo + FF_CHUNK] * INV_SQRT2
        prev = pltpu.roll(seq, 1, 0)[halo:halo + tm]
        nxt = pltpu.roll(seq, tm + 2 * halo - 1, 0)[halo:halo + tm]
        hs = prev * cw[0:1] + mid * cw[1:2] + nxt * cw[2:3] + cb_ref[:, lo:lo + FF_CHUNK] * INV_SQRT2
        act_ref[:, lo:lo + FF_CHUNK] = ((hs * hv) * ((1.0 + lax.erf(hs)) * INV_SQRT2)).astype(BF16)
    y = jnp.dot(act_ref[...], wd_ref[...], preferred_element_type=F32)
    o_ref[0] = _rms(x + y, gf_ref[...])


def _ffn(x1, norm2_g, w_up, conv_w, conv_b, w_down, final_g):
    b, s, d = x1.shape
    tm = TOKEN_TILE
    hb = tm // SUBLANES
    n_hblocks = s // SUBLANES
    return pl.pallas_call(
        _ffn_kernel,
        grid=(b, s // tm),
        in_specs=[
            pl.BlockSpec((1, SUBLANES, d), lambda i, j: (i, jnp.maximum(j * hb - 1, 0), 0)),
            pl.BlockSpec((1, tm, d), lambda i, j: (i, j, 0)),
            pl.BlockSpec((1, SUBLANES, d), lambda i, j: (i, jnp.minimum((j + 1) * hb, n_hblocks - 1), 0)),
            _resident((1, d)),
            _resident(w_up.shape),
            _resident(conv_w.shape),
            _resident(conv_b.shape),
            _resident(w_down.shape),
            _resident((1, d)),
        ],
        out_specs=pl.BlockSpec((1, tm, d), lambda i, j: (i, j, 0)),
        out_shape=jax.ShapeDtypeStruct((b, s, d), F32),
        scratch_shapes=[pltpu.VMEM((tm, D_FF), BF16)],
        compiler_params=pltpu.CompilerParams(
            dimension_semantics=("parallel", "parallel"), vmem_limit_bytes=VMEM_LIMIT_BYTES),
        name="convglu_ffn",
    )(x1, x1, x1, norm2_g, w_up, conv_w, conv_b, w_down, final_g)


def _rotary_tables(s):
    inv = ROPE_THETA ** (-np.arange(0, RET_QK_DIM, 2, dtype=np.float64) / RET_QK_DIM)
    ang = np.arange(s, dtype=np.float64)[:, None] * inv[None, :]
    cos, sin = np.cos(ang), np.sin(ang)
    reps = LANES // RET_QK_DIM
    cos_tab = np.concatenate([cos, cos] * reps, axis=1).astype(np.float32)
    sin_tab = np.concatenate([-sin, sin] * reps, axis=1).astype(np.float32)
    assert cos_tab.shape == (s, LANES)
    return jnp.asarray(cos_tab), jnp.asarray(sin_tab)


def _encoder_layer(x, p):
    b, s, d = x.shape
    cos_tab, sin_tab = _rotary_tables(s)
    f, q, k, v, g_ret, g_a, g_b = _inproj(x, p["norm1_g"], p["w_in"], cos_tab, sin_tab)
    y_four = _fourier_mix(f)
    y_ret = _retention(q, k, v, g_ret, p["log_gamma"])
    return _merge(x, y_four, y_ret, g_a, g_b, p["w_four_proj"], p["w_ret_proj"], p["w_out"])


def kernel(x_prompt, x_sample, norm1_g, w_in, w_four_proj, w_ret_proj, w_out, ret_decay_logit,
           norm2_g, w_up, conv_w, conv_b, w_down, final_norm_g):
    depth = w_in.shape[0]
    assert depth == 1, "the final RMSNorm is fused into the (single) layer's FFN kernel"
    layers = []
    for l in range(depth):
        layers.append(dict(
            norm1_g=norm1_g[l][None, :],
            w_in=w_in[l].astype(BF16),
            w_four_proj=w_four_proj[l].astype(BF16),
            w_ret_proj=w_ret_proj[l].astype(BF16),
            w_out=w_out[l].astype(BF16),
            log_gamma=jax.nn.log_sigmoid(ret_decay_logit[l].astype(F32)),
            norm2_g=norm2_g[l][None, :],
            w_up=w_up[l].astype(BF16),
            conv_w=conv_w[l],
            conv_b=conv_b[l][None, :],
            w_down=w_down[l].astype(BF16),
        ))
    final_g = final_norm_g[None, :]

    def trunk(x):
        p = layers[0]
        x1 = _encoder_layer(x, p)
        return _ffn(x1, p["norm2_g"], p["w_up"], p["conv_w"], p["conv_b"], p["w_down"], final_g)

    return (trunk(x_prompt), trunk(x_sample))
```

```python
import functools

import numpy as np
import jax
import jax.numpy as jnp
from jax import lax
from jax.experimental import pallas as pl
from jax.experimental.pallas import tpu as pltpu

D_MODEL = 1024
N_FOURIER_GROUPS = 4
FOURIER_GROUP_DIM = 128
FOURIER_WIDTH = N_FOURIER_GROUPS * FOURIER_GROUP_DIM
N_RET_HEADS = 8
RET_QK_DIM = 64
RET_V_DIM = 128
RET_QK_WIDTH = N_RET_HEADS * RET_QK_DIM
RET_V_WIDTH = N_RET_HEADS * RET_V_DIM
CHUNK = 128
ROPE_THETA = 10000.0
D_FF = 2816
NORM_EPS = 1e-6
GN_EPS = 1e-5
IN_SPLITS = (FOURIER_WIDTH, RET_QK_WIDTH, RET_QK_WIDTH, RET_V_WIDTH, RET_V_WIDTH, D_MODEL, D_MODEL)
IN_OFFSETS = tuple(int(o) for o in np.cumsum((0,) + IN_SPLITS))
IN_WIDTH = IN_OFFSETS[-1]

LANES = 128
SUBLANES = 8
VMEM_LIMIT_BYTES = 56 * 1024 * 1024

F32 = jnp.float32
BF16 = jnp.bfloat16

TOKEN_TILE = 1024
FF_CHUNK = 256
FFT_N1 = 128
FFT_STAGE0_ROWS = 512
MERGE_COLS = 256
PITCH_PAD = SUBLANES


def _resident(shape):
    return pl.BlockSpec(shape, lambda *_: (0,) * len(shape), pipeline_mode=pl.Buffered(1))


def _rms(x, g):
    ms = jnp.mean(x * x, axis=-1, keepdims=True)
    return x * lax.rsqrt(ms + NORM_EPS) * g


def _inproj_kernel(x_ref, g_ref, w_ref, cos_ref, sin_ref,
                   f_ref, q_ref, k_ref, v_ref, gr_ref, ga_ref, gb_ref):
    u = _rms(x_ref[0], g_ref[...]).astype(BF16)

    def proj(i):
        return jnp.dot(u, w_ref[:, IN_OFFSETS[i]:IN_OFFSETS[i + 1]], preferred_element_type=F32)

    reps = RET_QK_WIDTH // LANES
    cos = jnp.concatenate([cos_ref[...]] * reps, axis=1)
    sin = jnp.concatenate([sin_ref[...]] * reps, axis=1)
    lane = lax.broadcasted_iota(jnp.int32, cos.shape, 1)
    first_half = (lane % RET_QK_DIM) < (RET_QK_DIM // 2)

    def rotary(t):
        partner = jnp.where(first_half,
                            pltpu.roll(t, RET_QK_WIDTH - RET_QK_DIM // 2, 1),
                            pltpu.roll(t, RET_QK_DIM // 2, 1))
        return t * cos + partner * sin

    def store_split(ref, val):
        parts, w = ref.shape[1], ref.shape[3]
        for p in range(parts):
            ref[0, p] = val[:, p * w:(p + 1) * w].astype(BF16)

    store_split(f_ref, proj(0))
    store_split(q_ref, rotary(proj(1)) * (RET_QK_DIM ** -0.5))
    store_split(k_ref, rotary(proj(2)))
    store_split(v_ref, proj(3))
    store_split(gr_ref, proj(4))
    ga_ref[0] = proj(5).astype(BF16)
    gb_ref[0] = proj(6).astype(BF16)


def _inproj(x, norm_g, w_in_bf16, cos_tab, sin_tab):
    b, s, d = x.shape
    tm = TOKEN_TILE
    parts = (N_FOURIER_GROUPS,) + (N_RET_HEADS // 2,) * 4
    out_shape = [jax.ShapeDtypeStruct((b, p, s, w // p), BF16) for p, w in zip(parts, IN_SPLITS[:5])]
    out_specs = [pl.BlockSpec((1, p, tm, w // p), lambda i, j: (i, 0, j, 0))
                 for p, w in zip(parts, IN_SPLITS[:5])]
    out_shape += [jax.ShapeDtypeStruct((b, s, w), BF16) for w in IN_SPLITS[5:]]
    out_specs += [pl.BlockSpec((1, tm, w), lambda i, j: (i, j, 0)) for w in IN_SPLITS[5:]]
    return pl.pallas_call(
        _inproj_kernel,
        grid=(b, s // tm),
        in_specs=[
            pl.BlockSpec((1, tm, d), lambda i, j: (i, j, 0)),
            _resident((1, d)),
            _resident((d, IN_WIDTH)),
            pl.BlockSpec((tm, LANES), lambda i, j: (j, 0)),
            pl.BlockSpec((tm, LANES), lambda i, j: (j, 0)),
        ],
        out_specs=out_specs,
        out_shape=out_shape,
        compiler_params=pltpu.CompilerParams(
            dimension_semantics=("parallel", "parallel"), vmem_limit_bytes=VMEM_LIMIT_BYTES),
        name="inproj",
    )(x, norm_g, w_in_bf16, cos_tab, sin_tab)


def _fft_kernel(x_ref, fc_ref, f1_ref, g2_ref, o_ref, z_ref, y_ref, *, n1, n2):
    p1 = n2 + PITCH_PAD
    p2 = 2 * n1 + PITCH_PAD
    gd = FOURIER_GROUP_DIM

    slabs = FFT_STAGE0_ROWS // n2

    def stage0(i, carry):
        r0 = pl.multiple_of(i * FFT_STAGE0_ROWS, FFT_STAGE0_ROWS)
        z = jnp.dot(x_ref[0, pl.ds(r0, FFT_STAGE0_ROWS), :], fc_ref[...],
                    preferred_element_type=F32)
        for s in range(slabs):
            row = pl.multiple_of((i * slabs + s) * p1, SUBLANES)
            z_ref[0, pl.ds(row, n2), :] = z[s * n2:(s + 1) * n2, :gd]
            z_ref[1, pl.ds(row, n2), :] = z[s * n2:(s + 1) * n2, gd:]
        return carry

    lax.fori_loop(0, (n1 * n2) // FFT_STAGE0_ROWS, stage0, 0, unroll=16)

    def stage1(jj, carry):
        cols = []
        for t in range(2):
            j = 2 * jj + t
            zr = z_ref[0, pl.ds(j, n1, stride=p1), :]
            zi = z_ref[1, pl.ds(j, n1, stride=p1), :]
            cols.append(jnp.concatenate([zr, zi], axis=0).astype(BF16))
        rhs = jnp.concatenate(cols, axis=1)
        y = jnp.dot(f1_ref[...], rhs, preferred_element_type=F32)
        for t in range(2):
            row = pl.multiple_of((2 * jj + t) * p2, SUBLANES)
            y_ref[pl.ds(row, 2 * n1), :] = y[:, t * gd:(t + 1) * gd]
        return carry

    lax.fori_loop(0, n2 // 2, stage1, 0, unroll=32)

    def stage2(k1, carry):
        yr = y_ref[pl.ds(k1, n2, stride=p2), :]
        yi = y_ref[pl.ds(n1 + k1, n2, stride=p2), :]
        rhs = jnp.concatenate([yr, yi], axis=0).astype(BF16)
        out = jnp.dot(g2_ref[k1], rhs, preferred_element_type=F32)
        row = pl.multiple_of(k1 * SUBLANES, SUBLANES)
        o_ref[0, 0, :, pl.ds(row, SUBLANES), :] = out.reshape(n2 // SUBLANES, SUBLANES, gd)
        return carry

    lax.fori_loop(0, n1, stage2, 0, unroll=128)


def _fft_tables(s, n1, n2):
    c = FOURIER_GROUP_DIM
    ic = np.arange(c)
    ang_c = (2.0 * np.pi / c) * ((ic[:, None] * ic[None, :]) % c)
    fc = np.concatenate([np.cos(ang_c), -np.sin(ang_c)], axis=1) * c ** -0.5
    i1 = np.arange(n1)
    ang_1 = (2.0 * np.pi / n1) * ((i1[:, None] * i1[None, :]) % n1)
    c1, s1 = np.cos(ang_1), np.sin(ang_1)
    f1 = np.concatenate([np.concatenate([c1, s1], axis=1),
                         np.concatenate([-s1, c1], axis=1)], axis=0) * n1 ** -0.5
    k1 = np.arange(n1)[:, None, None]
    k2 = np.arange(n2)[None, :, None]
    m2 = np.arange(n2)[None, None, :]
    ang_2 = (2.0 * np.pi / s) * ((m2 * (k1 + n1 * k2)) % s)
    g2 = np.concatenate([np.cos(ang_2), np.sin(ang_2)], axis=2) * n2 ** -0.5
    return tuple(jnp.asarray(t.astype(np.float32)).astype(BF16) for t in (fc, f1, g2))


def _fourier_mix(f):
    b, _, s, _ = f.shape
    n1 = FFT_N1
    n2 = s // n1
    fc, f1, g2 = _fft_tables(s, n1, n2)
    gd = FOURIER_GROUP_DIM
    p1 = n2 + PITCH_PAD
    p2 = 2 * n1 + PITCH_PAD
    return pl.pallas_call(
        functools.partial(_fft_kernel, n1=n1, n2=n2),
        grid=(b, N_FOURIER_GROUPS),
        in_specs=[
            pl.BlockSpec((1, None, s, gd), lambda i, g: (i, g, 0, 0)),
            _resident((gd, 2 * gd)),
            _resident((2 * n1, 2 * n1)),
            _resident((n1, n2, 2 * n2)),
        ],
        out_specs=pl.BlockSpec((1, 1, n2 // SUBLANES, n1 * SUBLANES, gd), lambda i, g: (i, g, 0, 0, 0)),
        out_shape=jax.ShapeDtypeStruct((b, N_FOURIER_GROUPS, n2 // SUBLANES, n1 * SUBLANES, gd), F32),
        scratch_shapes=[
            pltpu.VMEM((2, n1 * p1, gd), F32),
            pltpu.VMEM((n2 * p2, gd), F32),
        ],
        compiler_params=pltpu.CompilerParams(
            dimension_semantics=("parallel", "parallel"), vmem_limit_bytes=VMEM_LIMIT_BYTES),
        name="fourier_mix",
    )(f, fc, f1, g2)


def _ret_kernel(lg_ref, q_ref, k_ref, v_ref, g_ref, o_ref,
                sf_ref, sb_ref, p_ref, fr_ref, br_ref, *, n_chunks):
    c = CHUNK
    dk2 = 2 * RET_QK_DIM
    dv = RET_V_DIM
    pair = pl.program_id(1)
    lgf = [lg_ref[0, 2 * pair + h] for h in range(2)]
    lgb = [lg_ref[1, 2 * pair + h] for h in range(2)]

    row = lax.broadcasted_iota(jnp.int32, (c, c), 0)
    col = lax.broadcasted_iota(jnp.int32, (c, c), 1)
    diff = (row - col).astype(F32)
    decay2 = jnp.concatenate(
        [jnp.where(diff >= 0.0,
                   jnp.exp(lgf[h] * jnp.maximum(diff, 0.0)),
                   jnp.exp(lgb[h] * jnp.maximum(-diff, 0.0))) for h in range(2)], axis=1)

    pos = lax.broadcasted_iota(jnp.int32, (c, dk2), 0).astype(F32)
    head_a = lax.broadcasted_iota(jnp.int32, (c, dk2), 1) < RET_QK_DIM

    def per_head(fa, fb):
        return jnp.where(head_a, fa, fb).astype(BF16)

    q_dec_f = per_head(jnp.exp(lgf[0] * (pos + 1.0)), jnp.exp(lgf[1] * (pos + 1.0)))
    q_dec_b = per_head(jnp.exp(lgb[0] * (c - pos)), jnp.exp(lgb[1] * (c - pos)))
    k_dec_f = per_head(jnp.exp(lgf[0] * (c - 1.0 - pos)), jnp.exp(lgf[1] * (c - 1.0 - pos)))
    k_dec_b = per_head(jnp.exp(lgb[0] * pos), jnp.exp(lgb[1] * pos))

    srow = lax.broadcasted_iota(jnp.int32, (dk2, 2 * dv), 0) < RET_QK_DIM
    scol = lax.broadcasted_iota(jnp.int32, (dk2, 2 * dv), 1) < dv
    block_diag = srow == scol
    chunk_f = jnp.where(scol, jnp.exp(lgf[0] * c), jnp.exp(lgf[1] * c))
    chunk_b = jnp.where(scol, jnp.exp(lgb[0] * c), jnp.exp(lgb[1] * c))

    def kv_update(state, k_dec, chunk_dec, kc, vc):
        kd = kc * k_dec
        kv = lax.dot_general(kd, vc, (((0,), (0,)), ((), ())), preferred_element_type=F32)
        return chunk_dec * state + jnp.where(block_diag, kv, 0.0)

    fr_ref[...] = jnp.zeros_like(fr_ref)
    br_ref[...] = jnp.zeros_like(br_ref)

    def scan_step(t, carry):
        for run_ref, out_ref, i, k_dec, chunk_dec in (
                (fr_ref, sf_ref, t, k_dec_f, chunk_f),
                (br_ref, sb_ref, n_chunks - 1 - t, k_dec_b, chunk_b)):
            r0 = pl.multiple_of(i * c, c)
            state = run_ref[...]
            out_ref[i] = state.astype(BF16)
            run_ref[...] = kv_update(state, k_dec, chunk_dec,
                                     k_ref[0, pl.ds(r0, c), :], v_ref[0, pl.ds(r0, c), :])
        r0 = pl.multiple_of(t * c, c)
        kc = k_ref[0, pl.ds(r0, c), :]
        zero = jnp.zeros_like(kc)
        k_heads = jnp.concatenate([jnp.where(head_a, kc, zero), jnp.where(head_a, zero, kc)], axis=0)
        scores = lax.dot_general(q_ref[0, pl.ds(r0, c), :], k_heads, (((1,), (1,)), ((), ())),
                                 preferred_element_type=F32)
        p_ref[t] = (scores * decay2).astype(BF16)
        return carry

    lax.fori_loop(0, n_chunks, scan_step, 0, unroll=32)

    def out_step(i, carry):
        r0 = pl.multiple_of(i * c, c)
        qc = q_ref[0, pl.ds(r0, c), :]
        vc = v_ref[0, pl.ds(r0, c), :]
        q_inter = jnp.concatenate([qc * q_dec_f, qc * q_dec_b], axis=1)
        states = jnp.concatenate([sf_ref[i], sb_ref[i]], axis=0)
        inter = jnp.dot(q_inter, states, preferred_element_type=F32)
        outs = []
        for h in range(2):
            o = jnp.dot(p_ref[i, :, h * c:(h + 1) * c], vc[:, h * dv:(h + 1) * dv],
                        preferred_element_type=F32)
            o = o + inter[:, h * dv:(h + 1) * dv]
            mu = jnp.mean(o, axis=-1, keepdims=True)
            d = o - mu
            var = jnp.mean(d * d, axis=-1, keepdims=True)
            outs.append(d * lax.rsqrt(var + GN_EPS))
        gate = g_ref[0, pl.ds(r0, c), :]
        o_ref[0, pl.ds(r0, c), :] = jnp.concatenate(outs, axis=1).astype(BF16) * (gate * jax.nn.sigmoid(gate))
        return carry

    lax.fori_loop(0, n_chunks, out_step, 0, unroll=8)


def _retention(q, k, v, g, log_gamma):
    b, _, s, _ = q.shape
    n_chunks = s // CHUNK
    n_pairs = N_RET_HEADS // 2
    dk2 = 2 * RET_QK_DIM
    dv2 = 2 * RET_V_DIM
    return pl.pallas_call(
        functools.partial(_ret_kernel, n_chunks=n_chunks),
        grid=(b, n_pairs),
        in_specs=[
            pl.BlockSpec(memory_space=pltpu.SMEM),
            pl.BlockSpec((1, None, s, dk2), lambda i, p: (i, p, 0, 0)),
            pl.BlockSpec((1, None, s, dk2), lambda i, p: (i, p, 0, 0)),
            pl.BlockSpec((1, None, s, dv2), lambda i, p: (i, p, 0, 0)),
            pl.BlockSpec((1, None, s, dv2), lambda i, p: (i, p, 0, 0)),
        ],
        out_specs=pl.BlockSpec((1, None, s, dv2), lambda i, p: (i, p, 0, 0)),
        out_shape=jax.ShapeDtypeStruct((b, n_pairs, s, dv2), BF16),
        scratch_shapes=[
            pltpu.VMEM((n_chunks, dk2, dv2), BF16),
            pltpu.VMEM((n_chunks, dk2, dv2), BF16),
            pltpu.VMEM((n_chunks, CHUNK, 2 * CHUNK), BF16),
            pltpu.VMEM((dk2, dv2), F32),
            pltpu.VMEM((dk2, dv2), F32),
        ],
        compiler_params=pltpu.CompilerParams(
            dimension_semantics=("parallel", "parallel"), vmem_limit_bytes=VMEM_LIMIT_BYTES),
        name="retention",
    )(log_gamma, q, k, v, g)


def _merge_kernel(x_ref, yf_ref, ret_ref, ga_ref, gb_ref, wf_ref, wr_ref, wo_ref, o_ref, m_ref):
    n_groups = yf_ref.shape[1]
    k2_tile = SUBLANES
    n1 = yf_ref.shape[2] // k2_tile
    yf = jnp.concatenate(
        [jnp.concatenate([yf_ref[0, g, pl.ds(k2l, n1, stride=k2_tile), :] for g in range(n_groups)], axis=1)
         for k2l in range(k2_tile)], axis=0)
    yf = yf.astype(BF16)
    ret = jnp.concatenate([ret_ref[0, p] for p in range(ret_ref.shape[1])], axis=1)
    for lo in range(0, m_ref.shape[1], MERGE_COLS):
        cols = slice(lo, lo + MERGE_COLS)
        a = jnp.dot(yf, wf_ref[:, cols], preferred_element_type=F32)
        bb = jnp.dot(ret, wr_ref[:, cols], preferred_element_type=F32)
        m_ref[:, cols] = (jax.nn.sigmoid(ga_ref[0, :, cols].astype(F32)) * a
                          + jax.nn.sigmoid(gb_ref[0, :, cols].astype(F32)) * bb).astype(BF16)
    o_ref[0] = x_ref[0] + jnp.dot(m_ref[...], wo_ref[...], preferred_element_type=F32)


def _merge(x, y_four, y_ret, g_a, g_b, wf, wr, wo):
    b, s, d = x.shape
    _, n_groups, _, tm, gd = y_four.shape

    def tile(w):
        return pl.BlockSpec((1, tm, w), lambda i, j: (i, j, 0))

    return pl.pallas_call(
        _merge_kernel,
        grid=(b, s // tm),
        in_specs=[tile(d),
                  pl.BlockSpec((1, n_groups, None, tm, gd), lambda i, j: (i, 0, j, 0, 0)),
                  pl.BlockSpec((1,) + y_ret.shape[1:2] + (tm,) + y_ret.shape[3:], lambda i, j: (i, 0, j, 0)),
                  tile(d), tile(d),
                  _resident(wf.shape), _resident(wr.shape), _resident(wo.shape)],
        out_specs=tile(d),
        out_shape=jax.ShapeDtypeStruct((b, s, d), F32),
        scratch_shapes=[pltpu.VMEM((tm, d), BF16)],
        compiler_params=pltpu.CompilerParams(
            dimension_semantics=("parallel", "parallel"), vmem_limit_bytes=VMEM_LIMIT_BYTES),
        name="merge",
    )(x, y_four, y_ret, g_a, g_b, wf, wr, wo)


def _ffn_kernel(xp_ref, x_ref, xn_ref, g2_ref, wu_ref, cw_ref, cb_ref, wd_ref, gf_ref, o_ref, act_ref):
    tm = x_ref.shape[1]
    halo = SUBLANES
    j = pl.program_id(1)
    last = pl.num_programs(1) - 1
    x = x_ref[0]
    xx = jnp.concatenate([x, xp_ref[0], xn_ref[0]], axis=0)
    u = _rms(xx, g2_ref[...]).astype(BF16)
    u_mid = u[0:tm]
    keep_top = (j > 0).astype(F32)
    keep_bot = (j < last).astype(F32)

    for ci in range(D_FF // FF_CHUNK):
        lo = ci * FF_CHUNK
        hg = jnp.dot(u, wu_ref[:, lo:lo + FF_CHUNK], preferred_element_type=F32)
        hv = jnp.dot(u_mid, wu_ref[:, D_FF + lo:D_FF + lo + FF_CHUNK], preferred_element_type=F32)
        mid = hg[0:tm]
        seq = jnp.concatenate([hg[tm:tm + halo] * keep_top, mid, hg[tm + halo:] * keep_bot], axis=0)
        cw = cw_ref[:, lo:lo + FF_CHUNK]
        prev = pltpu.roll(seq, 1, 0)[halo:halo + tm]
        nxt = pltpu.roll(seq, tm + 2 * halo - 1, 0)[halo:halo + tm]
        hc = prev * cw[0:1] + mid * cw[1:2] + nxt * cw[2:3] + cb_ref[:, lo:lo + FF_CHUNK]
        act = 0.5 * hc * (1.0 + lax.erf(hc * (2.0 ** -0.5)))
        act_ref[:, lo:lo + FF_CHUNK] = (act * hv).astype(BF16)
    y = jnp.dot(act_ref[...], wd_ref[...], preferred_element_type=F32)
    o_ref[0] = _rms(x + y, gf_ref[...])


def _ffn(x1, norm2_g, w_up, conv_w, conv_b, w_down, final_g):
    b, s, d = x1.shape
    tm = TOKEN_TILE
    hb = tm // SUBLANES
    n_hblocks = s // SUBLANES
    return pl.pallas_call(
        _ffn_kernel,
        grid=(b, s // tm),
        in_specs=[
            pl.BlockSpec((1, SUBLANES, d), lambda i, j: (i, jnp.maximum(j * hb - 1, 0), 0)),
            pl.BlockSpec((1, tm, d), lambda i, j: (i, j, 0)),
            pl.BlockSpec((1, SUBLANES, d), lambda i, j: (i, jnp.minimum((j + 1) * hb, n_hblocks - 1), 0)),
            _resident((1, d)),
            _resident(w_up.shape),
            _resident(conv_w.shape),
            _resident(conv_b.shape),
            _resident(w_down.shape),
            _resident((1, d)),
        ],
        out_specs=pl.BlockSpec((1, tm, d), lambda i, j: (i, j, 0)),
        out_shape=jax.ShapeDtypeStruct((b, s, d), F32),
        scratch_shapes=[pltpu.VMEM((tm, D_FF), BF16)],
        compiler_params=pltpu.CompilerParams(
            dimension_semantics=("parallel", "parallel"), vmem_limit_bytes=VMEM_LIMIT_BYTES),
        name="convglu_ffn",
    )(x1, x1, x1, norm2_g, w_up, conv_w, conv_b, w_down, final_g)


def _rotary_tables(s):
    inv = ROPE_THETA ** (-np.arange(0, RET_QK_DIM, 2, dtype=np.float64) / RET_QK_DIM)
    ang = np.arange(s, dtype=np.float64)[:, None] * inv[None, :]
    cos, sin = np.cos(ang), np.sin(ang)
    reps = LANES // RET_QK_DIM
    cos_tab = np.concatenate([cos, cos] * reps, axis=1).astype(np.float32)
    sin_tab = np.concatenate([-sin, sin] * reps, axis=1).astype(np.float32)
    assert cos_tab.shape == (s, LANES)
    return jnp.asarray(cos_tab), jnp.asarray(sin_tab)


def _encoder_layer(x, p):
    b, s, d = x.shape
    cos_tab, sin_tab = _rotary_tables(s)
    f, q, k, v, g_ret, g_a, g_b = _inproj(x, p["norm1_g"], p["w_in"], cos_tab, sin_tab)
    y_four = _fourier_mix(f)
    y_ret = _retention(q, k, v, g_ret, p["log_gamma"])
    return _merge(x, y_four, y_ret, g_a, g_b, p["w_four_proj"], p["w_ret_proj"], p["w_out"])


def kernel(x_prompt, x_sample, norm1_g, w_in, w_four_proj, w_ret_proj, w_out, ret_decay_logit,
           norm2_g, w_up, conv_w, conv_b, w_down, final_norm_g):
    depth = w_in.shape[0]
    assert depth == 1, "the final RMSNorm is fused into the (single) layer's FFN kernel"
    layers = []
    for l in range(depth):
        layers.append(dict(
            norm1_g=norm1_g[l][None, :],
            w_in=w_in[l].astype(BF16),
            w_four_proj=w_four_proj[l].astype(BF16),
            w_ret_proj=w_ret_proj[l].astype(BF16),
            w_out=w_out[l].astype(BF16),
            log_gamma=jax.nn.log_sigmoid(ret_decay_logit[l].astype(F32)),
            norm2_g=norm2_g[l][None, :],
            w_up=w_up[l].astype(BF16),
            conv_w=conv_w[l],
            conv_b=conv_b[l][None, :],
            w_down=w_down[l].astype(BF16),
        ))
    final_g = final_norm_g[None, :]

    def trunk(x):
        p = layers[0]
        x1 = _encoder_layer(x, p)
        return _ffn(x1, p["norm2_g"], p["w_up"], p["conv_w"], p["conv_b"], p["w_down"], final_g)

    return (trunk(x_prompt), trunk(x_sample))
```

```python
import functools

import numpy as np
import jax
import jax.numpy as jnp
from jax import lax
from jax.experimental import pallas as pl
from jax.experimental.pallas import tpu as pltpu

D_MODEL = 1024
N_FOURIER_GROUPS = 4
FOURIER_GROUP_DIM = 128
FOURIER_WIDTH = N_FOURIER_GROUPS * FOURIER_GROUP_DIM
N_RET_HEADS = 8
RET_QK_DIM = 64
RET_V_DIM = 128
RET_QK_WIDTH = N_RET_HEADS * RET_QK_DIM
RET_V_WIDTH = N_RET_HEADS * RET_V_DIM
CHUNK = 128
ROPE_THETA = 10000.0
D_FF = 2816
NORM_EPS = 1e-6
GN_EPS = 1e-5
IN_SPLITS = (FOURIER_WIDTH, RET_QK_WIDTH, RET_QK_WIDTH, RET_V_WIDTH, RET_V_WIDTH, D_MODEL, D_MODEL)
IN_OFFSETS = tuple(int(o) for o in np.cumsum((0,) + IN_SPLITS))
IN_WIDTH = IN_OFFSETS[-1]

LANES = 128
SUBLANES = 8
VMEM_LIMIT_BYTES = 56 * 1024 * 1024

F32 = jnp.float32
BF16 = jnp.bfloat16

TOKEN_TILE = 1024
FF_CHUNK = 256
FFT_N1 = 128
FFT_STAGE0_ROWS = 512
MERGE_COLS = 256
PITCH_PAD = SUBLANES


def _resident(shape):
    return pl.BlockSpec(shape, lambda *_: (0,) * len(shape), pipeline_mode=pl.Buffered(1))


def _rms(x, g):
    ms = jnp.mean(x * x, axis=-1, keepdims=True)
    return x * lax.rsqrt(ms + NORM_EPS) * g


def _inproj_kernel(x_ref, g_ref, w_ref, cos_ref, sin_ref,
                   f_ref, q_ref, k_ref, v_ref, gr_ref, ga_ref, gb_ref):
    u = _rms(x_ref[0], g_ref[...]).astype(BF16)

    def proj(i):
        return jnp.dot(u, w_ref[:, IN_OFFSETS[i]:IN_OFFSETS[i + 1]], preferred_element_type=F32)

    reps = RET_QK_WIDTH // LANES
    cos = jnp.concatenate([cos_ref[...]] * reps, axis=1)
    sin = jnp.concatenate([sin_ref[...]] * reps, axis=1)
    lane = lax.broadcasted_iota(jnp.int32, cos.shape, 1)
    first_half = (lane % RET_QK_DIM) < (RET_QK_DIM // 2)

    def rotary(t):
        partner = jnp.where(first_half,
                            pltpu.roll(t, RET_QK_WIDTH - RET_QK_DIM // 2, 1),
                            pltpu.roll(t, RET_QK_DIM // 2, 1))
        return t * cos + partner * sin

    def store_split(ref, val):
        parts, w = ref.shape[1], ref.shape[3]
        for p in range(parts):
            ref[0, p] = val[:, p * w:(p + 1) * w].astype(BF16)

    store_split(f_ref, proj(0))
    store_split(q_ref, rotary(proj(1)) * (RET_QK_DIM ** -0.5))
    store_split(k_ref, rotary(proj(2)))
    store_split(v_ref, proj(3))
    store_split(gr_ref, proj(4))
    ga_ref[0] = proj(5).astype(BF16)
    gb_ref[0] = proj(6).astype(BF16)


def _inproj(x, norm_g, w_in_bf16, cos_tab, sin_tab):
    b, s, d = x.shape
    tm = TOKEN_TILE
    parts = (N_FOURIER_GROUPS,) + (N_RET_HEADS // 2,) * 4
    out_shape = [jax.ShapeDtypeStruct((b, p, s, w // p), BF16) for p, w in zip(parts, IN_SPLITS[:5])]
    out_specs = [pl.BlockSpec((1, p, tm, w // p), lambda i, j: (i, 0, j, 0))
                 for p, w in zip(parts, IN_SPLITS[:5])]
    out_shape += [jax.ShapeDtypeStruct((b, s, w), BF16) for w in IN_SPLITS[5:]]
    out_specs += [pl.BlockSpec((1, tm, w), lambda i, j: (i, j, 0)) for w in IN_SPLITS[5:]]
    return pl.pallas_call(
        _inproj_kernel,
        grid=(b, s // tm),
        in_specs=[
            pl.BlockSpec((1, tm, d), lambda i, j: (i, j, 0)),
            _resident((1, d)),
            _resident((d, IN_WIDTH)),
            pl.BlockSpec((tm, LANES), lambda i, j: (j, 0)),
            pl.BlockSpec((tm, LANES), lambda i, j: (j, 0)),
        ],
        out_specs=out_specs,
        out_shape=out_shape,
        compiler_params=pltpu.CompilerParams(
            dimension_semantics=("parallel", "parallel"), vmem_limit_bytes=VMEM_LIMIT_BYTES),
        name="inproj",
    )(x, norm_g, w_in_bf16, cos_tab, sin_tab)


def _fft_kernel(x_ref, fc_ref, f1_ref, g2_ref, o_ref, z_ref, y_ref, *, n1, n2):
    p1 = n2 + PITCH_PAD
    p2 = 2 * n1 + PITCH_PAD
    gd = FOURIER_GROUP_DIM

    slabs = FFT_STAGE0_ROWS // n2

    def stage0(i, carry):
        r0 = pl.multiple_of(i * FFT_STAGE0_ROWS, FFT_STAGE0_ROWS)
        z = jnp.dot(x_ref[0, pl.ds(r0, FFT_STAGE0_ROWS), :], fc_ref[...],
                    preferred_element_type=F32)
        for s in range(slabs):
            row = pl.multiple_of((i * slabs + s) * p1, SUBLANES)
            z_ref[0, pl.ds(row, n2), :] = z[s * n2:(s + 1) * n2, :gd]
            z_ref[1, pl.ds(row, n2), :] = z[s * n2:(s + 1) * n2, gd:]
        return carry

    lax.fori_loop(0, (n1 * n2) // FFT_STAGE0_ROWS, stage0, 0, unroll=16)

    def stage1(jj, carry):
        cols = []
        for t in range(2):
            j = 2 * jj + t
            zr = z_ref[0, pl.ds(j, n1, stride=p1), :]
            zi = z_ref[1, pl.ds(j, n1, stride=p1), :]
            cols.append(jnp.concatenate([zr, zi], axis=0).astype(BF16))
        rhs = jnp.concatenate(cols, axis=1)
        y = jnp.dot(f1_ref[...], rhs, preferred_element_type=F32)
        for t in range(2):
            row = pl.multiple_of((2 * jj + t) * p2, SUBLANES)
            y_ref[pl.ds(row, 2 * n1), :] = y[:, t * gd:(t + 1) * gd]
        return carry

    lax.fori_loop(0, n2 // 2, stage1, 0, unroll=32)

    def stage2(k1, carry):
        yr = y_ref[pl.ds(k1, n2, stride=p2), :]
        yi = y_ref[pl.ds(n1 + k1, n2, stride=p2), :]
        rhs = jnp.concatenate([yr, yi], axis=0).astype(BF16)
        out = jnp.dot(g2_ref[k1], rhs, preferred_element_type=F32)
        row = pl.multiple_of(k1 * SUBLANES, SUBLANES)
        o_ref[0, 0, :, pl.ds(row, SUBLANES), :] = out.reshape(n2 // SUBLANES, SUBLANES, gd)
        return carry

    lax.fori_loop(0, n1, stage2, 0, unroll=128)


def _fft_tables(s, n1, n2):
    c = FOURIER_GROUP_DIM
    ic = np.arange(c)
    ang_c = (2.0 * np.pi / c) * ((ic[:, None] * ic[None, :]) % c)
    fc = np.concatenate([np.cos(ang_c), -np.sin(ang_c)], axis=1) * c ** -0.5
    i1 = np.arange(n1)
    ang_1 = (2.0 * np.pi / n1) * ((i1[:, None] * i1[None, :]) % n1)
    c1, s1 = np.cos(ang_1), np.sin(ang_1)
    f1 = np.concatenate([np.concatenate([c1, s1], axis=1),
                         np.concatenate([-s1, c1], axis=1)], axis=0) * n1 ** -0.5
    k1 = np.arange(n1)[:, None, None]
    k2 = np.arange(n2)[None, :, None]
    m2 = np.arange(n2)[None, None, :]
    ang_2 = (2.0 * np.pi / s) * ((m2 * (k1 + n1 * k2)) % s)
    g2 = np.concatenate([np.cos(ang_2), np.sin(ang_2)], axis=2) * n2 ** -0.5
    return tuple(jnp.asarray(t.astype(np.float32)).astype(BF16) for t in (fc, f1, g2))


def _fourier_mix(f):
    b, _, s, _ = f.shape
    n1 = FFT_N1
    n2 = s // n1
    fc, f1, g2 = _fft_tables(s, n1, n2)
    gd = FOURIER_GROUP_DIM
    p1 = n2 + PITCH_PAD
    p2 = 2 * n1 + PITCH_PAD
    return pl.pallas_call(
        functools.partial(_fft_kernel, n1=n1, n2=n2),
        grid=(b, N_FOURIER_GROUPS),
        in_specs=[
            pl.BlockSpec((1, None, s, gd), lambda i, g: (i, g, 0, 0)),
            _resident((gd, 2 * gd)),
            _resident((2 * n1, 2 * n1)),
            _resident((n1, n2, 2 * n2)),
        ],
        out_specs=pl.BlockSpec((1, 1, n2 // SUBLANES, n1 * SUBLANES, gd), lambda i, g: (i, g, 0, 0, 0)),
        out_shape=jax.ShapeDtypeStruct((b, N_FOURIER_GROUPS, n2 // SUBLANES, n1 * SUBLANES, gd), F32),
        scratch_shapes=[
            pltpu.VMEM((2, n1 * p1, gd), F32),
            pltpu.VMEM((n2 * p2, gd), F32),
        ],
        compiler_params=pltpu.CompilerParams(
            dimension_semantics=("parallel", "parallel"), vmem_limit_bytes=VMEM_LIMIT_BYTES),
        name="fourier_mix",
    )(f, fc, f1, g2)


def _ret_kernel(lg_ref, q_ref, k_ref, v_ref, g_ref, o_ref,
                sf_ref, sb_ref, p_ref, fr_ref, br_ref, *, n_chunks):
    c = CHUNK
    dk2 = 2 * RET_QK_DIM
    dv = RET_V_DIM
    pair = pl.program_id(1)
    lgf = [lg_ref[0, 2 * pair + h] for h in range(2)]
    lgb = [lg_ref[1, 2 * pair + h] for h in range(2)]

    row = lax.broadcasted_iota(jnp.int32, (c, c), 0)
    col = lax.broadcasted_iota(jnp.int32, (c, c), 1)
    diff = (row - col).astype(F32)
    decay2 = jnp.concatenate(
        [jnp.where(diff >= 0.0,
                   jnp.exp(lgf[h] * jnp.maximum(diff, 0.0)),
                   jnp.exp(lgb[h] * jnp.maximum(-diff, 0.0))) for h in range(2)], axis=1)

    pos = lax.broadcasted_iota(jnp.int32, (c, dk2), 0).astype(F32)
    head_a = lax.broadcasted_iota(jnp.int32, (c, dk2), 1) < RET_QK_DIM

    def per_head(fa, fb):
        return jnp.where(head_a, fa, fb).astype(BF16)

    q_dec_f = per_head(jnp.exp(lgf[0] * (pos + 1.0)), jnp.exp(lgf[1] * (pos + 1.0)))
    q_dec_b = per_head(jnp.exp(lgb[0] * (c - pos)), jnp.exp(lgb[1] * (c - pos)))
    k_dec_f = per_head(jnp.exp(lgf[0] * (c - 1.0 - pos)), jnp.exp(lgf[1] * (c - 1.0 - pos)))
    k_dec_b = per_head(jnp.exp(lgb[0] * pos), jnp.exp(lgb[1] * pos))

    srow = lax.broadcasted_iota(jnp.int32, (dk2, 2 * dv), 0) < RET_QK_DIM
    scol = lax.broadcasted_iota(jnp.int32, (dk2, 2 * dv), 1) < dv
    block_diag = srow == scol
    chunk_f = jnp.where(scol, jnp.exp(lgf[0] * c), jnp.exp(lgf[1] * c))
    chunk_b = jnp.where(scol, jnp.exp(lgb[0] * c), jnp.exp(lgb[1] * c))

    def kv_update(state, k_dec, chunk_dec, kc, vc):
        kd = kc * k_dec
        kv = lax.dot_general(kd, vc, (((0,), (0,)), ((), ())), preferred_element_type=F32)
        return chunk_dec * state + jnp.where(block_diag, kv, 0.0)

    fr_ref[...] = jnp.zeros_like(fr_ref)
    br_ref[...] = jnp.zeros_like(br_ref)

    def scan_step(t, carry):
        for run_ref, out_ref, i, k_dec, chunk_dec in (
                (fr_ref, sf_ref, t, k_dec_f, chunk_f),
                (br_ref, sb_ref, n_chunks - 1 - t, k_dec_b, chunk_b)):
            r0 = pl.multiple_of(i * c, c)
            state = run_ref[...]
            out_ref[i] = state.astype(BF16)
            run_ref[...] = kv_update(state, k_dec, chunk_dec,
                                     k_ref[0, pl.ds(r0, c), :], v_ref[0, pl.ds(r0, c), :])
        r0 = pl.multiple_of(t * c, c)
        kc = k_ref[0, pl.ds(r0, c), :]
        zero = jnp.zeros_like(kc)
        k_heads = jnp.concatenate([jnp.where(head_a, kc, zero), jnp.where(head_a, zero, kc)], axis=0)
        scores = lax.dot_general(q_ref[0, pl.ds(r0, c), :], k_heads, (((1,), (1,)), ((), ())),
                                 preferred_element_type=F32)
        p_ref[t] = (scores * decay2).astype(BF16)
        return carry

    lax.fori_loop(0, n_chunks, scan_step, 0, unroll=64)

    def out_step(i, carry):
        r0 = pl.multiple_of(i * c, c)
        qc = q_ref[0, pl.ds(r0, c), :]
        vc = v_ref[0, pl.ds(r0, c), :]
        q_inter = jnp.concatenate([qc * q_dec_f, qc * q_dec_b], axis=1)
        states = jnp.concatenate([sf_ref[i], sb_ref[i]], axis=0)
        inter = jnp.dot(q_inter, states, preferred_element_type=F32)
        outs = []
        for h in range(2):
            o = jnp.dot(p_ref[i, :, h * c:(h + 1) * c], vc[:, h * dv:(h + 1) * dv],
                        preferred_element_type=F32)
            o = o + inter[:, h * dv:(h + 1) * dv]
            mu = jnp.mean(o, axis=-1, keepdims=True)
            d = o - mu
            var = jnp.mean(d * d, axis=-1, keepdims=True)
            outs.append(d * lax.rsqrt(var + GN_EPS))
        gate = g_ref[0, pl.ds(r0, c), :]
        o_ref[0, pl.ds(r0, c), :] = jnp.concatenate(outs, axis=1).astype(BF16) * (gate * jax.nn.sigmoid(gate))
        return carry

    lax.fori_loop(0, n_chunks, out_step, 0, unroll=8)


def _retention(q, k, v, g, log_gamma):
    b, _, s, _ = q.shape
    n_chunks = s // CHUNK
    n_pairs = N_RET_HEADS // 2
    dk2 = 2 * RET_QK_DIM
    dv2 = 2 * RET_V_DIM
    return pl.pallas_call(
        functools.partial(_ret_kernel, n_chunks=n_chunks),
        grid=(b, n_pairs),
        in_specs=[
            pl.BlockSpec(memory_space=pltpu.SMEM),
            pl.BlockSpec((1, None, s, dk2), lambda i, p: (i, p, 0, 0)),
            pl.BlockSpec((1, None, s, dk2), lambda i, p: (i, p, 0, 0)),
            pl.BlockSpec((1, None, s, dv2), lambda i, p: (i, p, 0, 0)),
            pl.BlockSpec((1, None, s, dv2), lambda i, p: (i, p, 0, 0)),
        ],
        out_specs=pl.BlockSpec((1, None, s, dv2), lambda i, p: (i, p, 0, 0)),
        out_shape=jax.ShapeDtypeStruct((b, n_pairs, s, dv2), BF16),
        scratch_shapes=[
            pltpu.VMEM((n_chunks, dk2, dv2), BF16),
            pltpu.VMEM((n_chunks, dk2, dv2), BF16),
            pltpu.VMEM((n_chunks, CHUNK, 2 * CHUNK), BF16),
            pltpu.VMEM((dk2, dv2), F32),
            pltpu.VMEM((dk2, dv2), F32),
        ],
        compiler_params=pltpu.CompilerParams(
            dimension_semantics=("parallel", "parallel"), vmem_limit_bytes=VMEM_LIMIT_BYTES),
        name="retention",
    )(log_gamma, q, k, v, g)


def _merge_kernel(x_ref, yf_ref, ret_ref, ga_ref, gb_ref, wf_ref, wr_ref, wo_ref, o_ref, m_ref):
    n_groups = yf_ref.shape[1]
    k2_tile = SUBLANES
    n1 = yf_ref.shape[2] // k2_tile
    yf = jnp.concatenate(
        [jnp.concatenate([yf_ref[0, g, pl.ds(k2l, n1, stride=k2_tile), :] for g in range(n_groups)], axis=1)
         for k2l in range(k2_tile)], axis=0)
    yf = yf.astype(BF16)
    ret = jnp.concatenate([ret_ref[0, p] for p in range(ret_ref.shape[1])], axis=1)
    for lo in range(0, m_ref.shape[1], MERGE_COLS):
        cols = slice(lo, lo + MERGE_COLS)
        a = jnp.dot(yf, wf_ref[:, cols], preferred_element_type=F32)
        bb = jnp.dot(ret, wr_ref[:, cols], preferred_element_type=F32)
        m_ref[:, cols] = (jax.nn.sigmoid(ga_ref[0, :, cols].astype(F32)) * a
                          + jax.nn.sigmoid(gb_ref[0, :, cols].astype(F32)) * bb).astype(BF16)
    o_ref[0] = x_ref[0] + jnp.dot(m_ref[...], wo_ref[...], preferred_element_type=F32)


def _merge(x, y_four, y_ret, g_a, g_b, wf, wr, wo):
    b, s, d = x.shape
    _, n_groups, _, tm, gd = y_four.shape

    def tile(w):
        return pl.BlockSpec((1, tm, w), lambda i, j: (i, j, 0))

    return pl.pallas_call(
        _merge_kernel,
        grid=(b, s // tm),
        in_specs=[tile(d),
                  pl.BlockSpec((1, n_groups, None, tm, gd), lambda i, j: (i, 0, j, 0, 0)),
                  pl.BlockSpec((1,) + y_ret.shape[1:2] + (tm,) + y_ret.shape[3:], lambda i, j: (i, 0, j, 0)),
                  tile(d), tile(d),
                  _resident(wf.shape), _resident(wr.shape), _resident(wo.shape)],
        out_specs=tile(d),
        out_shape=jax.ShapeDtypeStruct((b, s, d), F32),
        scratch_shapes=[pltpu.VMEM((tm, d), BF16)],
        compiler_params=pltpu.CompilerParams(
            dimension_semantics=("parallel", "parallel"), vmem_limit_bytes=VMEM_LIMIT_BYTES),
        name="merge",
    )(x, y_four, y_ret, g_a, g_b, wf, wr, wo)


def _ffn_kernel(xp_ref, x_ref, xn_ref, g2_ref, wu_ref, cw_ref, cb_ref, wd_ref, gf_ref, o_ref, act_ref):
    tm = x_ref.shape[1]
    halo = SUBLANES
    j = pl.program_id(1)
    last = pl.num_programs(1) - 1
    x = x_ref[0]
    xx = jnp.concatenate([x, xp_ref[0], xn_ref[0]], axis=0)
    u = _rms(xx, g2_ref[...]).astype(BF16)
    u_mid = u[0:tm]
    keep_top = (j > 0).astype(F32)
    keep_bot = (j < last).astype(F32)

    for ci in range(D_FF // FF_CHUNK):
        lo = ci * FF_CHUNK
        hg = jnp.dot(u, wu_ref[:, lo:lo + FF_CHUNK], preferred_element_type=F32)
        hv = jnp.dot(u_mid, wu_ref[:, D_FF + lo:D_FF + lo + FF_CHUNK], preferred_element_type=F32)
        mid = hg[0:tm]
        seq = jnp.concatenate([hg[tm:tm + halo] * keep_top, mid, hg[tm + halo:] * keep_bot], axis=0)
        cw = cw_ref[:, lo:lo + FF_CHUNK]
        prev = pltpu.roll(seq, 1, 0)[halo:halo + tm]
        nxt = pltpu.roll(seq, tm + 2 * halo - 1, 0)[halo:halo + tm]
        hc = prev * cw[0:1] + mid * cw[1:2] + nxt * cw[2:3] + cb_ref[:, lo:lo + FF_CHUNK]
        act = 0.5 * hc * (1.0 + lax.erf(hc * (2.0 ** -0.5)))
        act_ref[:, lo:lo + FF_CHUNK] = (act * hv).astype(BF16)
    y = jnp.dot(act_ref[...], wd_ref[...], preferred_element_type=F32)
    o_ref[0] = _rms(x + y, gf_ref[...])


def _ffn(x1, norm2_g, w_up, conv_w, conv_b, w_down, final_g):
    b, s, d = x1.shape
    tm = TOKEN_TILE
    hb = tm // SUBLANES
    n_hblocks = s // SUBLANES
    return pl.pallas_call(
        _ffn_kernel,
        grid=(b, s // tm),
        in_specs=[
            pl.BlockSpec((1, SUBLANES, d), lambda i, j: (i, jnp.maximum(j * hb - 1, 0), 0)),
            pl.BlockSpec((1, tm, d), lambda i, j: (i, j, 0)),
            pl.BlockSpec((1, SUBLANES, d), lambda i, j: (i, jnp.minimum((j + 1) * hb, n_hblocks - 1), 0)),
            _resident((1, d)),
            _resident(w_up.shape),
            _resident(conv_w.shape),
            _resident(conv_b.shape),
            _resident(w_down.shape),
            _resident((1, d)),
        ],
        out_specs=pl.BlockSpec((1, tm, d), lambda i, j: (i, j, 0)),
        out_shape=jax.ShapeDtypeStruct((b, s, d), F32),
        scratch_shapes=[pltpu.VMEM((tm, D_FF), BF16)],
        compiler_params=pltpu.CompilerParams(
            dimension_semantics=("parallel", "parallel"), vmem_limit_bytes=VMEM_LIMIT_BYTES),
        name="convglu_ffn",
    )(x1, x1, x1, norm2_g, w_up, conv_w, conv_b, w_down, final_g)


def _rotary_tables(s):
    inv = ROPE_THETA ** (-np.arange(0, RET_QK_DIM, 2, dtype=np.float64) / RET_QK_DIM)
    ang = np.arange(s, dtype=np.float64)[:, None] * inv[None, :]
    cos, sin = np.cos(ang), np.sin(ang)
    reps = LANES // RET_QK_DIM
    cos_tab = np.concatenate([cos, cos] * reps, axis=1).astype(np.float32)
    sin_tab = np.concatenate([-sin, sin] * reps, axis=1).astype(np.float32)
    assert cos_tab.shape == (s, LANES)
    return jnp.asarray(cos_tab), jnp.asarray(sin_tab)


def _encoder_layer(x, p):
    b, s, d = x.shape
    cos_tab, sin_tab = _rotary_tables(s)
    f, q, k, v, g_ret, g_a, g_b = _inproj(x, p["norm1_g"], p["w_in"], cos_tab, sin_tab)
    y_four = _fourier_mix(f)
    y_ret = _retention(q, k, v, g_ret, p["log_gamma"])
    return _merge(x, y_four, y_ret, g_a, g_b, p["w_four_proj"], p["w_ret_proj"], p["w_out"])


def kernel(x_prompt, x_sample, norm1_g, w_in, w_four_proj, w_ret_proj, w_out, ret_decay_logit,
           norm2_g, w_up, conv_w, conv_b, w_down, final_norm_g):
    depth = w_in.shape[0]
    assert depth == 1, "the final RMSNorm is fused into the (single) layer's FFN kernel"
    layers = []
    for l in range(depth):
        layers.append(dict(
            norm1_g=norm1_g[l][None, :],
            w_in=w_in[l].astype(BF16),
            w_four_proj=w_four_proj[l].astype(BF16),
            w_ret_proj=w_ret_proj[l].astype(BF16),
            w_out=w_out[l].astype(BF16),
            log_gamma=jax.nn.log_sigmoid(ret_decay_logit[l].astype(F32)),
            norm2_g=norm2_g[l][None, :],
            w_up=w_up[l].astype(BF16),
            conv_w=conv_w[l],
            conv_b=conv_b[l][None, :],
            w_down=w_down[l].astype(BF16),
        ))
    final_g = final_norm_g[None, :]

    def trunk(x):
        p = layers[0]
        x1 = _encoder_layer(x, p)
        return _ffn(x1, p["norm2_g"], p["w_up"], p["conv_w"], p["conv_b"], p["w_down"], final_g)

    return (trunk(x_prompt), trunk(x_sample))
```

```python
import functools

import numpy as np
import jax
import jax.numpy as jnp
from jax import lax
from jax.experimental import pallas as pl
from jax.experimental.pallas import tpu as pltpu

D_MODEL = 1024
N_FOURIER_GROUPS = 4
FOURIER_GROUP_DIM = 128
FOURIER_WIDTH = N_FOURIER_GROUPS * FOURIER_GROUP_DIM
N_RET_HEADS = 8
RET_QK_DIM = 64
RET_V_DIM = 128
RET_QK_WIDTH = N_RET_HEADS * RET_QK_DIM
RET_V_WIDTH = N_RET_HEADS * RET_V_DIM
CHUNK = 128
ROPE_THETA = 10000.0
D_FF = 2816
NORM_EPS = 1e-6
GN_EPS = 1e-5
IN_SPLITS = (FOURIER_WIDTH, RET_QK_WIDTH, RET_QK_WIDTH, RET_V_WIDTH, RET_V_WIDTH, D_MODEL, D_MODEL)
IN_OFFSETS = tuple(int(o) for o in np.cumsum((0,) + IN_SPLITS))
IN_WIDTH = IN_OFFSETS[-1]

LANES = 128
SUBLANES = 8
VMEM_LIMIT_BYTES = 56 * 1024 * 1024

F32 = jnp.float32
BF16 = jnp.bfloat16

TOKEN_TILE = 1024
FF_CHUNK = 256
FFN_OUT_ROWS = 256
FFT_N1 = 128
FFT_STAGE0_ROWS = 512
MERGE_COLS = 256
PITCH_PAD = SUBLANES


def _resident(shape):
    return pl.BlockSpec(shape, lambda *_: (0,) * len(shape), pipeline_mode=pl.Buffered(1))


def _rms(x, g):
    ms = jnp.mean(x * x, axis=-1, keepdims=True)
    return x * lax.rsqrt(ms + NORM_EPS) * g


def _inproj_kernel(x_ref, g_ref, w_ref, cos_ref, sin_ref,
                   f_ref, q_ref, k_ref, v_ref, gr_ref, ga_ref, gb_ref):
    u = _rms(x_ref[0], g_ref[...]).astype(BF16)

    def proj(i):
        return jnp.dot(u, w_ref[:, IN_OFFSETS[i]:IN_OFFSETS[i + 1]], preferred_element_type=F32)

    reps = RET_QK_WIDTH // LANES
    cos = jnp.concatenate([cos_ref[...]] * reps, axis=1)
    sin = jnp.concatenate([sin_ref[...]] * reps, axis=1)
    lane = lax.broadcasted_iota(jnp.int32, cos.shape, 1)
    first_half = (lane % RET_QK_DIM) < (RET_QK_DIM // 2)

    def rotary(t):
        partner = jnp.where(first_half,
                            pltpu.roll(t, RET_QK_WIDTH - RET_QK_DIM // 2, 1),
                            pltpu.roll(t, RET_QK_DIM // 2, 1))
        return t * cos + partner * sin

    def store_split(ref, val):
        parts, w = ref.shape[1], ref.shape[3]
        for p in range(parts):
            ref[0, p] = val[:, p * w:(p + 1) * w].astype(BF16)

    store_split(f_ref, proj(0))
    store_split(q_ref, rotary(proj(1)) * (RET_QK_DIM ** -0.5))
    store_split(k_ref, rotary(proj(2)))
    store_split(v_ref, proj(3))
    store_split(gr_ref, proj(4))
    ga_ref[0] = proj(5).astype(BF16)
    gb_ref[0] = proj(6).astype(BF16)


def _inproj(x, norm_g, w_in_bf16, cos_tab, sin_tab):
    b, s, d = x.shape
    tm = TOKEN_TILE
    parts = (N_FOURIER_GROUPS,) + (N_RET_HEADS // 2,) * 4
    out_shape = [jax.ShapeDtypeStruct((b, p, s, w // p), BF16) for p, w in zip(parts, IN_SPLITS[:5])]
    out_specs = [pl.BlockSpec((1, p, tm, w // p), lambda i, j: (i, 0, j, 0))
                 for p, w in zip(parts, IN_SPLITS[:5])]
    out_shape += [jax.ShapeDtypeStruct((b, s, w), BF16) for w in IN_SPLITS[5:]]
    out_specs += [pl.BlockSpec((1, tm, w), lambda i, j: (i, j, 0)) for w in IN_SPLITS[5:]]
    return pl.pallas_call(
        _inproj_kernel,
        grid=(b, s // tm),
        in_specs=[
            pl.BlockSpec((1, tm, d), lambda i, j: (i, j, 0)),
            _resident((1, d)),
            _resident((d, IN_WIDTH)),
            pl.BlockSpec((tm, LANES), lambda i, j: (j, 0)),
            pl.BlockSpec((tm, LANES), lambda i, j: (j, 0)),
        ],
        out_specs=out_specs,
        out_shape=out_shape,
        compiler_params=pltpu.CompilerParams(
            dimension_semantics=("parallel", "parallel"), vmem_limit_bytes=VMEM_LIMIT_BYTES),
        name="inproj",
    )(x, norm_g, w_in_bf16, cos_tab, sin_tab)


def _fft_kernel(x_ref, fc_ref, f1_ref, g2_ref, o_ref, z_ref, y_ref, *, n1, n2):
    p1 = n2 + PITCH_PAD
    p2 = 2 * n1 + PITCH_PAD
    gd = FOURIER_GROUP_DIM

    slabs = FFT_STAGE0_ROWS // n2

    def stage0(i, carry):
        r0 = pl.multiple_of(i * FFT_STAGE0_ROWS, FFT_STAGE0_ROWS)
        z = jnp.dot(x_ref[0, pl.ds(r0, FFT_STAGE0_ROWS), :], fc_ref[...],
                    preferred_element_type=F32)
        for s in range(slabs):
            row = pl.multiple_of((i * slabs + s) * p1, SUBLANES)
            z_ref[0, pl.ds(row, n2), :] = z[s * n2:(s + 1) * n2, :gd]
            z_ref[1, pl.ds(row, n2), :] = z[s * n2:(s + 1) * n2, gd:]
        return carry

    lax.fori_loop(0, (n1 * n2) // FFT_STAGE0_ROWS, stage0, 0, unroll=16)

    def stage1(jj, carry):
        cols = []
        for t in range(2):
            j = 2 * jj + t
            zr = z_ref[0, pl.ds(j, n1, stride=p1), :]
            zi = z_ref[1, pl.ds(j, n1, stride=p1), :]
            cols.append(jnp.concatenate([zr, zi], axis=0).astype(BF16))
        rhs = jnp.concatenate(cols, axis=1)
        y = jnp.dot(f1_ref[...], rhs, preferred_element_type=F32)
        for t in range(2):
            row = pl.multiple_of((2 * jj + t) * p2, SUBLANES)
            y_ref[pl.ds(row, 2 * n1), :] = y[:, t * gd:(t + 1) * gd]
        return carry

    lax.fori_loop(0, n2 // 2, stage1, 0, unroll=32)

    def stage2(k1, carry):
        yr = y_ref[pl.ds(k1, n2, stride=p2), :]
        yi = y_ref[pl.ds(n1 + k1, n2, stride=p2), :]
        rhs = jnp.concatenate([yr, yi], axis=0).astype(BF16)
        out = jnp.dot(g2_ref[k1], rhs, preferred_element_type=F32)
        row = pl.multiple_of(k1 * SUBLANES, SUBLANES)
        o_ref[0, 0, :, pl.ds(row, SUBLANES), :] = out.reshape(n2 // SUBLANES, SUBLANES, gd)
        return carry

    lax.fori_loop(0, n1, stage2, 0, unroll=128)


def _fft_tables(s, n1, n2):
    c = FOURIER_GROUP_DIM
    ic = np.arange(c)
    ang_c = (2.0 * np.pi / c) * ((ic[:, None] * ic[None, :]) % c)
    fc = np.concatenate([np.cos(ang_c), -np.sin(ang_c)], axis=1) * c ** -0.5
    i1 = np.arange(n1)
    ang_1 = (2.0 * np.pi / n1) * ((i1[:, None] * i1[None, :]) % n1)
    c1, s1 = np.cos(ang_1), np.sin(ang_1)
    f1 = np.concatenate([np.concatenate([c1, s1], axis=1),
                         np.concatenate([-s1, c1], axis=1)], axis=0) * n1 ** -0.5
    k1 = np.arange(n1)[:, None, None]
    k2 = np.arange(n2)[None, :, None]
    m2 = np.arange(n2)[None, None, :]
    ang_2 = (2.0 * np.pi / s) * ((m2 * (k1 + n1 * k2)) % s)
    g2 = np.concatenate([np.cos(ang_2), np.sin(ang_2)], axis=2) * n2 ** -0.5
    return tuple(jnp.asarray(t.astype(np.float32)).astype(BF16) for t in (fc, f1, g2))


def _fourier_mix(f):
    b, _, s, _ = f.shape
    n1 = FFT_N1
    n2 = s // n1
    fc, f1, g2 = _fft_tables(s, n1, n2)
    gd = FOURIER_GROUP_DIM
    p1 = n2 + PITCH_PAD
    p2 = 2 * n1 + PITCH_PAD
    return pl.pallas_call(
        functools.partial(_fft_kernel, n1=n1, n2=n2),
        grid=(b, N_FOURIER_GROUPS),
        in_specs=[
            pl.BlockSpec((1, None, s, gd), lambda i, g: (i, g, 0, 0)),
            _resident((gd, 2 * gd)),
            _resident((2 * n1, 2 * n1)),
            _resident((n1, n2, 2 * n2)),
        ],
        out_specs=pl.BlockSpec((1, 1, n2 // SUBLANES, n1 * SUBLANES, gd), lambda i, g: (i, g, 0, 0, 0)),
        out_shape=jax.ShapeDtypeStruct((b, N_FOURIER_GROUPS, n2 // SUBLANES, n1 * SUBLANES, gd), F32),
        scratch_shapes=[
            pltpu.VMEM((2, n1 * p1, gd), F32),
            pltpu.VMEM((n2 * p2, gd), F32),
        ],
        compiler_params=pltpu.CompilerParams(
            dimension_semantics=("parallel", "parallel"), vmem_limit_bytes=VMEM_LIMIT_BYTES),
        name="fourier_mix",
    )(f, fc, f1, g2)


def _ret_kernel(lg_ref, q_ref, k_ref, v_ref, g_ref, o_ref,
                sf_ref, sb_ref, p_ref, fr_ref, br_ref, *, n_chunks):
    c = CHUNK
    dk2 = 2 * RET_QK_DIM
    dv = RET_V_DIM
    pair = pl.program_id(1)
    lgf = [lg_ref[0, 2 * pair + h] for h in range(2)]
    lgb = [lg_ref[1, 2 * pair + h] for h in range(2)]

    row = lax.broadcasted_iota(jnp.int32, (c, c), 0)
    col = lax.broadcasted_iota(jnp.int32, (c, c), 1)
    diff = (row - col).astype(F32)
    decay2 = jnp.concatenate(
        [jnp.where(diff >= 0.0,
                   jnp.exp(lgf[h] * jnp.maximum(diff, 0.0)),
                   jnp.exp(lgb[h] * jnp.maximum(-diff, 0.0))) for h in range(2)], axis=1)

    pos = lax.broadcasted_iota(jnp.int32, (c, dk2), 0).astype(F32)
    head_a = lax.broadcasted_iota(jnp.int32, (c, dk2), 1) < RET_QK_DIM

    def per_head(fa, fb):
        return jnp.where(head_a, fa, fb).astype(BF16)

    q_dec_f = per_head(jnp.exp(lgf[0] * (pos + 1.0)), jnp.exp(lgf[1] * (pos + 1.0)))
    q_dec_b = per_head(jnp.exp(lgb[0] * (c - pos)), jnp.exp(lgb[1] * (c - pos)))
    k_dec_f = per_head(jnp.exp(lgf[0] * (c - 1.0 - pos)), jnp.exp(lgf[1] * (c - 1.0 - pos)))
    k_dec_b = per_head(jnp.exp(lgb[0] * pos), jnp.exp(lgb[1] * pos))

    srow = lax.broadcasted_iota(jnp.int32, (dk2, 2 * dv), 0) < RET_QK_DIM
    scol = lax.broadcasted_iota(jnp.int32, (dk2, 2 * dv), 1) < dv
    block_diag = srow == scol
    chunk_f = jnp.where(scol, jnp.exp(lgf[0] * c), jnp.exp(lgf[1] * c))
    chunk_b = jnp.where(scol, jnp.exp(lgb[0] * c), jnp.exp(lgb[1] * c))

    def kv_update(state, k_dec, chunk_dec, kc, vc):
        kd = kc * k_dec
        kv = lax.dot_general(kd, vc, (((0,), (0,)), ((), ())), preferred_element_type=F32)
        return chunk_dec * state + jnp.where(block_diag, kv, 0.0)

    fr_ref[...] = jnp.zeros_like(fr_ref)
    br_ref[...] = jnp.zeros_like(br_ref)

    def scan_step(t, carry):
        for run_ref, out_ref, i, k_dec, chunk_dec in (
                (fr_ref, sf_ref, t, k_dec_f, chunk_f),
                (br_ref, sb_ref, n_chunks - 1 - t, k_dec_b, chunk_b)):
            r0 = pl.multiple_of(i * c, c)
            state = run_ref[...]
            out_ref[i] = state.astype(BF16)
            run_ref[...] = kv_update(state, k_dec, chunk_dec,
                                     k_ref[0, pl.ds(r0, c), :], v_ref[0, pl.ds(r0, c), :])
        r0 = pl.multiple_of(t * c, c)
        kc = k_ref[0, pl.ds(r0, c), :]
        zero = jnp.zeros_like(kc)
        k_heads = jnp.concatenate([jnp.where(head_a, kc, zero), jnp.where(head_a, zero, kc)], axis=0)
        scores = lax.dot_general(q_ref[0, pl.ds(r0, c), :], k_heads, (((1,), (1,)), ((), ())),
                                 preferred_element_type=F32)
        p_ref[t] = (scores * decay2).astype(BF16)
        return carry

    lax.fori_loop(0, n_chunks, scan_step, 0, unroll=64)

    def out_step(i, carry):
        r0 = pl.multiple_of(i * c, c)
        qc = q_ref[0, pl.ds(r0, c), :]
        vc = v_ref[0, pl.ds(r0, c), :]
        q_inter = jnp.concatenate([qc * q_dec_f, qc * q_dec_b], axis=1)
        states = jnp.concatenate([sf_ref[i], sb_ref[i]], axis=0)
        inter = jnp.dot(q_inter, states, preferred_element_type=F32)
        outs = []
        for h in range(2):
            o = jnp.dot(p_ref[i, :, h * c:(h + 1) * c], vc[:, h * dv:(h + 1) * dv],
                        preferred_element_type=F32)
            o = o + inter[:, h * dv:(h + 1) * dv]
            mu = jnp.mean(o, axis=-1, keepdims=True)
            d = o - mu
            var = jnp.mean(d * d, axis=-1, keepdims=True)
            outs.append(d * lax.rsqrt(var + GN_EPS))
        gate = g_ref[0, pl.ds(r0, c), :]
        o_ref[0, pl.ds(r0, c), :] = jnp.concatenate(outs, axis=1).astype(BF16) * (gate * jax.nn.sigmoid(gate))
        return carry

    lax.fori_loop(0, n_chunks, out_step, 0, unroll=8)


def _retention(q, k, v, g, log_gamma):
    b, _, s, _ = q.shape
    n_chunks = s // CHUNK
    n_pairs = N_RET_HEADS // 2
    dk2 = 2 * RET_QK_DIM
    dv2 = 2 * RET_V_DIM
    return pl.pallas_call(
        functools.partial(_ret_kernel, n_chunks=n_chunks),
        grid=(b, n_pairs),
        in_specs=[
            pl.BlockSpec(memory_space=pltpu.SMEM),
            pl.BlockSpec((1, None, s, dk2), lambda i, p: (i, p, 0, 0)),
            pl.BlockSpec((1, None, s, dk2), lambda i, p: (i, p, 0, 0)),
            pl.BlockSpec((1, None, s, dv2), lambda i, p: (i, p, 0, 0)),
            pl.BlockSpec((1, None, s, dv2), lambda i, p: (i, p, 0, 0)),
        ],
        out_specs=pl.BlockSpec((1, None, s, dv2), lambda i, p: (i, p, 0, 0)),
        out_shape=jax.ShapeDtypeStruct((b, n_pairs, s, dv2), BF16),
        scratch_shapes=[
            pltpu.VMEM((n_chunks, dk2, dv2), BF16),
            pltpu.VMEM((n_chunks, dk2, dv2), BF16),
            pltpu.VMEM((n_chunks, CHUNK, 2 * CHUNK), BF16),
            pltpu.VMEM((dk2, dv2), F32),
            pltpu.VMEM((dk2, dv2), F32),
        ],
        compiler_params=pltpu.CompilerParams(
            dimension_semantics=("parallel", "parallel"), vmem_limit_bytes=VMEM_LIMIT_BYTES),
        name="retention",
    )(log_gamma, q, k, v, g)


def _merge_kernel(x_ref, yf_ref, ret_ref, ga_ref, gb_ref, wf_ref, wr_ref, wo_ref, o_ref, m_ref):
    n_groups = yf_ref.shape[1]
    k2_tile = SUBLANES
    n1 = yf_ref.shape[2] // k2_tile
    yf = jnp.concatenate(
        [jnp.concatenate([yf_ref[0, g, pl.ds(k2l, n1, stride=k2_tile), :] for g in range(n_groups)], axis=1)
         for k2l in range(k2_tile)], axis=0)
    yf = yf.astype(BF16)
    ret = jnp.concatenate([ret_ref[0, p] for p in range(ret_ref.shape[1])], axis=1)
    for lo in range(0, m_ref.shape[1], MERGE_COLS):
        cols = slice(lo, lo + MERGE_COLS)
        a = jnp.dot(yf, wf_ref[:, cols], preferred_element_type=F32)
        bb = jnp.dot(ret, wr_ref[:, cols], preferred_element_type=F32)
        m_ref[:, cols] = (jax.nn.sigmoid(ga_ref[0, :, cols].astype(F32)) * a
                          + jax.nn.sigmoid(gb_ref[0, :, cols].astype(F32)) * bb).astype(BF16)
    o_ref[0] = x_ref[0] + jnp.dot(m_ref[...], wo_ref[...], preferred_element_type=F32)


def _merge(x, y_four, y_ret, g_a, g_b, wf, wr, wo):
    b, s, d = x.shape
    _, n_groups, _, tm, gd = y_four.shape

    def tile(w):
        return pl.BlockSpec((1, tm, w), lambda i, j: (i, j, 0))

    return pl.pallas_call(
        _merge_kernel,
        grid=(b, s // tm),
        in_specs=[tile(d),
                  pl.BlockSpec((1, n_groups, None, tm, gd), lambda i, j: (i, 0, j, 0, 0)),
                  pl.BlockSpec((1,) + y_ret.shape[1:2] + (tm,) + y_ret.shape[3:], lambda i, j: (i, 0, j, 0)),
                  tile(d), tile(d),
                  _resident(wf.shape), _resident(wr.shape), _resident(wo.shape)],
        out_specs=tile(d),
        out_shape=jax.ShapeDtypeStruct((b, s, d), F32),
        scratch_shapes=[pltpu.VMEM((tm, d), BF16)],
        compiler_params=pltpu.CompilerParams(
            dimension_semantics=("parallel", "parallel"), vmem_limit_bytes=VMEM_LIMIT_BYTES),
        name="merge",
    )(x, y_four, y_ret, g_a, g_b, wf, wr, wo)


def _ffn_kernel(xp_ref, x_ref, xn_ref, g2_ref, wu_ref, cw_ref, cb_ref, wd_ref, gf_ref, o_ref, act_ref):
    tm = x_ref.shape[1]
    halo = SUBLANES
    j = pl.program_id(1)
    last = pl.num_programs(1) - 1
    x = x_ref[0]
    xx = jnp.concatenate([x, xp_ref[0], xn_ref[0]], axis=0)
    u = _rms(xx, g2_ref[...]).astype(BF16)
    u_mid = u[0:tm]
    keep_top = (j > 0).astype(F32)
    keep_bot = (j < last).astype(F32)

    for ci in range(D_FF // FF_CHUNK):
        lo = ci * FF_CHUNK
        hg = jnp.dot(u, wu_ref[:, lo:lo + FF_CHUNK], preferred_element_type=F32)
        hv = jnp.dot(u_mid, wu_ref[:, D_FF + lo:D_FF + lo + FF_CHUNK], preferred_element_type=F32)
        mid = hg[0:tm]
        seq = jnp.concatenate([hg[tm:tm + halo] * keep_top, mid, hg[tm + halo:] * keep_bot], axis=0)
        cw = cw_ref[:, lo:lo + FF_CHUNK]
        prev = pltpu.roll(seq, 1, 0)[halo:halo + tm]
        nxt = pltpu.roll(seq, tm + 2 * halo - 1, 0)[halo:halo + tm]
        hc = prev * cw[0:1] + mid * cw[1:2] + nxt * cw[2:3] + cb_ref[:, lo:lo + FF_CHUNK]
        act = 0.5 * hc * (1.0 + lax.erf(hc * (2.0 ** -0.5)))
        act_ref[:, lo:lo + FF_CHUNK] = (act * hv).astype(BF16)
    for r in range(0, tm, FFN_OUT_ROWS):
        rows = slice(r, r + FFN_OUT_ROWS)
        y = jnp.dot(act_ref[rows, :], wd_ref[...], preferred_element_type=F32)
        o_ref[0, rows, :] = _rms(x[rows] + y, gf_ref[...])


def _ffn(x1, norm2_g, w_up, conv_w, conv_b, w_down, final_g):
    b, s, d = x1.shape
    tm = TOKEN_TILE
    hb = tm // SUBLANES
    n_hblocks = s // SUBLANES
    return pl.pallas_call(
        _ffn_kernel,
        grid=(b, s // tm),
        in_specs=[
            pl.BlockSpec((1, SUBLANES, d), lambda i, j: (i, jnp.maximum(j * hb - 1, 0), 0)),
            pl.BlockSpec((1, tm, d), lambda i, j: (i, j, 0)),
            pl.BlockSpec((1, SUBLANES, d), lambda i, j: (i, jnp.minimum((j + 1) * hb, n_hblocks - 1), 0)),
            _resident((1, d)),
            _resident(w_up.shape),
            _resident(conv_w.shape),
            _resident(conv_b.shape),
            _resident(w_down.shape),
            _resident((1, d)),
        ],
        out_specs=pl.BlockSpec((1, tm, d), lambda i, j: (i, j, 0)),
        out_shape=jax.ShapeDtypeStruct((b, s, d), F32),
        scratch_shapes=[pltpu.VMEM((tm, D_FF), BF16)],
        compiler_params=pltpu.CompilerParams(
            dimension_semantics=("parallel", "parallel"), vmem_limit_bytes=VMEM_LIMIT_BYTES),
        name="convglu_ffn",
    )(x1, x1, x1, norm2_g, w_up, conv_w, conv_b, w_down, final_g)


def _rotary_tables(s):
    inv = ROPE_THETA ** (-np.arange(0, RET_QK_DIM, 2, dtype=np.float64) / RET_QK_DIM)
    ang = np.arange(s, dtype=np.float64)[:, None] * inv[None, :]
    cos, sin = np.cos(ang), np.sin(ang)
    reps = LANES // RET_QK_DIM
    cos_tab = np.concatenate([cos, cos] * reps, axis=1).astype(np.float32)
    sin_tab = np.concatenate([-sin, sin] * reps, axis=1).astype(np.float32)
    assert cos_tab.shape == (s, LANES)
    return jnp.asarray(cos_tab), jnp.asarray(sin_tab)


def _encoder_layer(x, p):
    b, s, d = x.shape
    cos_tab, sin_tab = _rotary_tables(s)
    f, q, k, v, g_ret, g_a, g_b = _inproj(x, p["norm1_g"], p["w_in"], cos_tab, sin_tab)
    y_four = _fourier_mix(f)
    y_ret = _retention(q, k, v, g_ret, p["log_gamma"])
    return _merge(x, y_four, y_ret, g_a, g_b, p["w_four_proj"], p["w_ret_proj"], p["w_out"])


def kernel(x_prompt, x_sample, norm1_g, w_in, w_four_proj, w_ret_proj, w_out, ret_decay_logit,
           norm2_g, w_up, conv_w, conv_b, w_down, final_norm_g):
    depth = w_in.shape[0]
    assert depth == 1, "the final RMSNorm is fused into the (single) layer's FFN kernel"
    layers = []
    for l in range(depth):
        layers.append(dict(
            norm1_g=norm1_g[l][None, :],
            w_in=w_in[l].astype(BF16),
            w_four_proj=w_four_proj[l].astype(BF16),
            w_ret_proj=w_ret_proj[l].astype(BF16),
            w_out=w_out[l].astype(BF16),
            log_gamma=jax.nn.log_sigmoid(ret_decay_logit[l].astype(F32)),
            norm2_g=norm2_g[l][None, :],
            w_up=w_up[l].astype(BF16),
            conv_w=conv_w[l],
            conv_b=conv_b[l][None, :],
            w_down=w_down[l].astype(BF16),
        ))
    final_g = final_norm_g[None, :]

    def trunk(x):
        p = layers[0]
        x1 = _encoder_layer(x, p)
        return _ffn(x1, p["norm2_g"], p["w_up"], p["conv_w"], p["conv_b"], p["w_down"], final_g)

    return (trunk(x_prompt), trunk(x_sample))
```

```python
import functools

import numpy as np
import jax
import jax.numpy as jnp
from jax import lax
from jax.experimental import pallas as pl
from jax.experimental.pallas import tpu as pltpu

D_MODEL = 1024
N_FOURIER_GROUPS = 4
FOURIER_GROUP_DIM = 128
FOURIER_WIDTH = N_FOURIER_GROUPS * FOURIER_GROUP_DIM
N_RET_HEADS = 8
RET_QK_DIM = 64
RET_V_DIM = 128
RET_QK_WIDTH = N_RET_HEADS * RET_QK_DIM
RET_V_WIDTH = N_RET_HEADS * RET_V_DIM
CHUNK = 128
ROPE_THETA = 10000.0
D_FF = 2816
NORM_EPS = 1e-6
GN_EPS = 1e-5
IN_SPLITS = (FOURIER_WIDTH, RET_QK_WIDTH, RET_QK_WIDTH, RET_V_WIDTH, RET_V_WIDTH, D_MODEL, D_MODEL)
IN_OFFSETS = tuple(int(o) for o in np.cumsum((0,) + IN_SPLITS))
IN_WIDTH = IN_OFFSETS[-1]

LANES = 128
SUBLANES = 8
VMEM_LIMIT_BYTES = 56 * 1024 * 1024

F32 = jnp.float32
BF16 = jnp.bfloat16

TOKEN_TILE = 1024
FF_CHUNK = 256
FFN_OUT_ROWS = 256
FFT_N1 = 128
FFT_STAGE0_ROWS = 512
MERGE_COLS = 256
PITCH_PAD = SUBLANES


def _resident(shape):
    return pl.BlockSpec(shape, lambda *_: (0,) * len(shape), pipeline_mode=pl.Buffered(1))


def _rms(x, g):
    ms = jnp.mean(x * x, axis=-1, keepdims=True)
    return x * lax.rsqrt(ms + NORM_EPS) * g


def _inproj_kernel(x_ref, g_ref, w_ref, cos_ref, sin_ref,
                   f_ref, q_ref, k_ref, v_ref, gr_ref, ga_ref, gb_ref):
    u = _rms(x_ref[0], g_ref[...]).astype(BF16)

    def proj(i):
        return jnp.dot(u, w_ref[:, IN_OFFSETS[i]:IN_OFFSETS[i + 1]], preferred_element_type=F32)

    reps = RET_QK_WIDTH // LANES
    cos = jnp.concatenate([cos_ref[...]] * reps, axis=1)
    sin = jnp.concatenate([sin_ref[...]] * reps, axis=1)
    lane = lax.broadcasted_iota(jnp.int32, cos.shape, 1)
    first_half = (lane % RET_QK_DIM) < (RET_QK_DIM // 2)

    def rotary(t):
        partner = jnp.where(first_half,
                            pltpu.roll(t, RET_QK_WIDTH - RET_QK_DIM // 2, 1),
                            pltpu.roll(t, RET_QK_DIM // 2, 1))
        return t * cos + partner * sin

    def store_split(ref, val):
        parts, w = ref.shape[1], ref.shape[3]
        for p in range(parts):
            ref[0, p] = val[:, p * w:(p + 1) * w].astype(BF16)

    store_split(f_ref, proj(0))
    store_split(q_ref, rotary(proj(1)) * (RET_QK_DIM ** -0.5))
    store_split(k_ref, rotary(proj(2)))
    store_split(v_ref, proj(3))
    store_split(gr_ref, proj(4))
    ga_ref[0] = proj(5).astype(BF16)
    gb_ref[0] = proj(6).astype(BF16)


def _inproj(x, norm_g, w_in_bf16, cos_tab, sin_tab):
    b, s, d = x.shape
    tm = TOKEN_TILE
    parts = (N_FOURIER_GROUPS,) + (N_RET_HEADS // 2,) * 4
    out_shape = [jax.ShapeDtypeStruct((b, p, s, w // p), BF16) for p, w in zip(parts, IN_SPLITS[:5])]
    out_specs = [pl.BlockSpec((1, p, tm, w // p), lambda i, j: (i, 0, j, 0))
                 for p, w in zip(parts, IN_SPLITS[:5])]
    out_shape += [jax.ShapeDtypeStruct((b, s, w), BF16) for w in IN_SPLITS[5:]]
    out_specs += [pl.BlockSpec((1, tm, w), lambda i, j: (i, j, 0)) for w in IN_SPLITS[5:]]
    return pl.pallas_call(
        _inproj_kernel,
        grid=(b, s // tm),
        in_specs=[
            pl.BlockSpec((1, tm, d), lambda i, j: (i, j, 0)),
            _resident((1, d)),
            _resident((d, IN_WIDTH)),
            pl.BlockSpec((tm, LANES), lambda i, j: (j, 0)),
            pl.BlockSpec((tm, LANES), lambda i, j: (j, 0)),
        ],
        out_specs=out_specs,
        out_shape=out_shape,
        compiler_params=pltpu.CompilerParams(
            dimension_semantics=("parallel", "parallel"), vmem_limit_bytes=VMEM_LIMIT_BYTES),
        name="inproj",
    )(x, norm_g, w_in_bf16, cos_tab, sin_tab)


def _fft_kernel(x_ref, fc_ref, f1_ref, g2_ref, o_ref, z_ref, y_ref, *, n1, n2):
    p1 = n2 + PITCH_PAD
    p2 = 2 * n1 + PITCH_PAD
    gd = FOURIER_GROUP_DIM

    slabs = FFT_STAGE0_ROWS // n2

    def stage0(i, carry):
        r0 = pl.multiple_of(i * FFT_STAGE0_ROWS, FFT_STAGE0_ROWS)
        z = jnp.dot(x_ref[0, pl.ds(r0, FFT_STAGE0_ROWS), :], fc_ref[...],
                    preferred_element_type=F32)
        for s in range(slabs):
            row = pl.multiple_of((i * slabs + s) * p1, SUBLANES)
            z_ref[0, pl.ds(row, n2), :] = z[s * n2:(s + 1) * n2, :gd]
            z_ref[1, pl.ds(row, n2), :] = z[s * n2:(s + 1) * n2, gd:]
        return carry

    lax.fori_loop(0, (n1 * n2) // FFT_STAGE0_ROWS, stage0, 0, unroll=16)

    def stage1(jj, carry):
        cols = []
        for t in range(2):
            j = 2 * jj + t
            zr = z_ref[0, pl.ds(j, n1, stride=p1), :]
            zi = z_ref[1, pl.ds(j, n1, stride=p1), :]
            cols.append(jnp.concatenate([zr, zi], axis=0).astype(BF16))
        rhs = jnp.concatenate(cols, axis=1)
        y = jnp.dot(f1_ref[...], rhs, preferred_element_type=F32)
        for t in range(2):
            row = pl.multiple_of((2 * jj + t) * p2, SUBLANES)
            y_ref[pl.ds(row, 2 * n1), :] = y[:, t * gd:(t + 1) * gd]
        return carry

    lax.fori_loop(0, n2 // 2, stage1, 0, unroll=32)

    def stage2(k1, carry):
        yr = y_ref[pl.ds(k1, n2, stride=p2), :]
        yi = y_ref[pl.ds(n1 + k1, n2, stride=p2), :]
        rhs = jnp.concatenate([yr, yi], axis=0).astype(BF16)
        out = jnp.dot(g2_ref[k1], rhs, preferred_element_type=F32)
        row = pl.multiple_of(k1 * SUBLANES, SUBLANES)
        o_ref[0, 0, :, pl.ds(row, SUBLANES), :] = out.reshape(n2 // SUBLANES, SUBLANES, gd)
        return carry

    lax.fori_loop(0, n1, stage2, 0, unroll=128)


def _fft_tables(s, n1, n2):
    c = FOURIER_GROUP_DIM
    ic = np.arange(c)
    ang_c = (2.0 * np.pi / c) * ((ic[:, None] * ic[None, :]) % c)
    fc = np.concatenate([np.cos(ang_c), -np.sin(ang_c)], axis=1) * c ** -0.5
    i1 = np.arange(n1)
    ang_1 = (2.0 * np.pi / n1) * ((i1[:, None] * i1[None, :]) % n1)
    c1, s1 = np.cos(ang_1), np.sin(ang_1)
    f1 = np.concatenate([np.concatenate([c1, s1], axis=1),
                         np.concatenate([-s1, c1], axis=1)], axis=0) * n1 ** -0.5
    k1 = np.arange(n1)[:, None, None]
    k2 = np.arange(n2)[None, :, None]
    m2 = np.arange(n2)[None, None, :]
    ang_2 = (2.0 * np.pi / s) * ((m2 * (k1 + n1 * k2)) % s)
    g2 = np.concatenate([np.cos(ang_2), np.sin(ang_2)], axis=2) * n2 ** -0.5
    return tuple(jnp.asarray(t.astype(np.float32)).astype(BF16) for t in (fc, f1, g2))


def _fourier_mix(f):
    b, _, s, _ = f.shape
    n1 = FFT_N1
    n2 = s // n1
    fc, f1, g2 = _fft_tables(s, n1, n2)
    gd = FOURIER_GROUP_DIM
    p1 = n2 + PITCH_PAD
    p2 = 2 * n1 + PITCH_PAD
    return pl.pallas_call(
        functools.partial(_fft_kernel, n1=n1, n2=n2),
        grid=(b, N_FOURIER_GROUPS),
        in_specs=[
            pl.BlockSpec((1, None, s, gd), lambda i, g: (i, g, 0, 0)),
            _resident((gd, 2 * gd)),
            _resident((2 * n1, 2 * n1)),
            _resident((n1, n2, 2 * n2)),
        ],
        out_specs=pl.BlockSpec((1, 1, n2 // SUBLANES, n1 * SUBLANES, gd), lambda i, g: (i, g, 0, 0, 0)),
        out_shape=jax.ShapeDtypeStruct((b, N_FOURIER_GROUPS, n2 // SUBLANES, n1 * SUBLANES, gd), F32),
        scratch_shapes=[
            pltpu.VMEM((2, n1 * p1, gd), F32),
            pltpu.VMEM((n2 * p2, gd), F32),
        ],
        compiler_params=pltpu.CompilerParams(
            dimension_semantics=("parallel", "parallel"), vmem_limit_bytes=VMEM_LIMIT_BYTES),
        name="fourier_mix",
    )(f, fc, f1, g2)


def _ret_kernel(lg_ref, q_ref, k_ref, v_ref, g_ref, o_ref,
                sf_ref, sb_ref, p_ref, fr_ref, br_ref, *, n_chunks):
    c = CHUNK
    dk2 = 2 * RET_QK_DIM
    dv = RET_V_DIM
    pair = pl.program_id(1)
    lgf = [lg_ref[0, 2 * pair + h] for h in range(2)]
    lgb = [lg_ref[1, 2 * pair + h] for h in range(2)]

    row = lax.broadcasted_iota(jnp.int32, (c, c), 0)
    col = lax.broadcasted_iota(jnp.int32, (c, c), 1)
    diff = (row - col).astype(F32)
    decay2 = jnp.concatenate(
        [jnp.where(diff >= 0.0,
                   jnp.exp(lgf[h] * jnp.maximum(diff, 0.0)),
                   jnp.exp(lgb[h] * jnp.maximum(-diff, 0.0))) for h in range(2)], axis=1)

    pos = lax.broadcasted_iota(jnp.int32, (c, dk2), 0).astype(F32)
    head_a = lax.broadcasted_iota(jnp.int32, (c, dk2), 1) < RET_QK_DIM

    def per_head(fa, fb):
        return jnp.where(head_a, fa, fb).astype(BF16)

    q_dec_f = per_head(jnp.exp(lgf[0] * (pos + 1.0)), jnp.exp(lgf[1] * (pos + 1.0)))
    q_dec_b = per_head(jnp.exp(lgb[0] * (c - pos)), jnp.exp(lgb[1] * (c - pos)))
    k_dec_f = per_head(jnp.exp(lgf[0] * (c - 1.0 - pos)), jnp.exp(lgf[1] * (c - 1.0 - pos)))
    k_dec_b = per_head(jnp.exp(lgb[0] * pos), jnp.exp(lgb[1] * pos))

    srow = lax.broadcasted_iota(jnp.int32, (dk2, 2 * dv), 0) < RET_QK_DIM
    scol = lax.broadcasted_iota(jnp.int32, (dk2, 2 * dv), 1) < dv
    block_diag = srow == scol
    chunk_f = jnp.where(scol, jnp.exp(lgf[0] * c), jnp.exp(lgf[1] * c))
    chunk_b = jnp.where(scol, jnp.exp(lgb[0] * c), jnp.exp(lgb[1] * c))

    def kv_update(state, k_dec, chunk_dec, kc, vc):
        kd = kc * k_dec
        kv = lax.dot_general(kd, vc, (((0,), (0,)), ((), ())), preferred_element_type=F32)
        return chunk_dec * state + jnp.where(block_diag, kv, 0.0)

    fr_ref[...] = jnp.zeros_like(fr_ref)
    br_ref[...] = jnp.zeros_like(br_ref)

    def scan_step(t, carry):
        for run_ref, out_ref, i, k_dec, chunk_dec in (
                (fr_ref, sf_ref, t, k_dec_f, chunk_f),
                (br_ref, sb_ref, n_chunks - 1 - t, k_dec_b, chunk_b)):
            r0 = pl.multiple_of(i * c, c)
            state = run_ref[...]
            out_ref[i] = state.astype(BF16)
            run_ref[...] = kv_update(state, k_dec, chunk_dec,
                                     k_ref[0, pl.ds(r0, c), :], v_ref[0, pl.ds(r0, c), :])
        r0 = pl.multiple_of(t * c, c)
        kc = k_ref[0, pl.ds(r0, c), :]
        zero = jnp.zeros_like(kc)
        k_heads = jnp.concatenate([jnp.where(head_a, kc, zero), jnp.where(head_a, zero, kc)], axis=0)
        scores = lax.dot_general(q_ref[0, pl.ds(r0, c), :], k_heads, (((1,), (1,)), ((), ())),
                                 preferred_element_type=F32)
        p_ref[t] = (scores * decay2).astype(BF16)
        return carry

    lax.fori_loop(0, n_chunks, scan_step, 0, unroll=64)

    def out_step(i, carry):
        r0 = pl.multiple_of(i * c, c)
        qc = q_ref[0, pl.ds(r0, c), :]
        vc = v_ref[0, pl.ds(r0, c), :]
        q_inter = jnp.concatenate([qc * q_dec_f, qc * q_dec_b], axis=1)
        states = jnp.concatenate([sf_ref[i], sb_ref[i]], axis=0)
        inter = jnp.dot(q_inter, states, preferred_element_type=F32)
        outs = []
        for h in range(2):
            o = jnp.dot(p_ref[i, :, h * c:(h + 1) * c], vc[:, h * dv:(h + 1) * dv],
                        preferred_element_type=F32)
            o = o + inter[:, h * dv:(h + 1) * dv]
            mu = jnp.mean(o, axis=-1, keepdims=True)
            d = o - mu
            var = jnp.mean(d * d, axis=-1, keepdims=True)
            outs.append(d * lax.rsqrt(var + GN_EPS))
        gate = g_ref[0, pl.ds(r0, c), :]
        o_ref[0, pl.ds(r0, c), :] = jnp.concatenate(outs, axis=1).astype(BF16) * (gate * jax.nn.sigmoid(gate))
        return carry

    lax.fori_loop(0, n_chunks, out_step, 0, unroll=16)


def _retention(q, k, v, g, log_gamma):
    b, _, s, _ = q.shape
    n_chunks = s // CHUNK
    n_pairs = N_RET_HEADS // 2
    dk2 = 2 * RET_QK_DIM
    dv2 = 2 * RET_V_DIM
    return pl.pallas_call(
        functools.partial(_ret_kernel, n_chunks=n_chunks),
        grid=(b, n_pairs),
        in_specs=[
            pl.BlockSpec(memory_space=pltpu.SMEM),
            pl.BlockSpec((1, None, s, dk2), lambda i, p: (i, p, 0, 0)),
            pl.BlockSpec((1, None, s, dk2), lambda i, p: (i, p, 0, 0)),
            pl.BlockSpec((1, None, s, dv2), lambda i, p: (i, p, 0, 0)),
            pl.BlockSpec((1, None, s, dv2), lambda i, p: (i, p, 0, 0)),
        ],
        out_specs=pl.BlockSpec((1, None, s, dv2), lambda i, p: (i, p, 0, 0)),
        out_shape=jax.ShapeDtypeStruct((b, n_pairs, s, dv2), BF16),
        scratch_shapes=[
            pltpu.VMEM((n_chunks, dk2, dv2), BF16),
            pltpu.VMEM((n_chunks, dk2, dv2), BF16),
            pltpu.VMEM((n_chunks, CHUNK, 2 * CHUNK), BF16),
            pltpu.VMEM((dk2, dv2), F32),
            pltpu.VMEM((dk2, dv2), F32),
        ],
        compiler_params=pltpu.CompilerParams(
            dimension_semantics=("parallel", "parallel"), vmem_limit_bytes=VMEM_LIMIT_BYTES),
        name="retention",
    )(log_gamma, q, k, v, g)


def _merge_kernel(x_ref, yf_ref, ret_ref, ga_ref, gb_ref, wf_ref, wr_ref, wo_ref, o_ref, m_ref):
    n_groups = yf_ref.shape[1]
    k2_tile = SUBLANES
    n1 = yf_ref.shape[2] // k2_tile
    yf = jnp.concatenate(
        [jnp.concatenate([yf_ref[0, g, pl.ds(k2l, n1, stride=k2_tile), :] for g in range(n_groups)], axis=1)
         for k2l in range(k2_tile)], axis=0)
    yf = yf.astype(BF16)
    ret = jnp.concatenate([ret_ref[0, p] for p in range(ret_ref.shape[1])], axis=1)
    for lo in range(0, m_ref.shape[1], MERGE_COLS):
        cols = slice(lo, lo + MERGE_COLS)
        a = jnp.dot(yf, wf_ref[:, cols], preferred_element_type=F32)
        bb = jnp.dot(ret, wr_ref[:, cols], preferred_element_type=F32)
        m_ref[:, cols] = (jax.nn.sigmoid(ga_ref[0, :, cols].astype(F32)) * a
                          + jax.nn.sigmoid(gb_ref[0, :, cols].astype(F32)) * bb).astype(BF16)
    o_ref[0] = x_ref[0] + jnp.dot(m_ref[...], wo_ref[...], preferred_element_type=F32)


def _merge(x, y_four, y_ret, g_a, g_b, wf, wr, wo):
    b, s, d = x.shape
    _, n_groups, _, tm, gd = y_four.shape

    def tile(w):
        return pl.BlockSpec((1, tm, w), lambda i, j: (i, j, 0))

    return pl.pallas_call(
        _merge_kernel,
        grid=(b, s // tm),
        in_specs=[tile(d),
                  pl.BlockSpec((1, n_groups, None, tm, gd), lambda i, j: (i, 0, j, 0, 0)),
                  pl.BlockSpec((1,) + y_ret.shape[1:2] + (tm,) + y_ret.shape[3:], lambda i, j: (i, 0, j, 0)),
                  tile(d), tile(d),
                  _resident(wf.shape), _resident(wr.shape), _resident(wo.shape)],
        out_specs=tile(d),
        out_shape=jax.ShapeDtypeStruct((b, s, d), F32),
        scratch_shapes=[pltpu.VMEM((tm, d), BF16)],
        compiler_params=pltpu.CompilerParams(
            dimension_semantics=("parallel", "parallel"), vmem_limit_bytes=VMEM_LIMIT_BYTES),
        name="merge",
    )(x, y_four, y_ret, g_a, g_b, wf, wr, wo)


def _ffn_kernel(xp_ref, x_ref, xn_ref, g2_ref, wu_ref, cw_ref, cb_ref, wd_ref, gf_ref, o_ref, act_ref):
    tm = x_ref.shape[1]
    halo = SUBLANES
    j = pl.program_id(1)
    last = pl.num_programs(1) - 1
    x = x_ref[0]
    xx = jnp.concatenate([x, xp_ref[0], xn_ref[0]], axis=0)
    u = _rms(xx, g2_ref[...]).astype(BF16)
    u_mid = u[0:tm]
    keep_top = (j > 0).astype(F32)
    keep_bot = (j < last).astype(F32)

    for ci in range(D_FF // FF_CHUNK):
        lo = ci * FF_CHUNK
        hg = jnp.dot(u, wu_ref[:, lo:lo + FF_CHUNK], preferred_element_type=F32)
        hv = jnp.dot(u_mid, wu_ref[:, D_FF + lo:D_FF + lo + FF_CHUNK], preferred_element_type=F32)
        mid = hg[0:tm]
        seq = jnp.concatenate([hg[tm:tm + halo] * keep_top, mid, hg[tm + halo:] * keep_bot], axis=0)
        cw = cw_ref[:, lo:lo + FF_CHUNK]
        prev = pltpu.roll(seq, 1, 0)[halo:halo + tm]
        nxt = pltpu.roll(seq, tm + 2 * halo - 1, 0)[halo:halo + tm]
        hc = prev * cw[0:1] + mid * cw[1:2] + nxt * cw[2:3] + cb_ref[:, lo:lo + FF_CHUNK]
        act = 0.5 * hc * (1.0 + lax.erf(hc * (2.0 ** -0.5)))
        act_ref[:, lo:lo + FF_CHUNK] = (act * hv).astype(BF16)
    for r in range(0, tm, FFN_OUT_ROWS):
        rows = slice(r, r + FFN_OUT_ROWS)
        y = jnp.dot(act_ref[rows, :], wd_ref[...], preferred_element_type=F32)
        o_ref[0, rows, :] = _rms(x[rows] + y, gf_ref[...])


def _ffn(x1, norm2_g, w_up, conv_w, conv_b, w_down, final_g):
    b, s, d = x1.shape
    tm = TOKEN_TILE
    hb = tm // SUBLANES
    n_hblocks = s // SUBLANES
    return pl.pallas_call(
        _ffn_kernel,
        grid=(b, s // tm),
        in_specs=[
            pl.BlockSpec((1, SUBLANES, d), lambda i, j: (i, jnp.maximum(j * hb - 1, 0), 0)),
            pl.BlockSpec((1, tm, d), lambda i, j: (i, j, 0)),
            pl.BlockSpec((1, SUBLANES, d), lambda i, j: (i, jnp.minimum((j + 1) * hb, n_hblocks - 1), 0)),
            _resident((1, d)),
            _resident(w_up.shape),
            _resident(conv_w.shape),
            _resident(conv_b.shape),
            _resident(w_down.shape),
            _resident((1, d)),
        ],
        out_specs=pl.BlockSpec((1, tm, d), lambda i, j: (i, j, 0)),
        out_shape=jax.ShapeDtypeStruct((b, s, d), F32),
        scratch_shapes=[pltpu.VMEM((tm, D_FF), BF16)],
        compiler_params=pltpu.CompilerParams(
            dimension_semantics=("parallel", "parallel"), vmem_limit_bytes=VMEM_LIMIT_BYTES),
        name="convglu_ffn",
    )(x1, x1, x1, norm2_g, w_up, conv_w, conv_b, w_down, final_g)


def _rotary_tables(s):
    inv = ROPE_THETA ** (-np.arange(0, RET_QK_DIM, 2, dtype=np.float64) / RET_QK_DIM)
    ang = np.arange(s, dtype=np.float64)[:, None] * inv[None, :]
    cos, sin = np.cos(ang), np.sin(ang)
    reps = LANES // RET_QK_DIM
    cos_tab = np.concatenate([cos, cos] * reps, axis=1).astype(np.float32)
    sin_tab = np.concatenate([-sin, sin] * reps, axis=1).astype(np.float32)
    assert cos_tab.shape == (s, LANES)
    return jnp.asarray(cos_tab), jnp.asarray(sin_tab)


def _encoder_layer(x, p):
    b, s, d = x.shape
    cos_tab, sin_tab = _rotary_tables(s)
    f, q, k, v, g_ret, g_a, g_b = _inproj(x, p["norm1_g"], p["w_in"], cos_tab, sin_tab)
    y_four = _fourier_mix(f)
    y_ret = _retention(q, k, v, g_ret, p["log_gamma"])
    return _merge(x, y_four, y_ret, g_a, g_b, p["w_four_proj"], p["w_ret_proj"], p["w_out"])


def kernel(x_prompt, x_sample, norm1_g, w_in, w_four_proj, w_ret_proj, w_out, ret_decay_logit,
           norm2_g, w_up, conv_w, conv_b, w_down, final_norm_g):
    depth = w_in.shape[0]
    assert depth == 1, "the final RMSNorm is fused into the (single) layer's FFN kernel"
    layers = []
    for l in range(depth):
        layers.append(dict(
            norm1_g=norm1_g[l][None, :],
            w_in=w_in[l].astype(BF16),
            w_four_proj=w_four_proj[l].astype(BF16),
            w_ret_proj=w_ret_proj[l].astype(BF16),
            w_out=w_out[l].astype(BF16),
            log_gamma=jax.nn.log_sigmoid(ret_decay_logit[l].astype(F32)),
            norm2_g=norm2_g[l][None, :],
            w_up=w_up[l].astype(BF16),
            conv_w=conv_w[l],
            conv_b=conv_b[l][None, :],
            w_down=w_down[l].astype(BF16),
        ))
    final_g = final_norm_g[None, :]

    def trunk(x):
        p = layers[0]
        x1 = _encoder_layer(x, p)
        return _ffn(x1, p["norm2_g"], p["w_up"], p["conv_w"], p["conv_b"], p["w_down"], final_g)

    return (trunk(x_prompt), trunk(x_sample))
```

```python
import functools

import numpy as np
import jax
import jax.numpy as jnp
from jax import lax
from jax.experimental import pallas as pl
from jax.experimental.pallas import tpu as pltpu

D_MODEL = 1024
N_FOURIER_GROUPS = 4
FOURIER_GROUP_DIM = 128
FOURIER_WIDTH = N_FOURIER_GROUPS * FOURIER_GROUP_DIM
N_RET_HEADS = 8
RET_QK_DIM = 64
RET_V_DIM = 128
RET_QK_WIDTH = N_RET_HEADS * RET_QK_DIM
RET_V_WIDTH = N_RET_HEADS * RET_V_DIM
CHUNK = 128
ROPE_THETA = 10000.0
D_FF = 2816
NORM_EPS = 1e-6
GN_EPS = 1e-5
IN_SPLITS = (FOURIER_WIDTH, RET_QK_WIDTH, RET_QK_WIDTH, RET_V_WIDTH, RET_V_WIDTH, D_MODEL, D_MODEL)
IN_OFFSETS = tuple(int(o) for o in np.cumsum((0,) + IN_SPLITS))
IN_WIDTH = IN_OFFSETS[-1]

LANES = 128
SUBLANES = 8
VMEM_LIMIT_BYTES = 56 * 1024 * 1024

F32 = jnp.float32
BF16 = jnp.bfloat16

TOKEN_TILE = 1024
FF_CHUNK = 256
FFN_OUT_ROWS = 256
FFT_N1 = 128
FFT_STAGE0_ROWS = 512
MERGE_COLS = 256
PITCH_PAD = SUBLANES


def _resident(shape):
    return pl.BlockSpec(shape, lambda *_: (0,) * len(shape), pipeline_mode=pl.Buffered(1))


def _rms(x, g):
    ms = jnp.mean(x * x, axis=-1, keepdims=True)
    return x * lax.rsqrt(ms + NORM_EPS) * g


def _inproj_kernel(x_ref, g_ref, w_ref, cos_ref, sin_ref,
                   f_ref, q_ref, k_ref, v_ref, gr_ref, ga_ref, gb_ref):
    u = _rms(x_ref[0], g_ref[...]).astype(BF16)

    def proj(i):
        return jnp.dot(u, w_ref[:, IN_OFFSETS[i]:IN_OFFSETS[i + 1]], preferred_element_type=F32)

    reps = RET_QK_WIDTH // LANES
    cos = jnp.concatenate([cos_ref[...]] * reps, axis=1)
    sin = jnp.concatenate([sin_ref[...]] * reps, axis=1)
    lane = lax.broadcasted_iota(jnp.int32, cos.shape, 1)
    first_half = (lane % RET_QK_DIM) < (RET_QK_DIM // 2)

    def rotary(t):
        partner = jnp.where(first_half,
                            pltpu.roll(t, RET_QK_WIDTH - RET_QK_DIM // 2, 1),
                            pltpu.roll(t, RET_QK_DIM // 2, 1))
        return t * cos + partner * sin

    def store_split(ref, val):
        parts, w = ref.shape[1], ref.shape[3]
        for p in range(parts):
            ref[0, p] = val[:, p * w:(p + 1) * w].astype(BF16)

    store_split(f_ref, proj(0))
    store_split(q_ref, rotary(proj(1)) * (RET_QK_DIM ** -0.5))
    store_split(k_ref, rotary(proj(2)))
    store_split(v_ref, proj(3))
    store_split(gr_ref, proj(4))
    ga_ref[0] = proj(5).astype(BF16)
    gb_ref[0] = proj(6).astype(BF16)


def _inproj(x, norm_g, w_in_bf16, cos_tab, sin_tab):
    b, s, d = x.shape
    tm = TOKEN_TILE
    parts = (N_FOURIER_GROUPS,) + (N_RET_HEADS // 2,) * 4
    out_shape = [jax.ShapeDtypeStruct((b, p, s, w // p), BF16) for p, w in zip(parts, IN_SPLITS[:5])]
    out_specs = [pl.BlockSpec((1, p, tm, w // p), lambda i, j: (i, 0, j, 0))
                 for p, w in zip(parts, IN_SPLITS[:5])]
    out_shape += [jax.ShapeDtypeStruct((b, s, w), BF16) for w in IN_SPLITS[5:]]
    out_specs += [pl.BlockSpec((1, tm, w), lambda i, j: (i, j, 0)) for w in IN_SPLITS[5:]]
    return pl.pallas_call(
        _inproj_kernel,
        grid=(b, s // tm),
        in_specs=[
            pl.BlockSpec((1, tm, d), lambda i, j: (i, j, 0)),
            _resident((1, d)),
            _resident((d, IN_WIDTH)),
            pl.BlockSpec((tm, LANES), lambda i, j: (j, 0)),
            pl.BlockSpec((tm, LANES), lambda i, j: (j, 0)),
        ],
        out_specs=out_specs,
        out_shape=out_shape,
        compiler_params=pltpu.CompilerParams(
            dimension_semantics=("parallel", "parallel"), vmem_limit_bytes=VMEM_LIMIT_BYTES),
        name="inproj",
    )(x, norm_g, w_in_bf16, cos_tab, sin_tab)


def _fft_kernel(x_ref, fc_ref, f1_ref, g2_ref, o_ref, z_ref, y_ref, *, n1, n2):
    p1 = n2 + PITCH_PAD
    p2 = 2 * n1 + PITCH_PAD
    gd = FOURIER_GROUP_DIM

    slabs = FFT_STAGE0_ROWS // n2

    def stage0(i, carry):
        r0 = pl.multiple_of(i * FFT_STAGE0_ROWS, FFT_STAGE0_ROWS)
        z = jnp.dot(x_ref[0, pl.ds(r0, FFT_STAGE0_ROWS), :], fc_ref[...],
                    preferred_element_type=F32)
        for s in range(slabs):
            row = pl.multiple_of((i * slabs + s) * p1, SUBLANES)
            z_ref[0, pl.ds(row, n2), :] = z[s * n2:(s + 1) * n2, :gd]
            z_ref[1, pl.ds(row, n2), :] = z[s * n2:(s + 1) * n2, gd:]
        return carry

    lax.fori_loop(0, (n1 * n2) // FFT_STAGE0_ROWS, stage0, 0, unroll=16)

    def stage1(jj, carry):
        cols = []
        for t in range(2):
            j = 2 * jj + t
            zr = z_ref[0, pl.ds(j, n1, stride=p1), :]
            zi = z_ref[1, pl.ds(j, n1, stride=p1), :]
            cols.append(jnp.concatenate([zr, zi], axis=0).astype(BF16))
        rhs = jnp.concatenate(cols, axis=1)
        y = jnp.dot(f1_ref[...], rhs, preferred_element_type=F32)
        for t in range(2):
            row = pl.multiple_of((2 * jj + t) * p2, SUBLANES)
            y_ref[pl.ds(row, 2 * n1), :] = y[:, t * gd:(t + 1) * gd]
        return carry

    lax.fori_loop(0, n2 // 2, stage1, 0, unroll=32)

    def stage2(k1, carry):
        yr = y_ref[pl.ds(k1, n2, stride=p2), :]
        yi = y_ref[pl.ds(n1 + k1, n2, stride=p2), :]
        rhs = jnp.concatenate([yr, yi], axis=0).astype(BF16)
        out = jnp.dot(g2_ref[k1], rhs, preferred_element_type=F32)
        row = pl.multiple_of(k1 * SUBLANES, SUBLANES)
        o_ref[0, 0, :, pl.ds(row, SUBLANES), :] = out.reshape(n2 // SUBLANES, SUBLANES, gd)
        return carry

    lax.fori_loop(0, n1, stage2, 0, unroll=128)


def _fft_tables(s, n1, n2):
    c = FOURIER_GROUP_DIM
    ic = np.arange(c)
    ang_c = (2.0 * np.pi / c) * ((ic[:, None] * ic[None, :]) % c)
    fc = np.concatenate([np.cos(ang_c), -np.sin(ang_c)], axis=1) * c ** -0.5
    i1 = np.arange(n1)
    ang_1 = (2.0 * np.pi / n1) * ((i1[:, None] * i1[None, :]) % n1)
    c1, s1 = np.cos(ang_1), np.sin(ang_1)
    f1 = np.concatenate([np.concatenate([c1, s1], axis=1),
                         np.concatenate([-s1, c1], axis=1)], axis=0) * n1 ** -0.5
    k1 = np.arange(n1)[:, None, None]
    k2 = np.arange(n2)[None, :, None]
    m2 = np.arange(n2)[None, None, :]
    ang_2 = (2.0 * np.pi / s) * ((m2 * (k1 + n1 * k2)) % s)
    g2 = np.concatenate([np.cos(ang_2), np.sin(ang_2)], axis=2) * n2 ** -0.5
    return tuple(jnp.asarray(t.astype(np.float32)).astype(BF16) for t in (fc, f1, g2))


def _fourier_mix(f):
    b, _, s, _ = f.shape
    n1 = FFT_N1
    n2 = s // n1
    fc, f1, g2 = _fft_tables(s, n1, n2)
    gd = FOURIER_GROUP_DIM
    p1 = n2 + PITCH_PAD
    p2 = 2 * n1 + PITCH_PAD
    return pl.pallas_call(
        functools.partial(_fft_kernel, n1=n1, n2=n2),
        grid=(b, N_FOURIER_GROUPS),
        in_specs=[
            pl.BlockSpec((1, None, s, gd), lambda i, g: (i, g, 0, 0)),
            _resident((gd, 2 * gd)),
            _resident((2 * n1, 2 * n1)),
            _resident((n1, n2, 2 * n2)),
        ],
        out_specs=pl.BlockSpec((1, 1, n2 // SUBLANES, n1 * SUBLANES, gd), lambda i, g: (i, g, 0, 0, 0)),
        out_shape=jax.ShapeDtypeStruct((b, N_FOURIER_GROUPS, n2 // SUBLANES, n1 * SUBLANES, gd), F32),
        scratch_shapes=[
            pltpu.VMEM((2, n1 * p1, gd), F32),
            pltpu.VMEM((n2 * p2, gd), F32),
        ],
        compiler_params=pltpu.CompilerParams(
            dimension_semantics=("parallel", "parallel"), vmem_limit_bytes=VMEM_LIMIT_BYTES),
        name="fourier_mix",
    )(f, fc, f1, g2)


def _ret_kernel(lg_ref, q_ref, k_ref, v_ref, g_ref, o_ref,
                sf_ref, sb_ref, p_ref, fr_ref, br_ref, *, n_chunks):
    c = CHUNK
    dk2 = 2 * RET_QK_DIM
    dv = RET_V_DIM
    pair = pl.program_id(1)
    lgf = [lg_ref[0, 2 * pair + h] for h in range(2)]
    lgb = [lg_ref[1, 2 * pair + h] for h in range(2)]

    row = lax.broadcasted_iota(jnp.int32, (c, c), 0)
    col = lax.broadcasted_iota(jnp.int32, (c, c), 1)
    diff = (row - col).astype(F32)
    decay2 = jnp.concatenate(
        [jnp.where(diff >= 0.0,
                   jnp.exp(lgf[h] * jnp.maximum(diff, 0.0)),
                   jnp.exp(lgb[h] * jnp.maximum(-diff, 0.0))) for h in range(2)], axis=1)

    pos = lax.broadcasted_iota(jnp.int32, (c, dk2), 0).astype(F32)
    head_a = lax.broadcasted_iota(jnp.int32, (c, dk2), 1) < RET_QK_DIM

    def per_head(fa, fb):
        return jnp.where(head_a, fa, fb).astype(BF16)

    q_dec_f = per_head(jnp.exp(lgf[0] * (pos + 1.0)), jnp.exp(lgf[1] * (pos + 1.0)))
    q_dec_b = per_head(jnp.exp(lgb[0] * (c - pos)), jnp.exp(lgb[1] * (c - pos)))
    k_dec_f = per_head(jnp.exp(lgf[0] * (c - 1.0 - pos)), jnp.exp(lgf[1] * (c - 1.0 - pos)))
    k_dec_b = per_head(jnp.exp(lgb[0] * pos), jnp.exp(lgb[1] * pos))

    srow = lax.broadcasted_iota(jnp.int32, (dk2, 2 * dv), 0) < RET_QK_DIM
    scol = lax.broadcasted_iota(jnp.int32, (dk2, 2 * dv), 1) < dv
    block_diag = srow == scol
    chunk_f = jnp.where(scol, jnp.exp(lgf[0] * c), jnp.exp(lgf[1] * c))
    chunk_b = jnp.where(scol, jnp.exp(lgb[0] * c), jnp.exp(lgb[1] * c))

    def kv_update(state, k_dec, chunk_dec, kc, vc):
        kd = kc * k_dec
        kv = lax.dot_general(kd, vc, (((0,), (0,)), ((), ())), preferred_element_type=F32)
        return chunk_dec * state + jnp.where(block_diag, kv, 0.0)

    fr_ref[...] = jnp.zeros_like(fr_ref)
    br_ref[...] = jnp.zeros_like(br_ref)

    def scan_step(t, carry):
        for run_ref, out_ref, i, k_dec, chunk_dec in (
                (fr_ref, sf_ref, t, k_dec_f, chunk_f),
                (br_ref, sb_ref, n_chunks - 1 - t, k_dec_b, chunk_b)):
            r0 = pl.multiple_of(i * c, c)
            state = run_ref[...]
            out_ref[i] = state.astype(BF16)
            run_ref[...] = kv_update(state, k_dec, chunk_dec,
                                     k_ref[0, pl.ds(r0, c), :], v_ref[0, pl.ds(r0, c), :])
        r0 = pl.multiple_of(t * c, c)
        kc = k_ref[0, pl.ds(r0, c), :]
        zero = jnp.zeros_like(kc)
        k_heads = jnp.concatenate([jnp.where(head_a, kc, zero), jnp.where(head_a, zero, kc)], axis=0)
        scores = lax.dot_general(q_ref[0, pl.ds(r0, c), :], k_heads, (((1,), (1,)), ((), ())),
                                 preferred_element_type=F32)
        p_ref[t] = (scores * decay2).astype(BF16)
        return carry

    lax.fori_loop(0, n_chunks, scan_step, 0, unroll=64)

    def out_step(i, carry):
        r0 = pl.multiple_of(i * c, c)
        qc = q_ref[0, pl.ds(r0, c), :]
        vc = v_ref[0, pl.ds(r0, c), :]
        q_inter = jnp.concatenate([qc * q_dec_f, qc * q_dec_b], axis=1)
        states = jnp.concatenate([sf_ref[i], sb_ref[i]], axis=0)
        inter = jnp.dot(q_inter, states, preferred_element_type=F32)
        outs = []
        for h in range(2):
            o = jnp.dot(p_ref[i, :, h * c:(h + 1) * c], vc[:, h * dv:(h + 1) * dv],
                        preferred_element_type=F32)
            o = o + inter[:, h * dv:(h + 1) * dv]
            mu = jnp.mean(o, axis=-1, keepdims=True)
            d = o - mu
            var = jnp.mean(d * d, axis=-1, keepdims=True)
            outs.append(d * lax.rsqrt(var + GN_EPS))
        gate = g_ref[0, pl.ds(r0, c), :]
        o_ref[0, pl.ds(r0, c), :] = jnp.concatenate(outs, axis=1).astype(BF16) * (gate * jax.nn.sigmoid(gate))
        return carry

    lax.fori_loop(0, n_chunks, out_step, 0, unroll=8)


def _retention(q, k, v, g, log_gamma):
    b, _, s, _ = q.shape
    n_chunks = s // CHUNK
    n_pairs = N_RET_HEADS // 2
    dk2 = 2 * RET_QK_DIM
    dv2 = 2 * RET_V_DIM
    return pl.pallas_call(
        functools.partial(_ret_kernel, n_chunks=n_chunks),
        grid=(b, n_pairs),
        in_specs=[
            pl.BlockSpec(memory_space=pltpu.SMEM),
            pl.BlockSpec((1, None, s, dk2), lambda i, p: (i, p, 0, 0)),
            pl.BlockSpec((1, None, s, dk2), lambda i, p: (i, p, 0, 0)),
            pl.BlockSpec((1, None, s, dv2), lambda i, p: (i, p, 0, 0)),
            pl.BlockSpec((1, None, s, dv2), lambda i, p: (i, p, 0, 0)),
        ],
        out_specs=pl.BlockSpec((1, None, s, dv2), lambda i, p: (i, p, 0, 0)),
        out_shape=jax.ShapeDtypeStruct((b, n_pairs, s, dv2), BF16),
        scratch_shapes=[
            pltpu.VMEM((n_chunks, dk2, dv2), BF16),
            pltpu.VMEM((n_chunks, dk2, dv2), BF16),
            pltpu.VMEM((n_chunks, CHUNK, 2 * CHUNK), BF16),
            pltpu.VMEM((dk2, dv2), F32),
            pltpu.VMEM((dk2, dv2), F32),
        ],
        compiler_params=pltpu.CompilerParams(
            dimension_semantics=("parallel", "parallel"), vmem_limit_bytes=VMEM_LIMIT_BYTES),
        name="retention",
    )(log_gamma, q, k, v, g)


def _merge_kernel(x_ref, yf_ref, ret_ref, ga_ref, gb_ref, wf_ref, wr_ref, wo_ref, o_ref, m_ref):
    n_groups = yf_ref.shape[1]
    k2_tile = SUBLANES
    n1 = yf_ref.shape[2] // k2_tile
    yf = jnp.concatenate(
        [jnp.concatenate([yf_ref[0, g, pl.ds(k2l, n1, stride=k2_tile), :] for g in range(n_groups)], axis=1)
         for k2l in range(k2_tile)], axis=0)
    yf = yf.astype(BF16)
    ret = jnp.concatenate([ret_ref[0, p] for p in range(ret_ref.shape[1])], axis=1)
    for lo in range(0, m_ref.shape[1], MERGE_COLS):
        cols = slice(lo, lo + MERGE_COLS)
        a = jnp.dot(yf, wf_ref[:, cols], preferred_element_type=F32)
        bb = jnp.dot(ret, wr_ref[:, cols], preferred_element_type=F32)
        m_ref[:, cols] = (jax.nn.sigmoid(ga_ref[0, :, cols].astype(F32)) * a
                          + jax.nn.sigmoid(gb_ref[0, :, cols].astype(F32)) * bb).astype(BF16)
    o_ref[0] = x_ref[0] + jnp.dot(m_ref[...], wo_ref[...], preferred_element_type=F32)


def _merge(x, y_four, y_ret, g_a, g_b, wf, wr, wo):
    b, s, d = x.shape
    _, n_groups, _, tm, gd = y_four.shape

    def tile(w):
        return pl.BlockSpec((1, tm, w), lambda i, j: (i, j, 0))

    return pl.pallas_call(
        _merge_kernel,
        grid=(b, s // tm),
        in_specs=[tile(d),
                  pl.BlockSpec((1, n_groups, None, tm, gd), lambda i, j: (i, 0, j, 0, 0)),
                  pl.BlockSpec((1,) + y_ret.shape[1:2] + (tm,) + y_ret.shape[3:], lambda i, j: (i, 0, j, 0)),
                  tile(d), tile(d),
                  _resident(wf.shape), _resident(wr.shape), _resident(wo.shape)],
        out_specs=tile(d),
        out_shape=jax.ShapeDtypeStruct((b, s, d), F32),
        scratch_shapes=[pltpu.VMEM((tm, d), BF16)],
        compiler_params=pltpu.CompilerParams(
            dimension_semantics=("parallel", "parallel"), vmem_limit_bytes=VMEM_LIMIT_BYTES),
        name="merge",
    )(x, y_four, y_ret, g_a, g_b, wf, wr, wo)


def _ffn_kernel(xp_ref, x_ref, xn_ref, g2_ref, wu_ref, cw_ref, cb_ref, wd_ref, gf_ref, o_ref, act_ref):
    tm = x_ref.shape[1]
    halo = SUBLANES
    j = pl.program_id(1)
    last = pl.num_programs(1) - 1
    x = x_ref[0]
    xx = jnp.concatenate([x, xp_ref[0], xn_ref[0]], axis=0)
    u = _rms(xx, g2_ref[...]).astype(BF16)
    u_mid = u[0:tm]
    keep_top = (j > 0).astype(F32)
    keep_bot = (j < last).astype(F32)

    for ci in range(D_FF // FF_CHUNK):
        lo = ci * FF_CHUNK
        hg = jnp.dot(u, wu_ref[:, lo:lo + FF_CHUNK], preferred_element_type=F32)
        hv = jnp.dot(u_mid, wu_ref[:, D_FF + lo:D_FF + lo + FF_CHUNK], preferred_element_type=F32)
        mid = hg[0:tm]
        seq = jnp.concatenate([hg[tm:tm + halo] * keep_top, mid, hg[tm + halo:] * keep_bot], axis=0)
        cw = cw_ref[:, lo:lo + FF_CHUNK]
        prev = pltpu.roll(seq, 1, 0)[halo:halo + tm]
        nxt = pltpu.roll(seq, tm + 2 * halo - 1, 0)[halo:halo + tm]
        hc = prev * cw[0:1] + mid * cw[1:2] + nxt * cw[2:3] + cb_ref[:, lo:lo + FF_CHUNK]
        act = 0.5 * hc * (1.0 + lax.erf(hc * (2.0 ** -0.5)))
        act_ref[:, lo:lo + FF_CHUNK] = (act * hv).astype(BF16)
    for r in range(0, tm, FFN_OUT_ROWS):
        rows = slice(r, r + FFN_OUT_ROWS)
        y = jnp.dot(act_ref[rows, :], wd_ref[...], preferred_element_type=F32)
        o_ref[0, rows, :] = _rms(x[rows] + y, gf_ref[...])


def _ffn(x1, norm2_g, w_up, conv_w, conv_b, w_down, final_g):
    b, s, d = x1.shape
    tm = TOKEN_TILE
    hb = tm // SUBLANES
    n_hblocks = s // SUBLANES
    return pl.pallas_call(
        _ffn_kernel,
        grid=(b, s // tm),
        in_specs=[
            pl.BlockSpec((1, SUBLANES, d), lambda i, j: (i, jnp.maximum(j * hb - 1, 0), 0)),
            pl.BlockSpec((1, tm, d), lambda i, j: (i, j, 0)),
            pl.BlockSpec((1, SUBLANES, d), lambda i, j: (i, jnp.minimum((j + 1) * hb, n_hblocks - 1), 0)),
            _resident((1, d)),
            _resident(w_up.shape),
            _resident(conv_w.shape),
            _resident(conv_b.shape),
            _resident(w_down.shape),
            _resident((1, d)),
        ],
        out_specs=pl.BlockSpec((1, tm, d), lambda i, j: (i, j, 0)),
        out_shape=jax.ShapeDtypeStruct((b, s, d), F32),
        scratch_shapes=[pltpu.VMEM((tm, D_FF), BF16)],
        compiler_params=pltpu.CompilerParams(
            dimension_semantics=("parallel", "parallel"), vmem_limit_bytes=VMEM_LIMIT_BYTES),
        name="convglu_ffn",
    )(x1, x1, x1, norm2_g, w_up, conv_w, conv_b, w_down, final_g)


def _rotary_tables(s):
    inv = ROPE_THETA ** (-np.arange(0, RET_QK_DIM, 2, dtype=np.float64) / RET_QK_DIM)
    ang = np.arange(s, dtype=np.float64)[:, None] * inv[None, :]
    cos, sin = np.cos(ang), np.sin(ang)
    reps = LANES // RET_QK_DIM
    cos_tab = np.concatenate([cos, cos] * reps, axis=1).astype(np.float32)
    sin_tab = np.concatenate([-sin, sin] * reps, axis=1).astype(np.float32)
    assert cos_tab.shape == (s, LANES)
    return jnp.asarray(cos_tab), jnp.asarray(sin_tab)


def _encoder_layer(x, p):
    b, s, d = x.shape
    cos_tab, sin_tab = _rotary_tables(s)
    f, q, k, v, g_ret, g_a, g_b = _inproj(x, p["norm1_g"], p["w_in"], cos_tab, sin_tab)
    y_four = _fourier_mix(f)
    y_ret = _retention(q, k, v, g_ret, p["log_gamma"])
    return _merge(x, y_four, y_ret, g_a, g_b, p["w_four_proj"], p["w_ret_proj"], p["w_out"])


def kernel(x_prompt, x_sample, norm1_g, w_in, w_four_proj, w_ret_proj, w_out, ret_decay_logit,
           norm2_g, w_up, conv_w, conv_b, w_down, final_norm_g):
    depth = w_in.shape[0]
    assert depth == 1, "the final RMSNorm is fused into the (single) layer's FFN kernel"
    layers = []
    for l in range(depth):
        layers.append(dict(
            norm1_g=norm1_g[l][None, :],
            w_in=w_in[l].astype(BF16),
            w_four_proj=w_four_proj[l].astype(BF16),
            w_ret_proj=w_ret_proj[l].astype(BF16),
            w_out=w_out[l].astype(BF16),
            log_gamma=jax.nn.log_sigmoid(ret_decay_logit[l].astype(F32)),
            norm2_g=norm2_g[l][None, :],
            w_up=w_up[l].astype(BF16),
            conv_w=conv_w[l],
            conv_b=conv_b[l][None, :],
            w_down=w_down[l].astype(BF16),
        ))
    final_g = final_norm_g[None, :]

    def trunk(x):
        p = layers[0]
        x1 = _encoder_layer(x, p)
        return _ffn(x1, p["norm2_g"], p["w_up"], p["conv_w"], p["conv_b"], p["w_down"], final_g)

    return (trunk(x_prompt), trunk(x_sample))
```

```python
import functools

import numpy as np
import jax
import jax.numpy as jnp
from jax import lax
from jax.experimental import pallas as pl
from jax.experimental.pallas import tpu as pltpu

D_MODEL = 1024
N_FOURIER_GROUPS = 4
FOURIER_GROUP_DIM = 128
FOURIER_WIDTH = N_FOURIER_GROUPS * FOURIER_GROUP_DIM
N_RET_HEADS = 8
RET_QK_DIM = 64
RET_V_DIM = 128
RET_QK_WIDTH = N_RET_HEADS * RET_QK_DIM
RET_V_WIDTH = N_RET_HEADS * RET_V_DIM
CHUNK = 128
ROPE_THETA = 10000.0
D_FF = 2816
NORM_EPS = 1e-6
GN_EPS = 1e-5
IN_SPLITS = (FOURIER_WIDTH, RET_QK_WIDTH, RET_QK_WIDTH, RET_V_WIDTH, RET_V_WIDTH, D_MODEL, D_MODEL)
IN_OFFSETS = tuple(int(o) for o in np.cumsum((0,) + IN_SPLITS))
IN_WIDTH = IN_OFFSETS[-1]

LANES = 128
SUBLANES = 8
VMEM_LIMIT_BYTES = 56 * 1024 * 1024

F32 = jnp.float32
BF16 = jnp.bfloat16

TOKEN_TILE = 1024
INPROJ_ROWS = 512
FF_CHUNK = 256
FFN_OUT_ROWS = 256
FFT_N1 = 128
FFT_STAGE0_ROWS = 512
MERGE_COLS = 256
PITCH_PAD = SUBLANES


def _resident(shape):
    return pl.BlockSpec(shape, lambda *_: (0,) * len(shape), pipeline_mode=pl.Buffered(1))


def _rms(x, g):
    ms = jnp.mean(x * x, axis=-1, keepdims=True)
    return x * lax.rsqrt(ms + NORM_EPS) * g


def _inproj_kernel(x_ref, g_ref, w_ref, cos_ref, sin_ref,
                   f_ref, q_ref, k_ref, v_ref, gr_ref, ga_ref, gb_ref):
    reps = RET_QK_WIDTH // LANES
    lane = lax.broadcasted_iota(jnp.int32, (INPROJ_ROWS, RET_QK_WIDTH), 1)
    first_half = (lane % RET_QK_DIM) < (RET_QK_DIM // 2)

    for r in range(0, x_ref.shape[1], INPROJ_ROWS):
        rows = slice(r, r + INPROJ_ROWS)
        u = _rms(x_ref[0, rows, :], g_ref[...]).astype(BF16)

        def proj(i):
            return jnp.dot(u, w_ref[:, IN_OFFSETS[i]:IN_OFFSETS[i + 1]], preferred_element_type=F32)

        cos = jnp.concatenate([cos_ref[rows, :]] * reps, axis=1)
        sin = jnp.concatenate([sin_ref[rows, :]] * reps, axis=1)

        def rotary(t):
            partner = jnp.where(first_half,
                                pltpu.roll(t, RET_QK_WIDTH - RET_QK_DIM // 2, 1),
                                pltpu.roll(t, RET_QK_DIM // 2, 1))
            return t * cos + partner * sin

        def store_split(ref, val):
            parts, w = ref.shape[1], ref.shape[3]
            for p in range(parts):
                ref[0, p, rows, :] = val[:, p * w:(p + 1) * w].astype(BF16)

        store_split(f_ref, proj(0))
        store_split(q_ref, rotary(proj(1)) * (RET_QK_DIM ** -0.5))
        store_split(k_ref, rotary(proj(2)))
        store_split(v_ref, proj(3))
        store_split(gr_ref, proj(4))
        ga_ref[0, rows, :] = proj(5).astype(BF16)
        gb_ref[0, rows, :] = proj(6).astype(BF16)


def _inproj(x, norm_g, w_in_bf16, cos_tab, sin_tab):
    b, s, d = x.shape
    tm = TOKEN_TILE
    parts = (N_FOURIER_GROUPS,) + (N_RET_HEADS // 2,) * 4
    out_shape = [jax.ShapeDtypeStruct((b, p, s, w // p), BF16) for p, w in zip(parts, IN_SPLITS[:5])]
    out_specs = [pl.BlockSpec((1, p, tm, w // p), lambda i, j: (i, 0, j, 0))
                 for p, w in zip(parts, IN_SPLITS[:5])]
    out_shape += [jax.ShapeDtypeStruct((b, s, w), BF16) for w in IN_SPLITS[5:]]
    out_specs += [pl.BlockSpec((1, tm, w), lambda i, j: (i, j, 0)) for w in IN_SPLITS[5:]]
    return pl.pallas_call(
        _inproj_kernel,
        grid=(b, s // tm),
        in_specs=[
            pl.BlockSpec((1, tm, d), lambda i, j: (i, j, 0)),
            _resident((1, d)),
            _resident((d, IN_WIDTH)),
            pl.BlockSpec((tm, LANES), lambda i, j: (j, 0)),
            pl.BlockSpec((tm, LANES), lambda i, j: (j, 0)),
        ],
        out_specs=out_specs,
        out_shape=out_shape,
        compiler_params=pltpu.CompilerParams(
            dimension_semantics=("parallel", "parallel"), vmem_limit_bytes=VMEM_LIMIT_BYTES),
        name="inproj",
    )(x, norm_g, w_in_bf16, cos_tab, sin_tab)


def _fft_kernel(x_ref, fc_ref, f1_ref, g2_ref, o_ref, z_ref, y_ref, *, n1, n2):
    p1 = n2 + PITCH_PAD
    p2 = 2 * n1 + PITCH_PAD
    gd = FOURIER_GROUP_DIM

    slabs = FFT_STAGE0_ROWS // n2

    def stage0(i, carry):
        r0 = pl.multiple_of(i * FFT_STAGE0_ROWS, FFT_STAGE0_ROWS)
        z = jnp.dot(x_ref[0, pl.ds(r0, FFT_STAGE0_ROWS), :], fc_ref[...],
                    preferred_element_type=F32)
        for s in range(slabs):
            row = pl.multiple_of((i * slabs + s) * p1, SUBLANES)
            z_ref[0, pl.ds(row, n2), :] = z[s * n2:(s + 1) * n2, :gd]
            z_ref[1, pl.ds(row, n2), :] = z[s * n2:(s + 1) * n2, gd:]
        return carry

    lax.fori_loop(0, (n1 * n2) // FFT_STAGE0_ROWS, stage0, 0, unroll=16)

    def stage1(jj, carry):
        cols = []
        for t in range(2):
            j = 2 * jj + t
            zr = z_ref[0, pl.ds(j, n1, stride=p1), :]
            zi = z_ref[1, pl.ds(j, n1, stride=p1), :]
            cols.append(jnp.concatenate([zr, zi], axis=0).astype(BF16))
        rhs = jnp.concatenate(cols, axis=1)
        y = jnp.dot(f1_ref[...], rhs, preferred_element_type=F32)
        for t in range(2):
            row = pl.multiple_of((2 * jj + t) * p2, SUBLANES)
            y_ref[pl.ds(row, 2 * n1), :] = y[:, t * gd:(t + 1) * gd]
        return carry

    lax.fori_loop(0, n2 // 2, stage1, 0, unroll=32)

    def stage2(k1, carry):
        yr = y_ref[pl.ds(k1, n2, stride=p2), :]
        yi = y_ref[pl.ds(n1 + k1, n2, stride=p2), :]
        rhs = jnp.concatenate([yr, yi], axis=0).astype(BF16)
        out = jnp.dot(g2_ref[k1], rhs, preferred_element_type=F32)
        row = pl.multiple_of(k1 * SUBLANES, SUBLANES)
        o_ref[0, 0, :, pl.ds(row, SUBLANES), :] = out.reshape(n2 // SUBLANES, SUBLANES, gd)
        return carry

    lax.fori_loop(0, n1, stage2, 0, unroll=128)


def _fft_tables(s, n1, n2):
    c = FOURIER_GROUP_DIM
    ic = np.arange(c)
    ang_c = (2.0 * np.pi / c) * ((ic[:, None] * ic[None, :]) % c)
    fc = np.concatenate([np.cos(ang_c), -np.sin(ang_c)], axis=1) * c ** -0.5
    i1 = np.arange(n1)
    ang_1 = (2.0 * np.pi / n1) * ((i1[:, None] * i1[None, :]) % n1)
    c1, s1 = np.cos(ang_1), np.sin(ang_1)
    f1 = np.concatenate([np.concatenate([c1, s1], axis=1),
                         np.concatenate([-s1, c1], axis=1)], axis=0) * n1 ** -0.5
    k1 = np.arange(n1)[:, None, None]
    k2 = np.arange(n2)[None, :, None]
    m2 = np.arange(n2)[None, None, :]
    ang_2 = (2.0 * np.pi / s) * ((m2 * (k1 + n1 * k2)) % s)
    g2 = np.concatenate([np.cos(ang_2), np.sin(ang_2)], axis=2) * n2 ** -0.5
    return tuple(jnp.asarray(t.astype(np.float32)).astype(BF16) for t in (fc, f1, g2))


def _fourier_mix(f):
    b, _, s, _ = f.shape
    n1 = FFT_N1
    n2 = s // n1
    fc, f1, g2 = _fft_tables(s, n1, n2)
    gd = FOURIER_GROUP_DIM
    p1 = n2 + PITCH_PAD
    p2 = 2 * n1 + PITCH_PAD
    return pl.pallas_call(
        functools.partial(_fft_kernel, n1=n1, n2=n2),
        grid=(b, N_FOURIER_GROUPS),
        in_specs=[
            pl.BlockSpec((1, None, s, gd), lambda i, g: (i, g, 0, 0)),
            _resident((gd, 2 * gd)),
            _resident((2 * n1, 2 * n1)),
            _resident((n1, n2, 2 * n2)),
        ],
        out_specs=pl.BlockSpec((1, 1, n2 // SUBLANES, n1 * SUBLANES, gd), lambda i, g: (i, g, 0, 0, 0)),
        out_shape=jax.ShapeDtypeStruct((b, N_FOURIER_GROUPS, n2 // SUBLANES, n1 * SUBLANES, gd), F32),
        scratch_shapes=[
            pltpu.VMEM((2, n1 * p1, gd), F32),
            pltpu.VMEM((n2 * p2, gd), F32),
        ],
        compiler_params=pltpu.CompilerParams(
            dimension_semantics=("parallel", "parallel"), vmem_limit_bytes=VMEM_LIMIT_BYTES),
        name="fourier_mix",
    )(f, fc, f1, g2)


def _ret_kernel(lg_ref, q_ref, k_ref, v_ref, g_ref, o_ref,
                sf_ref, sb_ref, p_ref, fr_ref, br_ref, *, n_chunks):
    c = CHUNK
    dk2 = 2 * RET_QK_DIM
    dv = RET_V_DIM
    pair = pl.program_id(1)
    lgf = [lg_ref[0, 2 * pair + h] for h in range(2)]
    lgb = [lg_ref[1, 2 * pair + h] for h in range(2)]

    row = lax.broadcasted_iota(jnp.int32, (c, c), 0)
    col = lax.broadcasted_iota(jnp.int32, (c, c), 1)
    diff = (row - col).astype(F32)
    decay2 = jnp.concatenate(
        [jnp.where(diff >= 0.0,
                   jnp.exp(lgf[h] * jnp.maximum(diff, 0.0)),
                   jnp.exp(lgb[h] * jnp.maximum(-diff, 0.0))) for h in range(2)], axis=1)

    pos = lax.broadcasted_iota(jnp.int32, (c, dk2), 0).astype(F32)
    head_a = lax.broadcasted_iota(jnp.int32, (c, dk2), 1) < RET_QK_DIM

    def per_head(fa, fb):
        return jnp.where(head_a, fa, fb).astype(BF16)

    q_dec_f = per_head(jnp.exp(lgf[0] * (pos + 1.0)), jnp.exp(lgf[1] * (pos + 1.0)))
    q_dec_b = per_head(jnp.exp(lgb[0] * (c - pos)), jnp.exp(lgb[1] * (c - pos)))
    k_dec_f = per_head(jnp.exp(lgf[0] * (c - 1.0 - pos)), jnp.exp(lgf[1] * (c - 1.0 - pos)))
    k_dec_b = per_head(jnp.exp(lgb[0] * pos), jnp.exp(lgb[1] * pos))

    srow = lax.broadcasted_iota(jnp.int32, (dk2, 2 * dv), 0) < RET_QK_DIM
    scol = lax.broadcasted_iota(jnp.int32, (dk2, 2 * dv), 1) < dv
    block_diag = srow == scol
    chunk_f = jnp.where(scol, jnp.exp(lgf[0] * c), jnp.exp(lgf[1] * c))
    chunk_b = jnp.where(scol, jnp.exp(lgb[0] * c), jnp.exp(lgb[1] * c))

    def kv_update(state, k_dec, chunk_dec, kc, vc):
        kd = kc * k_dec
        kv = lax.dot_general(kd, vc, (((0,), (0,)), ((), ())), preferred_element_type=F32)
        return chunk_dec * state + jnp.where(block_diag, kv, 0.0)

    fr_ref[...] = jnp.zeros_like(fr_ref)
    br_ref[...] = jnp.zeros_like(br_ref)

    def scan_step(t, carry):
        for run_ref, out_ref, i, k_dec, chunk_dec in (
                (fr_ref, sf_ref, t, k_dec_f, chunk_f),
                (br_ref, sb_ref, n_chunks - 1 - t, k_dec_b, chunk_b)):
            r0 = pl.multiple_of(i * c, c)
            state = run_ref[...]
            out_ref[i] = state.astype(BF16)
            run_ref[...] = kv_update(state, k_dec, chunk_dec,
                                     k_ref[0, pl.ds(r0, c), :], v_ref[0, pl.ds(r0, c), :])
        r0 = pl.multiple_of(t * c, c)
        kc = k_ref[0, pl.ds(r0, c), :]
        zero = jnp.zeros_like(kc)
        k_heads = jnp.concatenate([jnp.where(head_a, kc, zero), jnp.where(head_a, zero, kc)], axis=0)
        scores = lax.dot_general(q_ref[0, pl.ds(r0, c), :], k_heads, (((1,), (1,)), ((), ())),
                                 preferred_element_type=F32)
        p_ref[t] = (scores * decay2).astype(BF16)
        return carry

    lax.fori_loop(0, n_chunks, scan_step, 0, unroll=64)

    def out_step(i, carry):
        r0 = pl.multiple_of(i * c, c)
        qc = q_ref[0, pl.ds(r0, c), :]
        vc = v_ref[0, pl.ds(r0, c), :]
        q_inter = jnp.concatenate([qc * q_dec_f, qc * q_dec_b], axis=1)
        states = jnp.concatenate([sf_ref[i], sb_ref[i]], axis=0)
        inter = jnp.dot(q_inter, states, preferred_element_type=F32)
        outs = []
        for h in range(2):
            o = jnp.dot(p_ref[i, :, h * c:(h + 1) * c], vc[:, h * dv:(h + 1) * dv],
                        preferred_element_type=F32)
            o = o + inter[:, h * dv:(h + 1) * dv]
            mu = jnp.mean(o, axis=-1, keepdims=True)
            d = o - mu
            var = jnp.mean(d * d, axis=-1, keepdims=True)
            outs.append(d * lax.rsqrt(var + GN_EPS))
        gate = g_ref[0, pl.ds(r0, c), :]
        o_ref[0, pl.ds(r0, c), :] = jnp.concatenate(outs, axis=1).astype(BF16) * (gate * jax.nn.sigmoid(gate))
        return carry

    lax.fori_loop(0, n_chunks, out_step, 0, unroll=8)


def _retention(q, k, v, g, log_gamma):
    b, _, s, _ = q.shape
    n_chunks = s // CHUNK
    n_pairs = N_RET_HEADS // 2
    dk2 = 2 * RET_QK_DIM
    dv2 = 2 * RET_V_DIM
    return pl.pallas_call(
        functools.partial(_ret_kernel, n_chunks=n_chunks),
        grid=(b, n_pairs),
        in_specs=[
            pl.BlockSpec(memory_space=pltpu.SMEM),
            pl.BlockSpec((1, None, s, dk2), lambda i, p: (i, p, 0, 0)),
            pl.BlockSpec((1, None, s, dk2), lambda i, p: (i, p, 0, 0)),
            pl.BlockSpec((1, None, s, dv2), lambda i, p: (i, p, 0, 0)),
            pl.BlockSpec((1, None, s, dv2), lambda i, p: (i, p, 0, 0)),
        ],
        out_specs=pl.BlockSpec((1, None, s, dv2), lambda i, p: (i, p, 0, 0)),
        out_shape=jax.ShapeDtypeStruct((b, n_pairs, s, dv2), BF16),
        scratch_shapes=[
            pltpu.VMEM((n_chunks, dk2, dv2), BF16),
            pltpu.VMEM((n_chunks, dk2, dv2), BF16),
            pltpu.VMEM((n_chunks, CHUNK, 2 * CHUNK), BF16),
            pltpu.VMEM((dk2, dv2), F32),
            pltpu.VMEM((dk2, dv2), F32),
        ],
        compiler_params=pltpu.CompilerParams(
            dimension_semantics=("parallel", "parallel"), vmem_limit_bytes=VMEM_LIMIT_BYTES),
        name="retention",
    )(log_gamma, q, k, v, g)


def _merge_kernel(x_ref, yf_ref, ret_ref, ga_ref, gb_ref, wf_ref, wr_ref, wo_ref, o_ref, m_ref):
    n_groups = yf_ref.shape[1]
    k2_tile = SUBLANES
    n1 = yf_ref.shape[2] // k2_tile
    yf = jnp.concatenate(
        [jnp.concatenate([yf_ref[0, g, pl.ds(k2l, n1, stride=k2_tile), :] for g in range(n_groups)], axis=1)
         for k2l in range(k2_tile)], axis=0)
    yf = yf.astype(BF16)
    ret = jnp.concatenate([ret_ref[0, p] for p in range(ret_ref.shape[1])], axis=1)
    for lo in range(0, m_ref.shape[1], MERGE_COLS):
        cols = slice(lo, lo + MERGE_COLS)
        a = jnp.dot(yf, wf_ref[:, cols], preferred_element_type=F32)
        bb = jnp.dot(ret, wr_ref[:, cols], preferred_element_type=F32)
        m_ref[:, cols] = (jax.nn.sigmoid(ga_ref[0, :, cols].astype(F32)) * a
                          + jax.nn.sigmoid(gb_ref[0, :, cols].astype(F32)) * bb).astype(BF16)
    o_ref[0] = x_ref[0] + jnp.dot(m_ref[...], wo_ref[...], preferred_element_type=F32)


def _merge(x, y_four, y_ret, g_a, g_b, wf, wr, wo):
    b, s, d = x.shape
    _, n_groups, _, tm, gd = y_four.shape

    def tile(w):
        return pl.BlockSpec((1, tm, w), lambda i, j: (i, j, 0))

    return pl.pallas_call(
        _merge_kernel,
        grid=(b, s // tm),
        in_specs=[tile(d),
                  pl.BlockSpec((1, n_groups, None, tm, gd), lambda i, j: (i, 0, j, 0, 0)),
                  pl.BlockSpec((1,) + y_ret.shape[1:2] + (tm,) + y_ret.shape[3:], lambda i, j: (i, 0, j, 0)),
                  tile(d), tile(d),
                  _resident(wf.shape), _resident(wr.shape), _resident(wo.shape)],
        out_specs=tile(d),
        out_shape=jax.ShapeDtypeStruct((b, s, d), F32),
        scratch_shapes=[pltpu.VMEM((tm, d), BF16)],
        compiler_params=pltpu.CompilerParams(
            dimension_semantics=("parallel", "parallel"), vmem_limit_bytes=VMEM_LIMIT_BYTES),
        name="merge",
    )(x, y_four, y_ret, g_a, g_b, wf, wr, wo)


def _ffn_kernel(xp_ref, x_ref, xn_ref, g2_ref, wu_ref, cw_ref, cb_ref, wd_ref, gf_ref, o_ref, act_ref):
    tm = x_ref.shape[1]
    halo = SUBLANES
    j = pl.program_id(1)
    last = pl.num_programs(1) - 1
    x = x_ref[0]
    xx = jnp.concatenate([x, xp_ref[0], xn_ref[0]], axis=0)
    u = _rms(xx, g2_ref[...]).astype(BF16)
    u_mid = u[0:tm]
    keep_top = (j > 0).astype(F32)
    keep_bot = (j < last).astype(F32)

    for ci in range(D_FF // FF_CHUNK):
        lo = ci * FF_CHUNK
        hg = jnp.dot(u, wu_ref[:, lo:lo + FF_CHUNK], preferred_element_type=F32)
        hv = jnp.dot(u_mid, wu_ref[:, D_FF + lo:D_FF + lo + FF_CHUNK], preferred_element_type=F32)
        mid = hg[0:tm]
        seq = jnp.concatenate([hg[tm:tm + halo] * keep_top, mid, hg[tm + halo:] * keep_bot], axis=0)
        cw = cw_ref[:, lo:lo + FF_CHUNK]
        prev = pltpu.roll(seq, 1, 0)[halo:halo + tm]
        nxt = pltpu.roll(seq, tm + 2 * halo - 1, 0)[halo:halo + tm]
        hc = prev * cw[0:1] + mid * cw[1:2] + nxt * cw[2:3] + cb_ref[:, lo:lo + FF_CHUNK]
        act = 0.5 * hc * (1.0 + lax.erf(hc * (2.0 ** -0.5)))
        act_ref[:, lo:lo + FF_CHUNK] = (act * hv).astype(BF16)
    for r in range(0, tm, FFN_OUT_ROWS):
        rows = slice(r, r + FFN_OUT_ROWS)
        y = jnp.dot(act_ref[rows, :], wd_ref[...], preferred_element_type=F32)
        o_ref[0, rows, :] = _rms(x[rows] + y, gf_ref[...])


def _ffn(x1, norm2_g, w_up, conv_w, conv_b, w_down, final_g):
    b, s, d = x1.shape
    tm = TOKEN_TILE
    hb = tm // SUBLANES
    n_hblocks = s // SUBLANES
    return pl.pallas_call(
        _ffn_kernel,
        grid=(b, s // tm),
        in_specs=[
            pl.BlockSpec((1, SUBLANES, d), lambda i, j: (i, jnp.maximum(j * hb - 1, 0), 0)),
            pl.BlockSpec((1, tm, d), lambda i, j: (i, j, 0)),
            pl.BlockSpec((1, SUBLANES, d), lambda i, j: (i, jnp.minimum((j + 1) * hb, n_hblocks - 1), 0)),
            _resident((1, d)),
            _resident(w_up.shape),
            _resident(conv_w.shape),
            _resident(conv_b.shape),
            _resident(w_down.shape),
            _resident((1, d)),
        ],
        out_specs=pl.BlockSpec((1, tm, d), lambda i, j: (i, j, 0)),
        out_shape=jax.ShapeDtypeStruct((b, s, d), F32),
        scratch_shapes=[pltpu.VMEM((tm, D_FF), BF16)],
        compiler_params=pltpu.CompilerParams(
            dimension_semantics=("parallel", "parallel"), vmem_limit_bytes=VMEM_LIMIT_BYTES),
        name="convglu_ffn",
    )(x1, x1, x1, norm2_g, w_up, conv_w, conv_b, w_down, final_g)


def _rotary_tables(s):
    inv = ROPE_THETA ** (-np.arange(0, RET_QK_DIM, 2, dtype=np.float64) / RET_QK_DIM)
    ang = np.arange(s, dtype=np.float64)[:, None] * inv[None, :]
    cos, sin = np.cos(ang), np.sin(ang)
    reps = LANES // RET_QK_DIM
    cos_tab = np.concatenate([cos, cos] * reps, axis=1).astype(np.float32)
    sin_tab = np.concatenate([-sin, sin] * reps, axis=1).astype(np.float32)
    assert cos_tab.shape == (s, LANES)
    return jnp.asarray(cos_tab), jnp.asarray(sin_tab)


def _encoder_layer(x, p):
    b, s, d = x.shape
    cos_tab, sin_tab = _rotary_tables(s)
    f, q, k, v, g_ret, g_a, g_b = _inproj(x, p["norm1_g"], p["w_in"], cos_tab, sin_tab)
    y_four = _fourier_mix(f)
    y_ret = _retention(q, k, v, g_ret, p["log_gamma"])
    return _merge(x, y_four, y_ret, g_a, g_b, p["w_four_proj"], p["w_ret_proj"], p["w_out"])


def kernel(x_prompt, x_sample, norm1_g, w_in, w_four_proj, w_ret_proj, w_out, ret_decay_logit,
           norm2_g, w_up, conv_w, conv_b, w_down, final_norm_g):
    depth = w_in.shape[0]
    assert depth == 1, "the final RMSNorm is fused into the (single) layer's FFN kernel"
    layers = []
    for l in range(depth):
        layers.append(dict(
            norm1_g=norm1_g[l][None, :],
            w_in=w_in[l].astype(BF16),
            w_four_proj=w_four_proj[l].astype(BF16),
            w_ret_proj=w_ret_proj[l].astype(BF16),
            w_out=w_out[l].astype(BF16),
            log_gamma=jax.nn.log_sigmoid(ret_decay_logit[l].astype(F32)),
            norm2_g=norm2_g[l][None, :],
            w_up=w_up[l].astype(BF16),
            conv_w=conv_w[l],
            conv_b=conv_b[l][None, :],
            w_down=w_down[l].astype(BF16),
        ))
    final_g = final_norm_g[None, :]

    def trunk(x):
        p = layers[0]
        x1 = _encoder_layer(x, p)
        return _ffn(x1, p["norm2_g"], p["w_up"], p["conv_w"], p["conv_b"], p["w_down"], final_g)

    return (trunk(x_prompt), trunk(x_sample))
```

```python
import functools

import numpy as np
import jax
import jax.numpy as jnp
from jax import lax
from jax.experimental import pallas as pl
from jax.experimental.pallas import tpu as pltpu

D_MODEL = 1024
N_FOURIER_GROUPS = 4
FOURIER_GROUP_DIM = 128
FOURIER_WIDTH = N_FOURIER_GROUPS * FOURIER_GROUP_DIM
N_RET_HEADS = 8
RET_QK_DIM = 64
RET_V_DIM = 128
RET_QK_WIDTH = N_RET_HEADS * RET_QK_DIM
RET_V_WIDTH = N_RET_HEADS * RET_V_DIM
CHUNK = 128
ROPE_THETA = 10000.0
D_FF = 2816
NORM_EPS = 1e-6
GN_EPS = 1e-5
IN_SPLITS = (FOURIER_WIDTH, RET_QK_WIDTH, RET_QK_WIDTH, RET_V_WIDTH, RET_V_WIDTH, D_MODEL, D_MODEL)
IN_OFFSETS = tuple(int(o) for o in np.cumsum((0,) + IN_SPLITS))
IN_WIDTH = IN_OFFSETS[-1]

LANES = 128
SUBLANES = 8
VMEM_LIMIT_BYTES = 56 * 1024 * 1024

F32 = jnp.float32
BF16 = jnp.bfloat16

TOKEN_TILE = 1024
INPROJ_ROWS = 512
FF_CHUNK = 256
FFN_OUT_ROWS = 256
FFT_N1 = 128
FFT_STAGE0_ROWS = 512
MERGE_COLS = 256
PITCH_PAD = SUBLANES


def _resident(shape):
    return pl.BlockSpec(shape, lambda *_: (0,) * len(shape), pipeline_mode=pl.Buffered(1))


def _rms(x, g):
    ms = jnp.mean(x * x, axis=-1, keepdims=True)
    return x * lax.rsqrt(ms + NORM_EPS) * g


def _inproj_kernel(x_ref, g_ref, w_ref, cos_ref, sin_ref,
                   f_ref, q_ref, k_ref, v_ref, gr_ref, ga_ref, gb_ref):
    reps = RET_QK_WIDTH // LANES
    lane = lax.broadcasted_iota(jnp.int32, (INPROJ_ROWS, RET_QK_WIDTH), 1)
    first_half = (lane % RET_QK_DIM) < (RET_QK_DIM // 2)

    for r in range(0, x_ref.shape[1], INPROJ_ROWS):
        rows = slice(r, r + INPROJ_ROWS)
        u = _rms(x_ref[0, rows, :], g_ref[...]).astype(BF16)

        def proj(i):
            return jnp.dot(u, w_ref[:, IN_OFFSETS[i]:IN_OFFSETS[i + 1]], preferred_element_type=F32)

        cos = jnp.concatenate([cos_ref[rows, :]] * reps, axis=1)
        sin = jnp.concatenate([sin_ref[rows, :]] * reps, axis=1)

        def rotary(t):
            partner = jnp.where(first_half,
                                pltpu.roll(t, RET_QK_WIDTH - RET_QK_DIM // 2, 1),
                                pltpu.roll(t, RET_QK_DIM // 2, 1))
            return t * cos + partner * sin

        def store_split(ref, val):
            parts, w = ref.shape[1], ref.shape[3]
            for p in range(parts):
                ref[0, p, rows, :] = val[:, p * w:(p + 1) * w].astype(BF16)

        store_split(f_ref, proj(0))
        store_split(q_ref, rotary(proj(1)) * (RET_QK_DIM ** -0.5))
        store_split(k_ref, rotary(proj(2)))
        store_split(v_ref, proj(3))
        store_split(gr_ref, proj(4))
        ga_ref[0, rows, :] = proj(5).astype(BF16)
        gb_ref[0, rows, :] = proj(6).astype(BF16)


def _inproj(x, norm_g, w_in_bf16, cos_tab, sin_tab):
    b, s, d = x.shape
    tm = TOKEN_TILE
    parts = (N_FOURIER_GROUPS,) + (N_RET_HEADS // 2,) * 4
    out_shape = [jax.ShapeDtypeStruct((b, p, s, w // p), BF16) for p, w in zip(parts, IN_SPLITS[:5])]
    out_specs = [pl.BlockSpec((1, p, tm, w // p), lambda i, j: (i, 0, j, 0))
                 for p, w in zip(parts, IN_SPLITS[:5])]
    out_shape += [jax.ShapeDtypeStruct((b, s, w), BF16) for w in IN_SPLITS[5:]]
    out_specs += [pl.BlockSpec((1, tm, w), lambda i, j: (i, j, 0)) for w in IN_SPLITS[5:]]
    return pl.pallas_call(
        _inproj_kernel,
        grid=(b, s // tm),
        in_specs=[
            pl.BlockSpec((1, tm, d), lambda i, j: (i, j, 0)),
            _resident((1, d)),
            _resident((d, IN_WIDTH)),
            pl.BlockSpec((tm, LANES), lambda i, j: (j, 0)),
            pl.BlockSpec((tm, LANES), lambda i, j: (j, 0)),
        ],
        out_specs=out_specs,
        out_shape=out_shape,
        compiler_params=pltpu.CompilerParams(
            dimension_semantics=("parallel", "parallel"), vmem_limit_bytes=VMEM_LIMIT_BYTES),
        name="inproj",
    )(x, norm_g, w_in_bf16, cos_tab, sin_tab)


def _fft_kernel(x_ref, fc_ref, f1_ref, g2_ref, o_ref, z_ref, y_ref, *, n1, n2):
    p1 = n2 + PITCH_PAD
    p2 = 2 * n1 + PITCH_PAD
    gd = FOURIER_GROUP_DIM

    slabs = FFT_STAGE0_ROWS // n2

    def stage0(i, carry):
        r0 = pl.multiple_of(i * FFT_STAGE0_ROWS, FFT_STAGE0_ROWS)
        z = jnp.dot(x_ref[0, pl.ds(r0, FFT_STAGE0_ROWS), :], fc_ref[...],
                    preferred_element_type=F32)
        for s in range(slabs):
            row = pl.multiple_of((i * slabs + s) * p1, SUBLANES)
            z_ref[0, pl.ds(row, n2), :] = z[s * n2:(s + 1) * n2, :gd]
            z_ref[1, pl.ds(row, n2), :] = z[s * n2:(s + 1) * n2, gd:]
        return carry

    lax.fori_loop(0, (n1 * n2) // FFT_STAGE0_ROWS, stage0, 0, unroll=16)

    def stage1(jj, carry):
        cols = []
        for t in range(2):
            j = 2 * jj + t
            zr = z_ref[0, pl.ds(j, n1, stride=p1), :]
            zi = z_ref[1, pl.ds(j, n1, stride=p1), :]
            cols.append(jnp.concatenate([zr, zi], axis=0).astype(BF16))
        rhs = jnp.concatenate(cols, axis=1)
        y = jnp.dot(f1_ref[...], rhs, preferred_element_type=F32)
        for t in range(2):
            row = pl.multiple_of((2 * jj + t) * p2, SUBLANES)
            y_ref[pl.ds(row, 2 * n1), :] = y[:, t * gd:(t + 1) * gd]
        return carry

    lax.fori_loop(0, n2 // 2, stage1, 0, unroll=32)

    def stage2(k1, carry):
        yr = y_ref[pl.ds(k1, n2, stride=p2), :]
        yi = y_ref[pl.ds(n1 + k1, n2, stride=p2), :]
        rhs = jnp.concatenate([yr, yi], axis=0).astype(BF16)
        out = jnp.dot(g2_ref[k1], rhs, preferred_element_type=F32)
        row = pl.multiple_of(k1 * SUBLANES, SUBLANES)
        o_ref[0, 0, :, pl.ds(row, SUBLANES), :] = out.reshape(n2 // SUBLANES, SUBLANES, gd)
        return carry

    lax.fori_loop(0, n1, stage2, 0, unroll=128)


def _fft_tables(s, n1, n2):
    c = FOURIER_GROUP_DIM
    ic = np.arange(c)
    ang_c = (2.0 * np.pi / c) * ((ic[:, None] * ic[None, :]) % c)
    fc = np.concatenate([np.cos(ang_c), -np.sin(ang_c)], axis=1) * c ** -0.5
    i1 = np.arange(n1)
    ang_1 = (2.0 * np.pi / n1) * ((i1[:, None] * i1[None, :]) % n1)
    c1, s1 = np.cos(ang_1), np.sin(ang_1)
    f1 = np.concatenate([np.concatenate([c1, s1], axis=1),
                         np.concatenate([-s1, c1], axis=1)], axis=0) * n1 ** -0.5
    k1 = np.arange(n1)[:, None, None]
    k2 = np.arange(n2)[None, :, None]
    m2 = np.arange(n2)[None, None, :]
    ang_2 = (2.0 * np.pi / s) * ((m2 * (k1 + n1 * k2)) % s)
    g2 = np.concatenate([np.cos(ang_2), np.sin(ang_2)], axis=2) * n2 ** -0.5
    return tuple(jnp.asarray(t.astype(np.float32)).astype(BF16) for t in (fc, f1, g2))


def _fourier_mix(f):
    b, _, s, _ = f.shape
    n1 = FFT_N1
    n2 = s // n1
    fc, f1, g2 = _fft_tables(s, n1, n2)
    gd = FOURIER_GROUP_DIM
    p1 = n2 + PITCH_PAD
    p2 = 2 * n1 + PITCH_PAD
    return pl.pallas_call(
        functools.partial(_fft_kernel, n1=n1, n2=n2),
        grid=(b, N_FOURIER_GROUPS),
        in_specs=[
            pl.BlockSpec((1, None, s, gd), lambda i, g: (i, g, 0, 0)),
            _resident((gd, 2 * gd)),
            _resident((2 * n1, 2 * n1)),
            _resident((n1, n2, 2 * n2)),
        ],
        out_specs=pl.BlockSpec((1, 1, n2 // SUBLANES, n1 * SUBLANES, gd), lambda i, g: (i, g, 0, 0, 0)),
        out_shape=jax.ShapeDtypeStruct((b, N_FOURIER_GROUPS, n2 // SUBLANES, n1 * SUBLANES, gd), F32),
        scratch_shapes=[
            pltpu.VMEM((2, n1 * p1, gd), F32),
            pltpu.VMEM((n2 * p2, gd), F32),
        ],
        compiler_params=pltpu.CompilerParams(
            dimension_semantics=("parallel", "parallel"), vmem_limit_bytes=VMEM_LIMIT_BYTES),
        name="fourier_mix",
    )(f, fc, f1, g2)


def _ret_kernel(lg_ref, q_ref, k_ref, v_ref, g_ref, o_ref,
                sf_ref, sb_ref, p_ref, fr_ref, br_ref, *, n_chunks):
    c = CHUNK
    dk2 = 2 * RET_QK_DIM
    dv = RET_V_DIM
    pair = pl.program_id(1)
    lgf = [lg_ref[0, 2 * pair + h] for h in range(2)]
    lgb = [lg_ref[1, 2 * pair + h] for h in range(2)]

    row = lax.broadcasted_iota(jnp.int32, (c, c), 0)
    col = lax.broadcasted_iota(jnp.int32, (c, c), 1)
    diff = (row - col).astype(F32)
    decay2 = jnp.concatenate(
        [jnp.where(diff >= 0.0,
                   jnp.exp(lgf[h] * jnp.maximum(diff, 0.0)),
                   jnp.exp(lgb[h] * jnp.maximum(-diff, 0.0))) for h in range(2)], axis=1)

    pos = lax.broadcasted_iota(jnp.int32, (c, dk2), 0).astype(F32)
    head_a = lax.broadcasted_iota(jnp.int32, (c, dk2), 1) < RET_QK_DIM

    def per_head(fa, fb):
        return jnp.where(head_a, fa, fb).astype(BF16)

    q_dec_f = per_head(jnp.exp(lgf[0] * (pos + 1.0)), jnp.exp(lgf[1] * (pos + 1.0)))
    q_dec_b = per_head(jnp.exp(lgb[0] * (c - pos)), jnp.exp(lgb[1] * (c - pos)))
    k_dec_f = per_head(jnp.exp(lgf[0] * (c - 1.0 - pos)), jnp.exp(lgf[1] * (c - 1.0 - pos)))
    k_dec_b = per_head(jnp.exp(lgb[0] * pos), jnp.exp(lgb[1] * pos))

    srow = lax.broadcasted_iota(jnp.int32, (dk2, 2 * dv), 0) < RET_QK_DIM
    scol = lax.broadcasted_iota(jnp.int32, (dk2, 2 * dv), 1) < dv
    block_diag = srow == scol
    chunk_f = jnp.where(scol, jnp.exp(lgf[0] * c), jnp.exp(lgf[1] * c))
    chunk_b = jnp.where(scol, jnp.exp(lgb[0] * c), jnp.exp(lgb[1] * c))

    def kv_update(state, k_dec, chunk_dec, kc, vc):
        kd = kc * k_dec
        kv = lax.dot_general(kd, vc, (((0,), (0,)), ((), ())), preferred_element_type=F32)
        return chunk_dec * state + jnp.where(block_diag, kv, 0.0)

    fr_ref[...] = jnp.zeros_like(fr_ref)
    br_ref[...] = jnp.zeros_like(br_ref)

    def scan_step(t, carry):
        for run_ref, out_ref, i, k_dec, chunk_dec in (
                (fr_ref, sf_ref, t, k_dec_f, chunk_f),
                (br_ref, sb_ref, n_chunks - 1 - t, k_dec_b, chunk_b)):
            r0 = pl.multiple_of(i * c, c)
            state = run_ref[...]
            out_ref[i] = state.astype(BF16)
            run_ref[...] = kv_update(state, k_dec, chunk_dec,
                                     k_ref[0, pl.ds(r0, c), :], v_ref[0, pl.ds(r0, c), :])
        r0 = pl.multiple_of(t * c, c)
        kc = k_ref[0, pl.ds(r0, c), :]
        zero = jnp.zeros_like(kc)
        k_heads = jnp.concatenate([jnp.where(head_a, kc, zero), jnp.where(head_a, zero, kc)], axis=0)
        scores = lax.dot_general(q_ref[0, pl.ds(r0, c), :], k_heads, (((1,), (1,)), ((), ())),
                                 preferred_element_type=F32)
        p_ref[t] = (scores * decay2).astype(BF16)
        return carry

    lax.fori_loop(0, n_chunks, scan_step, 0, unroll=64)

    def out_step(i, carry):
        r0 = pl.multiple_of(i * c, c)
        qc = q_ref[0, pl.ds(r0, c), :]
        vc = v_ref[0, pl.ds(r0, c), :]
        q_inter = jnp.concatenate([qc * q_dec_f, qc * q_dec_b], axis=1)
        states = jnp.concatenate([sf_ref[i], sb_ref[i]], axis=0)
        inter = jnp.dot(q_inter, states, preferred_element_type=F32)
        outs = []
        for h in range(2):
            o = jnp.dot(p_ref[i, :, h * c:(h + 1) * c], vc[:, h * dv:(h + 1) * dv],
                        preferred_element_type=F32)
            o = o + inter[:, h * dv:(h + 1) * dv]
            mu = jnp.mean(o, axis=-1, keepdims=True)
            d = o - mu
            var = jnp.mean(d * d, axis=-1, keepdims=True)
            outs.append(d * lax.rsqrt(var + GN_EPS))
        gate = g_ref[0, pl.ds(r0, c), :]
        o_ref[0, pl.ds(r0, c), :] = jnp.concatenate(outs, axis=1).astype(BF16) * (gate * jax.nn.sigmoid(gate))
        return carry

    lax.fori_loop(0, n_chunks, out_step, 0, unroll=8)


def _retention(q, k, v, g, log_gamma):
    b, _, s, _ = q.shape
    n_chunks = s // CHUNK
    n_pairs = N_RET_HEADS // 2
    dk2 = 2 * RET_QK_DIM
    dv2 = 2 * RET_V_DIM
    return pl.pallas_call(
        functools.partial(_ret_kernel, n_chunks=n_chunks),
        grid=(b, n_pairs),
        in_specs=[
            pl.BlockSpec(memory_space=pltpu.SMEM),
            pl.BlockSpec((1, None, s, dk2), lambda i, p: (i, p, 0, 0)),
            pl.BlockSpec((1, None, s, dk2), lambda i, p: (i, p, 0, 0)),
            pl.BlockSpec((1, None, s, dv2), lambda i, p: (i, p, 0, 0)),
            pl.BlockSpec((1, None, s, dv2), lambda i, p: (i, p, 0, 0)),
        ],
        out_specs=pl.BlockSpec((1, None, s, dv2), lambda i, p: (i, p, 0, 0)),
        out_shape=jax.ShapeDtypeStruct((b, n_pairs, s, dv2), BF16),
        scratch_shapes=[
            pltpu.VMEM((n_chunks, dk2, dv2), BF16),
            pltpu.VMEM((n_chunks, dk2, dv2), BF16),
            pltpu.VMEM((n_chunks, CHUNK, 2 * CHUNK), BF16),
            pltpu.VMEM((dk2, dv2), F32),
            pltpu.VMEM((dk2, dv2), F32),
        ],
        compiler_params=pltpu.CompilerParams(
            dimension_semantics=("parallel", "parallel"), vmem_limit_bytes=VMEM_LIMIT_BYTES),
        name="retention",
    )(log_gamma, q, k, v, g)


def _merge_kernel(x_ref, yf_ref, ret_ref, ga_ref, gb_ref, wf_ref, wr_ref, wo_ref, o_ref, m_ref):
    n_groups = yf_ref.shape[1]
    k2_tile = SUBLANES
    n1 = yf_ref.shape[2] // k2_tile
    half = k2_tile // 2
    for k2_lo in range(0, k2_tile, half):
        rows = slice(k2_lo * n1, (k2_lo + half) * n1)
        yf = jnp.concatenate(
            [jnp.concatenate([yf_ref[0, g, pl.ds(k2l, n1, stride=k2_tile), :] for g in range(n_groups)], axis=1)
             for k2l in range(k2_lo, k2_lo + half)], axis=0).astype(BF16)
        ret = jnp.concatenate([ret_ref[0, p, rows, :] for p in range(ret_ref.shape[1])], axis=1)
        for lo in range(0, m_ref.shape[1], MERGE_COLS):
            cols = slice(lo, lo + MERGE_COLS)
            a = jnp.dot(yf, wf_ref[:, cols], preferred_element_type=F32)
            bb = jnp.dot(ret, wr_ref[:, cols], preferred_element_type=F32)
            m_ref[rows, cols] = (jax.nn.sigmoid(ga_ref[0, rows, cols].astype(F32)) * a
                                 + jax.nn.sigmoid(gb_ref[0, rows, cols].astype(F32)) * bb).astype(BF16)
        o_ref[0, rows, :] = x_ref[0, rows, :] + jnp.dot(m_ref[rows, :], wo_ref[...], preferred_element_type=F32)


def _merge(x, y_four, y_ret, g_a, g_b, wf, wr, wo):
    b, s, d = x.shape
    _, n_groups, _, tm, gd = y_four.shape

    def tile(w):
        return pl.BlockSpec((1, tm, w), lambda i, j: (i, j, 0))

    return pl.pallas_call(
        _merge_kernel,
        grid=(b, s // tm),
        in_specs=[tile(d),
                  pl.BlockSpec((1, n_groups, None, tm, gd), lambda i, j: (i, 0, j, 0, 0)),
                  pl.BlockSpec((1,) + y_ret.shape[1:2] + (tm,) + y_ret.shape[3:], lambda i, j: (i, 0, j, 0)),
                  tile(d), tile(d),
                  _resident(wf.shape), _resident(wr.shape), _resident(wo.shape)],
        out_specs=tile(d),
        out_shape=jax.ShapeDtypeStruct((b, s, d), F32),
        scratch_shapes=[pltpu.VMEM((tm, d), BF16)],
        compiler_params=pltpu.CompilerParams(
            dimension_semantics=("parallel", "parallel"), vmem_limit_bytes=VMEM_LIMIT_BYTES),
        name="merge",
    )(x, y_four, y_ret, g_a, g_b, wf, wr, wo)


def _ffn_kernel(xp_ref, x_ref, xn_ref, g2_ref, wu_ref, cw_ref, cb_ref, wd_ref, gf_ref, o_ref, act_ref):
    tm = x_ref.shape[1]
    halo = SUBLANES
    j = pl.program_id(1)
    last = pl.num_programs(1) - 1
    x = x_ref[0]
    xx = jnp.concatenate([x, xp_ref[0], xn_ref[0]], axis=0)
    u = _rms(xx, g2_ref[...]).astype(BF16)
    u_mid = u[0:tm]
    keep_top = (j > 0).astype(F32)
    keep_bot = (j < last).astype(F32)

    for ci in range(D_FF // FF_CHUNK):
        lo = ci * FF_CHUNK
        hg = jnp.dot(u, wu_ref[:, lo:lo + FF_CHUNK], preferred_element_type=F32)
        hv = jnp.dot(u_mid, wu_ref[:, D_FF + lo:D_FF + lo + FF_CHUNK], preferred_element_type=F32)
        mid = hg[0:tm]
        seq = jnp.concatenate([hg[tm:tm + halo] * keep_top, mid, hg[tm + halo:] * keep_bot], axis=0)
        cw = cw_ref[:, lo:lo + FF_CHUNK]
        prev = pltpu.roll(seq, 1, 0)[halo:halo + tm]
        nxt = pltpu.roll(seq, tm + 2 * halo - 1, 0)[halo:halo + tm]
        hc = prev * cw[0:1] + mid * cw[1:2] + nxt * cw[2:3] + cb_ref[:, lo:lo + FF_CHUNK]
        act = 0.5 * hc * (1.0 + lax.erf(hc * (2.0 ** -0.5)))
        act_ref[:, lo:lo + FF_CHUNK] = (act * hv).astype(BF16)
    for r in range(0, tm, FFN_OUT_ROWS):
        rows = slice(r, r + FFN_OUT_ROWS)
        y = jnp.dot(act_ref[rows, :], wd_ref[...], preferred_element_type=F32)
        o_ref[0, rows, :] = _rms(x[rows] + y, gf_ref[...])


def _ffn(x1, norm2_g, w_up, conv_w, conv_b, w_down, final_g):
    b, s, d = x1.shape
    tm = TOKEN_TILE
    hb = tm // SUBLANES
    n_hblocks = s // SUBLANES
    return pl.pallas_call(
        _ffn_kernel,
        grid=(b, s // tm),
        in_specs=[
            pl.BlockSpec((1, SUBLANES, d), lambda i, j: (i, jnp.maximum(j * hb - 1, 0), 0)),
            pl.BlockSpec((1, tm, d), lambda i, j: (i, j, 0)),
            pl.BlockSpec((1, SUBLANES, d), lambda i, j: (i, jnp.minimum((j + 1) * hb, n_hblocks - 1), 0)),
            _resident((1, d)),
            _resident(w_up.shape),
            _resident(conv_w.shape),
            _resident(conv_b.shape),
            _resident(w_down.shape),
            _resident((1, d)),
        ],
        out_specs=pl.BlockSpec((1, tm, d), lambda i, j: (i, j, 0)),
        out_shape=jax.ShapeDtypeStruct((b, s, d), F32),
        scratch_shapes=[pltpu.VMEM((tm, D_FF), BF16)],
        compiler_params=pltpu.CompilerParams(
            dimension_semantics=("parallel", "parallel"), vmem_limit_bytes=VMEM_LIMIT_BYTES),
        name="convglu_ffn",
    )(x1, x1, x1, norm2_g, w_up, conv_w, conv_b, w_down, final_g)


def _rotary_tables(s):
    inv = ROPE_THETA ** (-np.arange(0, RET_QK_DIM, 2, dtype=np.float64) / RET_QK_DIM)
    ang = np.arange(s, dtype=np.float64)[:, None] * inv[None, :]
    cos, sin = np.cos(ang), np.sin(ang)
    reps = LANES // RET_QK_DIM
    cos_tab = np.concatenate([cos, cos] * reps, axis=1).astype(np.float32)
    sin_tab = np.concatenate([-sin, sin] * reps, axis=1).astype(np.float32)
    assert cos_tab.shape == (s, LANES)
    return jnp.asarray(cos_tab), jnp.asarray(sin_tab)


def _encoder_layer(x, p):
    b, s, d = x.shape
    cos_tab, sin_tab = _rotary_tables(s)
    f, q, k, v, g_ret, g_a, g_b = _inproj(x, p["norm1_g"], p["w_in"], cos_tab, sin_tab)
    y_four = _fourier_mix(f)
    y_ret = _retention(q, k, v, g_ret, p["log_gamma"])
    return _merge(x, y_four, y_ret, g_a, g_b, p["w_four_proj"], p["w_ret_proj"], p["w_out"])


def kernel(x_prompt, x_sample, norm1_g, w_in, w_four_proj, w_ret_proj, w_out, ret_decay_logit,
           norm2_g, w_up, conv_w, conv_b, w_down, final_norm_g):
    depth = w_in.shape[0]
    assert depth == 1, "the final RMSNorm is fused into the (single) layer's FFN kernel"
    layers = []
    for l in range(depth):
        layers.append(dict(
            norm1_g=norm1_g[l][None, :],
            w_in=w_in[l].astype(BF16),
            w_four_proj=w_four_proj[l].astype(BF16),
            w_ret_proj=w_ret_proj[l].astype(BF16),
            w_out=w_out[l].astype(BF16),
            log_gamma=jax.nn.log_sigmoid(ret_decay_logit[l].astype(F32)),
            norm2_g=norm2_g[l][None, :],
            w_up=w_up[l].astype(BF16),
            conv_w=conv_w[l],
            conv_b=conv_b[l][None, :],
            w_down=w_down[l].astype(BF16),
        ))
    final_g = final_norm_g[None, :]

    def trunk(x):
        p = layers[0]
        x1 = _encoder_layer(x, p)
        return _ffn(x1, p["norm2_g"], p["w_up"], p["conv_w"], p["conv_b"], p["w_down"], final_g)

    return (trunk(x_prompt), trunk(x_sample))
```

```python
import functools

import numpy as np
import jax
import jax.numpy as jnp
from jax import lax
from jax.experimental import pallas as pl
from jax.experimental.pallas import tpu as pltpu

D_MODEL = 1024
N_FOURIER_GROUPS = 4
FOURIER_GROUP_DIM = 128
FOURIER_WIDTH = N_FOURIER_GROUPS * FOURIER_GROUP_DIM
N_RET_HEADS = 8
RET_QK_DIM = 64
RET_V_DIM = 128
RET_QK_WIDTH = N_RET_HEADS * RET_QK_DIM
RET_V_WIDTH = N_RET_HEADS * RET_V_DIM
CHUNK = 128
ROPE_THETA = 10000.0
D_FF = 2816
NORM_EPS = 1e-6
GN_EPS = 1e-5
IN_SPLITS = (FOURIER_WIDTH, RET_QK_WIDTH, RET_QK_WIDTH, RET_V_WIDTH, RET_V_WIDTH, D_MODEL, D_MODEL)
IN_OFFSETS = tuple(int(o) for o in np.cumsum((0,) + IN_SPLITS))
IN_WIDTH = IN_OFFSETS[-1]

LANES = 128
SUBLANES = 8
VMEM_LIMIT_BYTES = 56 * 1024 * 1024

F32 = jnp.float32
BF16 = jnp.bfloat16

TOKEN_TILE = 1024
INPROJ_ROWS = 512
FF_CHUNK = 256
FFN_OUT_ROWS = 256
FFT_N1 = 128
FFT_STAGE0_ROWS = 512
MERGE_COLS = 256
PITCH_PAD = SUBLANES


def _resident(shape):
    return pl.BlockSpec(shape, lambda *_: (0,) * len(shape), pipeline_mode=pl.Buffered(1))


def _rms(x, g):
    ms = jnp.mean(x * x, axis=-1, keepdims=True)
    return x * lax.rsqrt(ms + NORM_EPS) * g


def _inproj_kernel(x_ref, g_ref, w_ref, cos_ref, sin_ref,
                   f_ref, q_ref, k_ref, v_ref, gr_ref, ga_ref, gb_ref):
    reps = RET_QK_WIDTH // LANES
    lane = lax.broadcasted_iota(jnp.int32, (INPROJ_ROWS, RET_QK_WIDTH), 1)
    first_half = (lane % RET_QK_DIM) < (RET_QK_DIM // 2)

    for r in range(0, x_ref.shape[1], INPROJ_ROWS):
        rows = slice(r, r + INPROJ_ROWS)
        u = _rms(x_ref[0, rows, :], g_ref[...]).astype(BF16)

        def proj(i):
            return jnp.dot(u, w_ref[:, IN_OFFSETS[i]:IN_OFFSETS[i + 1]], preferred_element_type=F32)

        cos = jnp.concatenate([cos_ref[rows, :]] * reps, axis=1)
        sin = jnp.concatenate([sin_ref[rows, :]] * reps, axis=1)

        def rotary(t):
            partner = jnp.where(first_half,
                                pltpu.roll(t, RET_QK_WIDTH - RET_QK_DIM // 2, 1),
                                pltpu.roll(t, RET_QK_DIM // 2, 1))
            return t * cos + partner * sin

        def store_split(ref, val):
            parts, w = ref.shape[1], ref.shape[3]
            for p in range(parts):
                ref[0, p, rows, :] = val[:, p * w:(p + 1) * w].astype(BF16)

        store_split(f_ref, proj(0))
        store_split(q_ref, rotary(proj(1)) * (RET_QK_DIM ** -0.5))
        store_split(k_ref, rotary(proj(2)))
        store_split(v_ref, proj(3))
        store_split(gr_ref, proj(4))
        ga_ref[0, rows, :] = proj(5).astype(BF16)
        gb_ref[0, rows, :] = proj(6).astype(BF16)


def _inproj(x, norm_g, w_in_bf16, cos_tab, sin_tab):
    b, s, d = x.shape
    tm = TOKEN_TILE
    parts = (N_FOURIER_GROUPS,) + (N_RET_HEADS // 2,) * 4
    out_shape = [jax.ShapeDtypeStruct((b, p, s, w // p), BF16) for p, w in zip(parts, IN_SPLITS[:5])]
    out_specs = [pl.BlockSpec((1, p, tm, w // p), lambda i, j: (i, 0, j, 0))
                 for p, w in zip(parts, IN_SPLITS[:5])]
    out_shape += [jax.ShapeDtypeStruct((b, s, w), BF16) for w in IN_SPLITS[5:]]
    out_specs += [pl.BlockSpec((1, tm, w), lambda i, j: (i, j, 0)) for w in IN_SPLITS[5:]]
    return pl.pallas_call(
        _inproj_kernel,
        grid=(b, s // tm),
        in_specs=[
            pl.BlockSpec((1, tm, d), lambda i, j: (i, j, 0)),
            _resident((1, d)),
            _resident((d, IN_WIDTH)),
            pl.BlockSpec((tm, LANES), lambda i, j: (j, 0)),
            pl.BlockSpec((tm, LANES), lambda i, j: (j, 0)),
        ],
        out_specs=out_specs,
        out_shape=out_shape,
        compiler_params=pltpu.CompilerParams(
            dimension_semantics=("parallel", "parallel"), vmem_limit_bytes=VMEM_LIMIT_BYTES),
        name="inproj",
    )(x, norm_g, w_in_bf16, cos_tab, sin_tab)


def _fft_kernel(x_ref, fc_ref, f1_ref, g2_ref, o_ref, z_ref, y_ref, *, n1, n2):
    p1 = n2 + PITCH_PAD
    p2 = 2 * n1 + PITCH_PAD
    gd = FOURIER_GROUP_DIM

    slabs = FFT_STAGE0_ROWS // n2

    def stage0(i, carry):
        r0 = pl.multiple_of(i * FFT_STAGE0_ROWS, FFT_STAGE0_ROWS)
        z = jnp.dot(x_ref[0, pl.ds(r0, FFT_STAGE0_ROWS), :], fc_ref[...],
                    preferred_element_type=F32)
        for s in range(slabs):
            row = pl.multiple_of((i * slabs + s) * p1, SUBLANES)
            z_ref[0, pl.ds(row, n2), :] = z[s * n2:(s + 1) * n2, :gd]
            z_ref[1, pl.ds(row, n2), :] = z[s * n2:(s + 1) * n2, gd:]
        return carry

    lax.fori_loop(0, (n1 * n2) // FFT_STAGE0_ROWS, stage0, 0, unroll=16)

    def stage1(jj, carry):
        cols = []
        for t in range(2):
            j = 2 * jj + t
            zr = z_ref[0, pl.ds(j, n1, stride=p1), :]
            zi = z_ref[1, pl.ds(j, n1, stride=p1), :]
            cols.append(jnp.concatenate([zr, zi], axis=0).astype(BF16))
        rhs = jnp.concatenate(cols, axis=1)
        y = jnp.dot(f1_ref[...], rhs, preferred_element_type=F32)
        for t in range(2):
            row = pl.multiple_of((2 * jj + t) * p2, SUBLANES)
            y_ref[pl.ds(row, 2 * n1), :] = y[:, t * gd:(t + 1) * gd]
        return carry

    lax.fori_loop(0, n2 // 2, stage1, 0, unroll=32)

    def stage2(k1, carry):
        yr = y_ref[pl.ds(k1, n2, stride=p2), :]
        yi = y_ref[pl.ds(n1 + k1, n2, stride=p2), :]
        rhs = jnp.concatenate([yr, yi], axis=0).astype(BF16)
        out = jnp.dot(g2_ref[k1], rhs, preferred_element_type=F32)
        row = pl.multiple_of(k1 * SUBLANES, SUBLANES)
        o_ref[0, 0, :, pl.ds(row, SUBLANES), :] = out.reshape(n2 // SUBLANES, SUBLANES, gd)
        return carry

    lax.fori_loop(0, n1, stage2, 0, unroll=128)


def _fft_tables(s, n1, n2):
    c = FOURIER_GROUP_DIM
    ic = np.arange(c)
    ang_c = (2.0 * np.pi / c) * ((ic[:, None] * ic[None, :]) % c)
    fc = np.concatenate([np.cos(ang_c), -np.sin(ang_c)], axis=1) * c ** -0.5
    i1 = np.arange(n1)
    ang_1 = (2.0 * np.pi / n1) * ((i1[:, None] * i1[None, :]) % n1)
    c1, s1 = np.cos(ang_1), np.sin(ang_1)
    f1 = np.concatenate([np.concatenate([c1, s1], axis=1),
                         np.concatenate([-s1, c1], axis=1)], axis=0) * n1 ** -0.5
    k1 = np.arange(n1)[:, None, None]
    k2 = np.arange(n2)[None, :, None]
    m2 = np.arange(n2)[None, None, :]
    ang_2 = (2.0 * np.pi / s) * ((m2 * (k1 + n1 * k2)) % s)
    g2 = np.concatenate([np.cos(ang_2), np.sin(ang_2)], axis=2) * n2 ** -0.5
    return tuple(jnp.asarray(t.astype(np.float32)).astype(BF16) for t in (fc, f1, g2))


def _fourier_mix(f):
    b, _, s, _ = f.shape
    n1 = FFT_N1
    n2 = s // n1
    fc, f1, g2 = _fft_tables(s, n1, n2)
    gd = FOURIER_GROUP_DIM
    p1 = n2 + PITCH_PAD
    p2 = 2 * n1 + PITCH_PAD
    return pl.pallas_call(
        functools.partial(_fft_kernel, n1=n1, n2=n2),
        grid=(b, N_FOURIER_GROUPS),
        in_specs=[
            pl.BlockSpec((1, None, s, gd), lambda i, g: (i, g, 0, 0)),
            _resident((gd, 2 * gd)),
            _resident((2 * n1, 2 * n1)),
            _resident((n1, n2, 2 * n2)),
        ],
        out_specs=pl.BlockSpec((1, 1, n2 // SUBLANES, n1 * SUBLANES, gd), lambda i, g: (i, g, 0, 0, 0)),
        out_shape=jax.ShapeDtypeStruct((b, N_FOURIER_GROUPS, n2 // SUBLANES, n1 * SUBLANES, gd), F32),
        scratch_shapes=[
            pltpu.VMEM((2, n1 * p1, gd), F32),
            pltpu.VMEM((n2 * p2, gd), F32),
        ],
        compiler_params=pltpu.CompilerParams(
            dimension_semantics=("parallel", "parallel"), vmem_limit_bytes=VMEM_LIMIT_BYTES),
        name="fourier_mix",
    )(f, fc, f1, g2)


def _ret_kernel(lg_ref, q_ref, k_ref, v_ref, g_ref, o_ref,
                sf_ref, sb_ref, p_ref, fr_ref, br_ref, *, n_chunks):
    c = CHUNK
    dk2 = 2 * RET_QK_DIM
    dv = RET_V_DIM
    pair = pl.program_id(1)
    lgf = [lg_ref[0, 2 * pair + h] for h in range(2)]
    lgb = [lg_ref[1, 2 * pair + h] for h in range(2)]

    row = lax.broadcasted_iota(jnp.int32, (c, c), 0)
    col = lax.broadcasted_iota(jnp.int32, (c, c), 1)
    diff = (row - col).astype(F32)
    decay2 = jnp.concatenate(
        [jnp.where(diff >= 0.0,
                   jnp.exp(lgf[h] * jnp.maximum(diff, 0.0)),
                   jnp.exp(lgb[h] * jnp.maximum(-diff, 0.0))) for h in range(2)], axis=1)

    pos = lax.broadcasted_iota(jnp.int32, (c, dk2), 0).astype(F32)
    head_a = lax.broadcasted_iota(jnp.int32, (c, dk2), 1) < RET_QK_DIM

    def per_head(fa, fb):
        return jnp.where(head_a, fa, fb).astype(BF16)

    q_dec_f = per_head(jnp.exp(lgf[0] * (pos + 1.0)), jnp.exp(lgf[1] * (pos + 1.0)))
    q_dec_b = per_head(jnp.exp(lgb[0] * (c - pos)), jnp.exp(lgb[1] * (c - pos)))
    k_dec_f = per_head(jnp.exp(lgf[0] * (c - 1.0 - pos)), jnp.exp(lgf[1] * (c - 1.0 - pos)))
    k_dec_b = per_head(jnp.exp(lgb[0] * pos), jnp.exp(lgb[1] * pos))

    srow = lax.broadcasted_iota(jnp.int32, (dk2, 2 * dv), 0) < RET_QK_DIM
    scol = lax.broadcasted_iota(jnp.int32, (dk2, 2 * dv), 1) < dv
    block_diag = srow == scol
    chunk_f = jnp.where(scol, jnp.exp(lgf[0] * c), jnp.exp(lgf[1] * c))
    chunk_b = jnp.where(scol, jnp.exp(lgb[0] * c), jnp.exp(lgb[1] * c))

    def kv_update(state, k_dec, chunk_dec, kc, vc):
        kd = kc * k_dec
        kv = lax.dot_general(kd, vc, (((0,), (0,)), ((), ())), preferred_element_type=F32)
        return chunk_dec * state + jnp.where(block_diag, kv, 0.0)

    fr_ref[...] = jnp.zeros_like(fr_ref)
    br_ref[...] = jnp.zeros_like(br_ref)

    def scan_step(t, carry):
        for run_ref, out_ref, i, k_dec, chunk_dec in (
                (fr_ref, sf_ref, t, k_dec_f, chunk_f),
                (br_ref, sb_ref, n_chunks - 1 - t, k_dec_b, chunk_b)):
            r0 = pl.multiple_of(i * c, c)
            state = run_ref[...]
            out_ref[i] = state.astype(BF16)
            run_ref[...] = kv_update(state, k_dec, chunk_dec,
                                     k_ref[0, pl.ds(r0, c), :], v_ref[0, pl.ds(r0, c), :])
        r0 = pl.multiple_of(t * c, c)
        kc = k_ref[0, pl.ds(r0, c), :]
        zero = jnp.zeros_like(kc)
        k_heads = jnp.concatenate([jnp.where(head_a, kc, zero), jnp.where(head_a, zero, kc)], axis=0)
        scores = lax.dot_general(q_ref[0, pl.ds(r0, c), :], k_heads, (((1,), (1,)), ((), ())),
                                 preferred_element_type=F32)
        p_ref[t] = (scores * decay2).astype(BF16)
        return carry

    lax.fori_loop(0, n_chunks, scan_step, 0, unroll=64)

    def out_step(i, carry):
        r0 = pl.multiple_of(i * c, c)
        qc = q_ref[0, pl.ds(r0, c), :]
        vc = v_ref[0, pl.ds(r0, c), :]
        q_inter = jnp.concatenate([qc * q_dec_f, qc * q_dec_b], axis=1)
        states = jnp.concatenate([sf_ref[i], sb_ref[i]], axis=0)
        inter = jnp.dot(q_inter, states, preferred_element_type=F32)
        outs = []
        for h in range(2):
            o = jnp.dot(p_ref[i, :, h * c:(h + 1) * c], vc[:, h * dv:(h + 1) * dv],
                        preferred_element_type=F32)
            o = o + inter[:, h * dv:(h + 1) * dv]
            mu = jnp.mean(o, axis=-1, keepdims=True)
            d = o - mu
            var = jnp.mean(d * d, axis=-1, keepdims=True)
            outs.append(d * lax.rsqrt(var + GN_EPS))
        gate = g_ref[0, pl.ds(r0, c), :]
        o_ref[0, pl.ds(r0, c), :] = jnp.concatenate(outs, axis=1).astype(BF16) * (gate * jax.nn.sigmoid(gate))
        return carry

    lax.fori_loop(0, n_chunks, out_step, 0, unroll=8)


def _retention(q, k, v, g, log_gamma):
    b, _, s, _ = q.shape
    n_chunks = s // CHUNK
    n_pairs = N_RET_HEADS // 2
    dk2 = 2 * RET_QK_DIM
    dv2 = 2 * RET_V_DIM
    return pl.pallas_call(
        functools.partial(_ret_kernel, n_chunks=n_chunks),
        grid=(b, n_pairs),
        in_specs=[
            pl.BlockSpec(memory_space=pltpu.SMEM),
            pl.BlockSpec((1, None, s, dk2), lambda i, p: (i, p, 0, 0)),
            pl.BlockSpec((1, None, s, dk2), lambda i, p: (i, p, 0, 0)),
            pl.BlockSpec((1, None, s, dv2), lambda i, p: (i, p, 0, 0)),
            pl.BlockSpec((1, None, s, dv2), lambda i, p: (i, p, 0, 0)),
        ],
        out_specs=pl.BlockSpec((1, None, s, dv2), lambda i, p: (i, p, 0, 0)),
        out_shape=jax.ShapeDtypeStruct((b, n_pairs, s, dv2), BF16),
        scratch_shapes=[
            pltpu.VMEM((n_chunks, dk2, dv2), BF16),
            pltpu.VMEM((n_chunks, dk2, dv2), BF16),
            pltpu.VMEM((n_chunks, CHUNK, 2 * CHUNK), BF16),
            pltpu.VMEM((dk2, dv2), F32),
            pltpu.VMEM((dk2, dv2), F32),
        ],
        compiler_params=pltpu.CompilerParams(
            dimension_semantics=("parallel", "parallel"), vmem_limit_bytes=VMEM_LIMIT_BYTES),
        name="retention",
    )(log_gamma, q, k, v, g)


def _merge_kernel(yf_ref, ret_ref, ga_ref, gb_ref, wf_ref, wr_ref, wo_ref, o_ref, m_ref):
    n_groups = yf_ref.shape[1]
    k2_tile = SUBLANES
    n1 = yf_ref.shape[2] // k2_tile
    yf = jnp.concatenate(
        [jnp.concatenate([yf_ref[0, g, pl.ds(k2l, n1, stride=k2_tile), :] for g in range(n_groups)], axis=1)
         for k2l in range(k2_tile)], axis=0)
    yf = yf.astype(BF16)
    ret = jnp.concatenate([ret_ref[0, p] for p in range(ret_ref.shape[1])], axis=1)
    for lo in range(0, m_ref.shape[1], MERGE_COLS):
        cols = slice(lo, lo + MERGE_COLS)
        a = jnp.dot(yf, wf_ref[:, cols], preferred_element_type=F32)
        bb = jnp.dot(ret, wr_ref[:, cols], preferred_element_type=F32)
        m_ref[:, cols] = (jax.nn.sigmoid(ga_ref[0, :, cols].astype(F32)) * a
                          + jax.nn.sigmoid(gb_ref[0, :, cols].astype(F32)) * bb).astype(BF16)
    o_ref[0] = jnp.dot(m_ref[...], wo_ref[...], preferred_element_type=F32).astype(BF16)


def _merge(y_four, y_ret, g_a, g_b, wf, wr, wo):
    b, s, d = g_a.shape
    _, n_groups, _, tm, gd = y_four.shape

    def tile(w):
        return pl.BlockSpec((1, tm, w), lambda i, j: (i, j, 0))

    return pl.pallas_call(
        _merge_kernel,
        grid=(b, s // tm),
        in_specs=[pl.BlockSpec((1, n_groups, None, tm, gd), lambda i, j: (i, 0, j, 0, 0)),
                  pl.BlockSpec((1,) + y_ret.shape[1:2] + (tm,) + y_ret.shape[3:], lambda i, j: (i, 0, j, 0)),
                  tile(d), tile(d),
                  _resident(wf.shape), _resident(wr.shape), _resident(wo.shape)],
        out_specs=tile(d),
        out_shape=jax.ShapeDtypeStruct((b, s, d), BF16),
        scratch_shapes=[pltpu.VMEM((tm, d), BF16)],
        compiler_params=pltpu.CompilerParams(
            dimension_semantics=("parallel", "parallel"), vmem_limit_bytes=VMEM_LIMIT_BYTES),
        name="merge",
    )(y_four, y_ret, g_a, g_b, wf, wr, wo)


def _ffn_kernel(xp_ref, x_ref, xn_ref, dp_ref, d_ref, dn_ref, g2_ref, wu_ref, cw_ref, cb_ref, wd_ref, gf_ref,
                o_ref, act_ref):
    tm = x_ref.shape[1]
    halo = SUBLANES
    j = pl.program_id(1)
    last = pl.num_programs(1) - 1
    x = x_ref[0] + d_ref[0].astype(F32)
    x_prev = xp_ref[0] + dp_ref[0, halo:2 * halo].astype(F32)
    x_next = xn_ref[0] + dn_ref[0, 0:halo].astype(F32)
    xx = jnp.concatenate([x, x_prev, x_next], axis=0)
    u = _rms(xx, g2_ref[...]).astype(BF16)
    u_mid = u[0:tm]
    keep_top = (j > 0).astype(F32)
    keep_bot = (j < last).astype(F32)

    for ci in range(D_FF // FF_CHUNK):
        lo = ci * FF_CHUNK
        hg = jnp.dot(u, wu_ref[:, lo:lo + FF_CHUNK], preferred_element_type=F32)
        hv = jnp.dot(u_mid, wu_ref[:, D_FF + lo:D_FF + lo + FF_CHUNK], preferred_element_type=F32)
        mid = hg[0:tm]
        seq = jnp.concatenate([hg[tm:tm + halo] * keep_top, mid, hg[tm + halo:] * keep_bot], axis=0)
        cw = cw_ref[:, lo:lo + FF_CHUNK]
        prev = pltpu.roll(seq, 1, 0)[halo:halo + tm]
        nxt = pltpu.roll(seq, tm + 2 * halo - 1, 0)[halo:halo + tm]
        hc = prev * cw[0:1] + mid * cw[1:2] + nxt * cw[2:3] + cb_ref[:, lo:lo + FF_CHUNK]
        act = 0.5 * hc * (1.0 + lax.erf(hc * (2.0 ** -0.5)))
        act_ref[:, lo:lo + FF_CHUNK] = (act * hv).astype(BF16)
    for r in range(0, tm, FFN_OUT_ROWS):
        rows = slice(r, r + FFN_OUT_ROWS)
        y = jnp.dot(act_ref[rows, :], wd_ref[...], preferred_element_type=F32)
        o_ref[0, rows, :] = _rms(x[rows] + y, gf_ref[...])


def _ffn(x, delta, norm2_g, w_up, conv_w, conv_b, w_down, final_g):
    b, s, d = x.shape
    tm = TOKEN_TILE
    hb = tm // SUBLANES
    n_hblocks = s // SUBLANES
    halo16 = 2 * SUBLANES
    hb16 = tm // halo16
    n_h16 = s // halo16
    return pl.pallas_call(
        _ffn_kernel,
        grid=(b, s // tm),
        in_specs=[
            pl.BlockSpec((1, SUBLANES, d), lambda i, j: (i, jnp.maximum(j * hb - 1, 0), 0)),
            pl.BlockSpec((1, tm, d), lambda i, j: (i, j, 0)),
            pl.BlockSpec((1, SUBLANES, d), lambda i, j: (i, jnp.minimum((j + 1) * hb, n_hblocks - 1), 0)),
            pl.BlockSpec((1, halo16, d), lambda i, j: (i, jnp.maximum(j * hb16 - 1, 0), 0)),
            pl.BlockSpec((1, tm, d), lambda i, j: (i, j, 0)),
            pl.BlockSpec((1, halo16, d), lambda i, j: (i, jnp.minimum((j + 1) * hb16, n_h16 - 1), 0)),
            _resident((1, d)),
            _resident(w_up.shape),
            _resident(conv_w.shape),
            _resident(conv_b.shape),
            _resident(w_down.shape),
            _resident((1, d)),
        ],
        out_specs=pl.BlockSpec((1, tm, d), lambda i, j: (i, j, 0)),
        out_shape=jax.ShapeDtypeStruct((b, s, d), F32),
        scratch_shapes=[pltpu.VMEM((tm, D_FF), BF16)],
        compiler_params=pltpu.CompilerParams(
            dimension_semantics=("parallel", "parallel"), vmem_limit_bytes=VMEM_LIMIT_BYTES),
        name="convglu_ffn",
    )(x, x, x, delta, delta, delta, norm2_g, w_up, conv_w, conv_b, w_down, final_g)


def _rotary_tables(s):
    inv = ROPE_THETA ** (-np.arange(0, RET_QK_DIM, 2, dtype=np.float64) / RET_QK_DIM)
    ang = np.arange(s, dtype=np.float64)[:, None] * inv[None, :]
    cos, sin = np.cos(ang), np.sin(ang)
    reps = LANES // RET_QK_DIM
    cos_tab = np.concatenate([cos, cos] * reps, axis=1).astype(np.float32)
    sin_tab = np.concatenate([-sin, sin] * reps, axis=1).astype(np.float32)
    assert cos_tab.shape == (s, LANES)
    return jnp.asarray(cos_tab), jnp.asarray(sin_tab)


def _encoder_layer(x, p):
    b, s, d = x.shape
    cos_tab, sin_tab = _rotary_tables(s)
    f, q, k, v, g_ret, g_a, g_b = _inproj(x, p["norm1_g"], p["w_in"], cos_tab, sin_tab)
    y_four = _fourier_mix(f)
    y_ret = _retention(q, k, v, g_ret, p["log_gamma"])
    return _merge(y_four, y_ret, g_a, g_b, p["w_four_proj"], p["w_ret_proj"], p["w_out"])


def kernel(x_prompt, x_sample, norm1_g, w_in, w_four_proj, w_ret_proj, w_out, ret_decay_logit,
           norm2_g, w_up, conv_w, conv_b, w_down, final_norm_g):
    depth = w_in.shape[0]
    assert depth == 1, "the final RMSNorm is fused into the (single) layer's FFN kernel"
    layers = []
    for l in range(depth):
        layers.append(dict(
            norm1_g=norm1_g[l][None, :],
            w_in=w_in[l].astype(BF16),
            w_four_proj=w_four_proj[l].astype(BF16),
            w_ret_proj=w_ret_proj[l].astype(BF16),
            w_out=w_out[l].astype(BF16),
            log_gamma=jax.nn.log_sigmoid(ret_decay_logit[l].astype(F32)),
            norm2_g=norm2_g[l][None, :],
            w_up=w_up[l].astype(BF16),
            conv_w=conv_w[l],
            conv_b=conv_b[l][None, :],
            w_down=w_down[l].astype(BF16),
        ))
    final_g = final_norm_g[None, :]

    def trunk(x):
        p = layers[0]
        delta = _encoder_layer(x, p)
        return _ffn(x, delta, p["norm2_g"], p["w_up"], p["conv_w"], p["conv_b"], p["w_down"], final_g)

    return (trunk(x_prompt), trunk(x_sample))
```
